```python
import math
import jax, jax.numpy as jnp
from jax import lax
import numpy as np

D_MODEL = 1024
BATCH = 4
SEQ = 4096
DEPTH = 1

RWKV_HEADS = 8
RWKV_HEAD = 64
RWKV_W = RWKV_HEADS * RWKV_HEAD
DIFF_HEADS = 4
DIFF_QK = 64
DIFF_V = 2 * DIFF_QK
DIFF_W = DIFF_HEADS * DIFF_V
MIX_W = RWKV_W + DIFF_W
DECAY_LORA = 64
ICLR_LORA = 64
GATE_LORA = 160
RWKV_COLS = [RWKV_W, RWKV_W, RWKV_W, DECAY_LORA, ICLR_LORA, GATE_LORA]
SHIFT_COLS = sum(RWKV_COLS)
DIFF_COLS = [DIFF_HEADS * 2 * DIFF_QK, DIFF_HEADS * 2 * DIFF_QK, DIFF_W]
PROJ_W = SHIFT_COLS + sum(DIFF_COLS)
D_FF = 2816
FFN_RES = 0.5
Q_BLOCK = 128
NORM_EPS = 1e-6
GN_EPS = 64e-5
SUBLN_EPS = 1e-5

kernel_name = "hybrid_rwkv7_diffattn_macaron"


def _split(t, widths):
    idx = [int(v) for v in np.cumsum(widths)[:-1]]
    return jnp.split(t, idx, axis=-1)


def rmsnorm(x, g, eps=NORM_EPS):
    xf = x.astype(jnp.float32)
    y = xf * lax.rsqrt(jnp.mean(xf * xf, axis=-1, keepdims=True) + eps)
    return (y * g.astype(jnp.float32)).astype(x.dtype)


def swiglu(x, w_gate, w_up, w_down):
    return (jax.nn.silu(x @ w_gate) * (x @ w_up)) @ w_down


def token_shift(p):
    return jnp.pad(p, ((0, 0), (1, 0), (0, 0)))[:, :-1]


def rwkv7_scan(r, w, k, v, a, b):
    B, T, H, N = r.shape

    def step(S, inp):
        r_t, w_t, k_t, v_t, a_t, b_t = inp
        sa = jnp.einsum('bhij,bhj->bhi', S, a_t)
        S = (S * w_t[:, :, None, :] + sa[..., None] * b_t[:, :, None, :]
             + v_t[..., None] * k_t[:, :, None, :])
        y = jnp.einsum('bhij,bhj->bhi', S, r_t)
        return S, y

    S0 = jnp.zeros((B, H, N, N), jnp.float32)
    xs = tuple(jnp.moveaxis(t, 1, 0) for t in (r, w, k, v, a, b))
    _, ys = lax.scan(step, S0, xs)
    return jnp.moveaxis(ys, 0, 1)


def rwkv7_group(p_r, p_k, p_v, p_wd, p_ad, p_gd, w0, w_up, a0, a_up, g_up,
                k_k, k_a, r_k, gn_w, gn_b):
    B, T, _ = p_r.shape
    H, N = RWKV_HEADS, RWKV_HEAD
    f32 = jnp.float32
    logw = -jax.nn.softplus(-(w0 + jnp.tanh(p_wd) @ w_up)) - 0.5
    decay = jnp.exp(-jnp.exp(logw.astype(f32)))
    iclr = jax.nn.sigmoid(a0 + p_ad @ a_up)
    gate = jax.nn.sigmoid(p_gd) @ g_up
    kk = (p_k * k_k).reshape(B, T, H, N).astype(f32)
    kk = kk * lax.rsqrt(jnp.maximum(jnp.sum(kk * kk, -1, keepdims=True), 1e-24))
    k = p_k * (1.0 + (iclr - 1.0) * k_a)
    hs = lambda t: t.reshape(B, T, H, N).astype(f32)
    r_h, k_h, v_h, a_h = hs(p_r), hs(k), hs(p_v), hs(iclr)
    y = rwkv7_scan(r_h, hs(decay), k_h, v_h, -kk, kk * a_h)
    mu = jnp.mean(y, -1, keepdims=True)
    var = jnp.mean(jnp.square(y - mu), -1, keepdims=True)
    y = (y - mu) * lax.rsqrt(var + GN_EPS)
    y = y * gn_w.reshape(H, N).astype(f32) + gn_b.reshape(H, N).astype(f32)
    y = y + jnp.sum(r_h * k_h * r_k.astype(f32), -1, keepdims=True) * v_h
    y = y.reshape(B, T, RWKV_W)
    return (y * gate.astype(f32)).astype(p_r.dtype)


def diff_attention_group(q, k, v, lam_q1, lam_k1, lam_q2, lam_k2, subln_w, lambda_init):
    B, T, _ = q.shape
    H, D = DIFF_HEADS, DIFF_QK
    f32 = jnp.float32
    nb = T // Q_BLOCK
    qh = q.reshape(B, nb, Q_BLOCK, H, 2, D).transpose(1, 0, 2, 3, 4, 5).astype(f32)
    kh = k.reshape(B, T, H, 2, D).astype(f32)
    vh = v.reshape(B, T, H, DIFF_V).astype(f32)
    lam = (jnp.exp(jnp.sum(lam_q1.astype(f32) * lam_k1.astype(f32)))
           - jnp.exp(jnp.sum(lam_q2.astype(f32) * lam_k2.astype(f32))) + lambda_init)
    scale = D ** -0.5
    key_pos = jnp.arange(T)

    def block(args):
        qi, i = args
        s = jnp.einsum('bqhcd,bkhcd->bhcqk', qi, kh) * scale
        q_pos = i * Q_BLOCK + jnp.arange(Q_BLOCK)
        mask = key_pos[None, :] <= q_pos[:, None]
        s = jnp.where(mask, s, -jnp.inf)
        p = jax.nn.softmax(s, axis=-1)
        pd = p[:, :, 0] - lam * p[:, :, 1]
        return jnp.einsum('bhqk,bkhe->bqhe', pd, vh)

    o = lax.map(block, (qh, jnp.arange(nb)))
    o = jnp.moveaxis(o, 0, 1).reshape(B, T, H, DIFF_V)
    o = o * lax.rsqrt(jnp.mean(o * o, -1, keepdims=True) + SUBLN_EPS) * subln_w.astype(f32)
    o = o * (1.0 - lambda_init)
    return o.reshape(B, T, DIFF_W).astype(q.dtype)


def setup_inputs(seed: int = 0) -> dict:
    key = jax.random.key(seed)
    ks = iter(jax.random.split(key, 48))
    L, D, F = DEPTH, D_MODEL, D_FF
    nrm = lambda shape, s: jax.random.normal(next(ks), shape, jnp.float32) * s
    gain = lambda shape: 1.0 + nrm(shape, 0.02)
    n = jnp.arange(RWKV_W, dtype=jnp.float32) / (RWKV_W - 1)
    decay_speed = -7.0 + 5.0 * n ** 0.85 + 0.5
    return {
        "x": jax.random.normal(next(ks), (BATCH, SEQ, D), jnp.float32),
        "ffn1_pre_g": gain((L, D)),
        "ffn1_post_g": gain((L, D)),
        "ffn1_w_gate": nrm((L, D, F), D ** -0.5),
        "ffn1_w_up": nrm((L, D, F), D ** -0.5),
        "ffn1_w_down": nrm((L, F, D), F ** -0.5),
        "mix_pre_g": gain((L, D)),
        "mix_post_g": gain((L, D)),
        "w_in": nrm((L, D, PROJ_W), D ** -0.5),
        "shift_mu": jax.random.uniform(next(ks), (L, SHIFT_COLS), jnp.float32),
        "w_o": nrm((L, MIX_W, D), MIX_W ** -0.5),
        "rwkv_w0": decay_speed[None, :] + nrm((L, RWKV_W), 0.1),
        "rwkv_w_up": nrm((L, DECAY_LORA, RWKV_W), 0.5 * DECAY_LORA ** -0.5),
        "rwkv_a0": nrm((L, RWKV_W), 0.1),
        "rwkv_a_up": nrm((L, ICLR_LORA, RWKV_W), 0.5 * ICLR_LORA ** -0.5),
        "rwkv_g_up": nrm((L, GATE_LORA, RWKV_W), GATE_LORA ** -0.5),
        "rwkv_k_k": 0.85 + nrm((L, RWKV_W), 0.02),
        "rwkv_k_a": gain((L, RWKV_W)),
        "rwkv_r_k": nrm((L, RWKV_HEADS, RWKV_HEAD), 0.1),
        "rwkv_gn_w": gain((L, RWKV_W)),
        "rwkv_gn_b": nrm((L, RWKV_W), 0.02),
        "diff_lam_q1": nrm((L, DIFF_QK), 0.1),
        "diff_lam_k1": nrm((L, DIFF_QK), 0.1),
        "diff_lam_q2": nrm((L, DIFF_QK), 0.1),
        "diff_lam_k2": nrm((L, DIFF_QK), 0.1),
        "diff_subln_w": gain((L, DIFF_V)),
        "ffn2_pre_g": gain((L, D)),
        "ffn2_post_g": gain((L, D)),
        "ffn2_w_gate": nrm((L, D, F), D ** -0.5),
        "ffn2_w_up": nrm((L, D, F), D ** -0.5),
        "ffn2_w_down": nrm((L, F, D), F ** -0.5),
    }


def reference(x, ffn1_pre_g, ffn1_post_g, ffn1_w_gate, ffn1_w_up, ffn1_w_down,
              mix_pre_g, mix_post_g, w_in, shift_mu, w_o,
              rwkv_w0, rwkv_w_up, rwkv_a0, rwkv_a_up, rwkv_g_up, rwkv_k_k, rwkv_k_a,
              rwkv_r_k, rwkv_gn_w, rwkv_gn_b,
              diff_lam_q1, diff_lam_k1, diff_lam_q2, diff_lam_k2, diff_subln_w,
              ffn2_pre_g, ffn2_post_g, ffn2_w_gate, ffn2_w_up, ffn2_w_down):
    for l in range(DEPTH):
        lambda_init = 0.8 - 0.6 * math.exp(-0.3 * l)
        h = rmsnorm(x, ffn1_pre_g[l])
        x = x + FFN_RES * rmsnorm(swiglu(h, ffn1_w_gate[l], ffn1_w_up[l], ffn1_w_down[l]), ffn1_post_g[l])
        h = rmsnorm(x, mix_pre_g[l])
        p = h @ w_in[l]
        p_rw, p_diff = p[..., :SHIFT_COLS], p[..., SHIFT_COLS:]
        p_rw = p_rw + (token_shift(p_rw) - p_rw) * shift_mu[l]
        p_r, p_k, p_v, p_wd, p_ad, p_gd = _split(p_rw, RWKV_COLS)
        q_d, k_d, v_d = _split(p_diff, DIFF_COLS)
        y_rwkv = rwkv7_group(p_r, p_k, p_v, p_wd, p_ad, p_gd,
                             rwkv_w0[l], rwkv_w_up[l], rwkv_a0[l], rwkv_a_up[l], rwkv_g_up[l],
                             rwkv_k_k[l], rwkv_k_a[l], rwkv_r_k[l], rwkv_gn_w[l], rwkv_gn_b[l])
        y_diff = diff_attention_group(q_d, k_d, v_d, diff_lam_q1[l], diff_lam_k1[l],
                                      diff_lam_q2[l], diff_lam_k2[l], diff_subln_w[l], lambda_init)
        y = jnp.concatenate([y_rwkv, y_diff], axis=-1) @ w_o[l]
        x = x + rmsnorm(y, mix_post_g[l])
        h = rmsnorm(x, ffn2_pre_g[l])
        x = x + FFN_RES * rmsnorm(swiglu(h, ffn2_w_gate[l], ffn2_w_up[l], ffn2_w_down[l]), ffn2_post_g[l])
    return x
```

```python
import functools
import math

import jax
import jax.numpy as jnp
from jax import lax
from jax.experimental import pallas as pl
from jax.experimental.pallas import tpu as pltpu

F32 = jnp.float32
BF16 = jnp.bfloat16

RWKV_HEADS = 8
RWKV_HEAD = 64
RWKV_W = RWKV_HEADS * RWKV_HEAD
DIFF_HEADS = 4
DIFF_QK = 64
DIFF_V = 2 * DIFF_QK
DIFF_W = DIFF_HEADS * DIFF_V
DECAY_LORA = 64
ICLR_LORA = 64
GATE_LORA = 160
LORA_W = DECAY_LORA + ICLR_LORA + GATE_LORA
LANES = 128
LORA_PAD = -(-LORA_W // LANES) * LANES
SHIFT_COLS = 3 * RWKV_W + LORA_W
SHIFT_PAD = 3 * RWKV_W + LORA_PAD
FFN_RES = 0.5
NORM_EPS = 1e-6
GN_EPS = 64e-5
SUBLN_EPS = 1e-5
NEG_BIG = -1e30
VMEM_LIMIT = 56 * 1024 * 1024

_NT = (((1,), (1,)), ((), ()))
_TN = (((0,), (0,)), ((), ()))


def _dot(a, b):
    return jnp.dot(a, b, preferred_element_type=F32)


def _split_dot(x, w_bf16):
    hi = x.astype(BF16)
    lo = (x - hi.astype(F32)).astype(BF16)
    return _dot(hi, w_bf16) + _dot(lo, w_bf16)


def _rms(x, g, eps):
    return x * lax.rsqrt(jnp.mean(x * x, axis=-1, keepdims=True) + eps) * g


def _ffn_body(x_ref, gpre_ref, gpost_ref, wg_ref, wu_ref, wd_ref, o_ref, h_ref, acc_ref):
    f = pl.program_id(1)

    @pl.when(f == 0)
    def _():
        h_ref[...] = _rms(x_ref[...], gpre_ref[...], NORM_EPS).astype(BF16)
        acc_ref[...] = jnp.zeros_like(acc_ref)

    h = h_ref[...]
    g = _dot(h, wg_ref[...])
    u = _dot(h, wu_ref[...])
    a = (g * jax.nn.sigmoid(g) * u).astype(BF16)
    acc_ref[...] += _dot(a, wd_ref[...])

    @pl.when(f == pl.num_programs(1) - 1)
    def _():
        o_ref[...] = x_ref[...] + FFN_RES * _rms(acc_ref[...], gpost_ref[...], NORM_EPS)


def _ffn(x, g_pre, g_post, w_gate, w_up, w_down, *, tm, tf):
    m, d = x.shape
    f = w_gate.shape[1]
    return pl.pallas_call(
        _ffn_body,
        grid=(m // tm, f // tf),
        in_specs=[
            pl.BlockSpec((tm, d), lambda i, j: (i, 0)),
            pl.BlockSpec((1, d), lambda i, j: (0, 0)),
            pl.BlockSpec((1, d), lambda i, j: (0, 0)),
            pl.BlockSpec((d, tf), lambda i, j: (0, j)),
            pl.BlockSpec((d, tf), lambda i, j: (0, j)),
            pl.BlockSpec((tf, d), lambda i, j: (j, 0)),
        ],
        out_specs=pl.BlockSpec((tm, d), lambda i, j: (i, 0)),
        out_shape=jax.ShapeDtypeStruct((m, d), F32),
        scratch_shapes=[pltpu.VMEM((tm, d), BF16), pltpu.VMEM((tm, d), F32)],
        compiler_params=pltpu.CompilerParams(
            dimension_semantics=("parallel", "arbitrary"), vmem_limit_bytes=VMEM_LIMIT),
        name="ffn",
    )(x, g_pre.reshape(1, d), g_post.reshape(1, d), w_gate.astype(BF16), w_up.astype(BF16),
      w_down.astype(BF16))


def _proj_body(x_ref, g_ref, w_ref, mu_ref, wup_ref, ones_ref, w0_ref, a0_ref, kk_ref, ka_ref,
               rk_ref,
               r_out, ld_out, k_out, v_out, kk_out, ic_out, bonus_out, gate_out,
               qd_out, kd_out, vd_out, carry_ref):
    t = pl.program_id(1)

    @pl.when(t == 0)
    def _():
        carry_ref[...] = jnp.zeros_like(carry_ref)

    h = _rms(x_ref[0], g_ref[...], NORM_EPS).astype(BF16)
    p = _dot(h, w_ref[...])
    tm = p.shape[0]

    qd_out[0] = (p[:, SHIFT_PAD:SHIFT_PAD + DIFF_W] * (DIFF_QK ** -0.5)).astype(BF16)
    kd_out[0] = p[:, SHIFT_PAD + DIFF_W:SHIFT_PAD + 2 * DIFF_W].astype(BF16)
    vd_out[0] = p[:, SHIFT_PAD + 2 * DIFF_W:].astype(BF16)

    ps = p[:, :SHIFT_PAD]
    row = lax.broadcasted_iota(jnp.int32, (tm, 1), 0)
    prev = jnp.where(row == 0, carry_ref[...], pltpu.roll(ps, 1, 0))
    carry_ref[...] = ps[tm - 1:tm, :]
    ps = ps + (prev - ps) * mu_ref[...]

    p_r = ps[:, :RWKV_W]
    p_k = ps[:, RWKV_W:2 * RWKV_W]
    p_v = ps[:, 2 * RWKV_W:3 * RWKV_W]
    z = ps[:, 3 * RWKV_W:]
    lane = lax.broadcasted_iota(jnp.int32, (1, LORA_PAD), 1)
    act = jnp.where(lane < DECAY_LORA, jnp.tanh(z),
                    jnp.where(lane < DECAY_LORA + ICLR_LORA, z, jax.nn.sigmoid(z)))
    up = _dot(act.astype(BF16), wup_ref[...])
    zw = -(w0_ref[...] + up[:, :RWKV_W])
    softplus = jnp.maximum(zw, 0.0) + jnp.log(1.0 + jnp.exp(-jnp.abs(zw)))
    ld = -jnp.exp(-softplus - 0.5)
    iclr = jax.nn.sigmoid(a0_ref[...] + up[:, RWKV_W:2 * RWKV_W])
    gate = up[:, 2 * RWKV_W:]

    ones_bd = ones_ref[...]
    kk = p_k * kk_ref[...]
    kk = kk * lax.rsqrt(jnp.maximum(_split_dot(kk * kk, ones_bd), 1e-24))
    k = p_k * (1.0 + (iclr - 1.0) * ka_ref[...])
    bonus = _split_dot(p_r * k * rk_ref[...], ones_bd) * p_v

    r_out[0] = p_r
    ld_out[0] = ld
    k_out[0] = k
    v_out[0] = p_v
    kk_out[0] = kk
    ic_out[0] = iclr
    bonus_out[0] = bonus
    gate_out[0] = gate


def _head_ones(width, head):
    i = jnp.arange(width) // head
    return (i[:, None] == i[None, :]).astype(BF16)


def _proj(x, g, w_in, shift_mu, w_up, a_up, g_up, w0, a0, k_k, k_a, r_k, *, tm):
    b, t, d = x.shape
    pad = SHIFT_PAD - SHIFT_COLS
    w = jnp.concatenate(
        [w_in[:, :SHIFT_COLS], jnp.zeros((d, pad), F32), w_in[:, SHIFT_COLS:]], axis=1).astype(BF16)
    mu = jnp.concatenate([shift_mu, jnp.zeros((pad,), F32)]).reshape(1, SHIFT_PAD)
    wup = jnp.zeros((LORA_PAD, 3 * RWKV_W), F32)
    wup = wup.at[:DECAY_LORA, :RWKV_W].set(w_up)
    wup = wup.at[DECAY_LORA:DECAY_LORA + ICLR_LORA, RWKV_W:2 * RWKV_W].set(a_up)
    wup = wup.at[DECAY_LORA + ICLR_LORA:LORA_W, 2 * RWKV_W:].set(g_up)
    wup = wup.astype(BF16)
    wcols = w.shape[1]
    vec = lambda a: a.reshape(1, RWKV_W)
    const = lambda shape: pl.BlockSpec(shape, lambda i, j: (0,) * len(shape))
    tile = lambda width: pl.BlockSpec((1, tm, width), lambda i, j: (i, j, 0))
    f32_out = jax.ShapeDtypeStruct((b, t, RWKV_W), F32)
    bf_out = jax.ShapeDtypeStruct((b, t, DIFF_W), BF16)
    return pl.pallas_call(
        _proj_body,
        grid=(b, t // tm),
        in_specs=[tile(d), const((1, d)), const((d, wcols)), const((1, SHIFT_PAD)),
                  const((LORA_PAD, 3 * RWKV_W)), const((RWKV_W, RWKV_W))] + [const((1, RWKV_W))] * 5,
        out_specs=[tile(RWKV_W)] * 8 + [tile(DIFF_W)] * 3,
        out_shape=[f32_out] * 8 + [bf_out] * 3,
        scratch_shapes=[pltpu.VMEM((1, SHIFT_PAD), F32)],
        compiler_params=pltpu.CompilerParams(
            dimension_semantics=("arbitrary", "arbitrary"), vmem_limit_bytes=VMEM_LIMIT),
        name="proj",
    )(x, g.reshape(1, d), w, mu, wup, _head_ones(RWKV_W, RWKV_HEAD),
      vec(w0), vec(a0), vec(k_k), vec(k_a), vec(r_k))


def _rwkv_body(r_ref, ld_ref, k_ref, v_ref, kk_ref, ic_ref, bonus_ref, gate_ref, gnw_ref, gnb_ref,
               o_ref, h_ref, *, chunk):
    c = chunk
    c2 = 2 * c

    @pl.when(pl.program_id(1) == 0)
    def _():
        h_ref[...] = jnp.zeros_like(h_ref)

    ld = ld_ref[0]
    ri = lax.broadcasted_iota(jnp.int32, (c, c), 0)
    ci = lax.broadcasted_iota(jnp.int32, (c, c), 1)
    tril = (ci <= ri).astype(F32)
    cum = jnp.dot(tril, ld, preferred_element_type=F32, precision=lax.Precision.HIGHEST)
    tot = cum[c - 1:c, :]
    g_in = jnp.exp(cum)
    g_inv = jnp.exp(-cum)
    g_prev = jnp.exp(cum - ld)
    g_rest = jnp.exp(tot - cum)
    g_tot = jnp.exp(tot)

    kk = kk_ref[0]
    b = kk * ic_ref[0]
    k = k_ref[0]
    at = (-kk * g_prev).astype(BF16)
    rt = (r_ref[0] * g_in).astype(BF16)
    bt = (b * g_inv).astype(BF16)
    kt = (k * g_inv).astype(BF16)
    bh = (b * g_rest).astype(BF16)
    kh = (k * g_rest).astype(BF16)
    v = v_ref[0].astype(BF16)

    lane = lax.broadcasted_iota(jnp.int32, (1, LANES), 1)
    lo = lane < RWKV_HEAD

    def stack(x):
        zero = jnp.zeros_like(x)
        return jnp.concatenate([jnp.where(lo, x, zero), jnp.where(lo, zero, x)], axis=0)

    row2 = lax.broadcasted_iota(jnp.int32, (c2, c2), 0)
    col2 = lax.broadcasted_iota(jnp.int32, (c2, c2), 1)
    same = (row2 // c) == (col2 // c)
    strict = same & (col2 < row2)
    incl = same & (col2 <= row2)
    eye = (row2 == col2).astype(F32)
    levels = int(math.log2(c))
    level_masks = [((row2 >> l) == (col2 >> l)) & ((row2 >> (l - 1)) != (col2 >> (l - 1)))
                   & (col2 < row2) for l in range(1, levels + 1)]
    ones_bd = ((lax.broadcasted_iota(jnp.int32, (LANES, LANES), 0) // RWKV_HEAD)
               == (lax.broadcasted_iota(jnp.int32, (LANES, LANES), 1) // RWKV_HEAD)).astype(BF16)
    nt = lambda x, y: lax.dot_general(x, y, _NT, preferred_element_type=F32)
    tn = lambda x, y: lax.dot_general(x, y, _TN, preferred_element_type=F32)

    for p in range(RWKV_HEADS // 2):
        sl = slice(p * LANES, (p + 1) * LANES)
        a2, r2, b2, k2 = stack(at[:, sl]), stack(rt[:, sl]), stack(bt[:, sl]), stack(kt[:, sl])
        bh2, kh2, v2 = stack(bh[:, sl]), stack(kh[:, sl]), stack(v[:, sl])
        s0 = h_ref[p]
        s0b = s0.astype(BF16)

        a_ab = jnp.where(strict, nt(a2, b2), 0.0)
        a_ak = jnp.where(strict, nt(a2, k2), 0.0)
        a_rb = jnp.where(incl, nt(r2, b2), 0.0)
        a_rk = jnp.where(incl, nt(r2, k2), 0.0)

        tinv = eye + jnp.where(level_masks[0], a_ab, 0.0)
        for l in range(2, levels + 1):
            e = jnp.where(level_masks[l - 1], a_ab, 0.0).astype(BF16)
            tb = tinv.astype(BF16)
            tinv = tinv + _dot(_dot(tb, e).astype(BF16), tb)

        rhs = nt(a2, s0b) + _dot(a_ak.astype(BF16), v2)
        u2b = _dot(tinv.astype(BF16), rhs.astype(BF16)).astype(BF16)
        y2 = nt(r2, s0b) + _dot(a_rb.astype(BF16), u2b) + _dot(a_rk.astype(BF16), v2)
        h_ref[p] = g_tot[:, sl] * s0 + tn(u2b, bh2) + tn(v2, kh2)

        y = y2[:c] + y2[c:]
        mean = _split_dot(y, ones_bd) * (1.0 / RWKV_HEAD)
        yc = y - mean
        var = _split_dot(yc * yc, ones_bd) * (1.0 / RWKV_HEAD)
        yn = yc * lax.rsqrt(var + GN_EPS) * gnw_ref[:, sl] + gnb_ref[:, sl]
        o_ref[0, :, sl] = ((yn + bonus_ref[0, :, sl]) * gate_ref[0, :, sl]).astype(o_ref.dtype)


def _rwkv(r, ld, k, v, kk, iclr, bonus, gate, gn_w, gn_b, *, chunk):
    b, t, w = r.shape
    tile = pl.BlockSpec((1, chunk, w), lambda i, j: (i, j, 0))
    const = pl.BlockSpec((1, w), lambda i, j: (0, 0))
    return pl.pallas_call(
        functools.partial(_rwkv_body, chunk=chunk),
        grid=(b, t // chunk),
        in_specs=[tile] * 8 + [const] * 2,
        out_specs=tile,
        out_shape=jax.ShapeDtypeStruct((b, t, w), BF16),
        scratch_shapes=[pltpu.VMEM((RWKV_HEADS // 2, LANES, LANES), F32)],
        compiler_params=pltpu.CompilerParams(
            dimension_semantics=("arbitrary", "arbitrary"), vmem_limit_bytes=VMEM_LIMIT),
        name="rwkv",
    )(r, ld, k, v, kk, iclr, bonus, gate, gn_w.reshape(1, w), gn_b.reshape(1, w))


def _attn_body(lq1_ref, lk1_ref, lq2_ref, lk2_ref, q_ref, k_ref, v_ref, sw_ref, o_ref, acc_ref,
               *, tq, lambda_init):
    qi = pl.program_id(2)
    q = q_ref[0]
    lane = lax.broadcasted_iota(jnp.int32, (1, LANES), 1)
    zero = jnp.zeros_like(q)
    qs = (jnp.where(lane < DIFF_QK, q, zero), jnp.where(lane < DIFF_QK, zero, q))
    causal = (lax.broadcasted_iota(jnp.int32, (tq, tq), 1)
              <= lax.broadcasted_iota(jnp.int32, (tq, tq), 0))
    acc_ref[...] = jnp.zeros_like(acc_ref)

    def step(j, carry, diagonal):
        start = pl.multiple_of(j * tq, tq)
        ks = k_ref[0, pl.ds(start, tq), :]
        vs = v_ref[0, pl.ds(start, tq), :]
        out = []
        for c in range(2):
            m, l = carry[c]
            s = lax.dot_general(qs[c], ks, _NT, preferred_element_type=F32)
            if diagonal:
                s = jnp.where(causal, s, NEG_BIG)
            m_new = jnp.maximum(m, jnp.max(s, axis=-1, keepdims=True))
            alpha = jnp.exp(m - m_new)
            pr = jnp.exp(s - m_new)
            l_new = alpha * l + jnp.sum(pr, axis=-1, keepdims=True)
            acc_ref[c] = alpha * acc_ref[c] + _dot(pr.astype(BF16), vs)
            out.append((m_new, l_new))
        return tuple(out)

    init = tuple((jnp.full((tq, 1), NEG_BIG, F32), jnp.zeros((tq, 1), F32)) for _ in range(2))
    carry = lax.fori_loop(0, qi, lambda j, cr: step(j, cr, False), init)
    (_, l1), (_, l2) = step(qi, carry, True)

    lam = (jnp.exp(jnp.sum(lq1_ref[...] * lk1_ref[...], axis=-1, keepdims=True))
           - jnp.exp(jnp.sum(lq2_ref[...] * lk2_ref[...], axis=-1, keepdims=True)) + lambda_init)
    o = acc_ref[0] * (1.0 / l1) - lam * (acc_ref[1] * (1.0 / l2))
    o = _rms(o, sw_ref[...], SUBLN_EPS) * (1.0 - lambda_init)
    o_ref[0] = o.astype(o_ref.dtype)


def _attn(q, k, v, lq1, lk1, lq2, lk2, subln_w, *, tq, lambda_init):
    b, t, w = q.shape
    heads = w // DIFF_V
    lam_spec = pl.BlockSpec((1, DIFF_QK), lambda i, h, j: (0, 0))
    return pl.pallas_call(
        functools.partial(_attn_body, tq=tq, lambda_init=lambda_init),
        grid=(b, heads, t // tq),
        in_specs=[lam_spec] * 4 + [
            pl.BlockSpec((1, tq, DIFF_V), lambda i, h, j: (i, j, h)),
            pl.BlockSpec((1, t, DIFF_V), lambda i, h, j: (i, 0, h)),
            pl.BlockSpec((1, t, DIFF_V), lambda i, h, j: (i, 0, h)),
            pl.BlockSpec((1, DIFF_V), lambda i, h, j: (0, 0)),
        ],
        out_specs=pl.BlockSpec((1, tq, DIFF_V), lambda i, h, j: (i, j, h)),
        out_shape=jax.ShapeDtypeStruct((b, t, w), BF16),
        scratch_shapes=[pltpu.VMEM((2, tq, DIFF_V), F32)],
        compiler_params=pltpu.CompilerParams(
            dimension_semantics=("parallel", "parallel", "arbitrary"), vmem_limit_bytes=VMEM_LIMIT),
        name="attn",
    )(lq1.reshape(1, -1), lk1.reshape(1, -1), lq2.reshape(1, -1), lk2.reshape(1, -1),
      q, k, v, subln_w.reshape(1, -1))


def _outproj_body(x_ref, yr_ref, yd_ref, wr_ref, wd_ref, g_ref, o_ref):
    y = _dot(yr_ref[...], wr_ref[...]) + _dot(yd_ref[...], wd_ref[...])
    o_ref[...] = x_ref[...] + _rms(y, g_ref[...], NORM_EPS)


def _outproj(x, y_rwkv, y_diff, w_o, g, *, tm):
    m, d = x.shape
    w_o = w_o.astype(BF16)
    const = lambda shape: pl.BlockSpec(shape, lambda i: (0, 0))
    return pl.pallas_call(
        _outproj_body,
        grid=(m // tm,),
        in_specs=[pl.BlockSpec((tm, d), lambda i: (i, 0)),
                  pl.BlockSpec((tm, RWKV_W), lambda i: (i, 0)),
                  pl.BlockSpec((tm, DIFF_W), lambda i: (i, 0)),
                  const((RWKV_W, d)), const((DIFF_W, d)), const((1, d))],
        out_specs=pl.BlockSpec((tm, d), lambda i: (i, 0)),
        out_shape=jax.ShapeDtypeStruct((m, d), F32),
        compiler_params=pltpu.CompilerParams(
            dimension_semantics=("parallel",), vmem_limit_bytes=VMEM_LIMIT),
        name="outproj",
    )(x, y_rwkv, y_diff, w_o[:RWKV_W], w_o[RWKV_W:], g.reshape(1, d))


def _pick(n, pref):
    return pref if n % pref == 0 else n


def _layer(x, l, p, *, chunk=64, tm_ffn=1024, tf=256, tm_proj=256, tq=512, tm_out=1024):
    b, t, d = x.shape
    m = b * t
    lambda_init = 0.8 - 0.6 * math.exp(-0.3 * l)
    tm_ffn = _pick(m, tm_ffn)
    tf = _pick(p["ffn1_w_gate"].shape[-1], tf)
    x = _ffn(x.reshape(m, d), p["ffn1_pre_g"][l], p["ffn1_post_g"][l], p["ffn1_w_gate"][l],
             p["ffn1_w_up"][l], p["ffn1_w_down"][l], tm=tm_ffn, tf=tf)
    outs = _proj(x.reshape(b, t, d), p["mix_pre_g"][l], p["w_in"][l], p["shift_mu"][l],
                 p["rwkv_w_up"][l], p["rwkv_a_up"][l], p["rwkv_g_up"][l], p["rwkv_w0"][l],
                 p["rwkv_a0"][l], p["rwkv_k_k"][l], p["rwkv_k_a"][l], p["rwkv_r_k"][l].reshape(-1),
                 tm=_pick(t, tm_proj))
    y_rwkv = _rwkv(*outs[:8], p["rwkv_gn_w"][l], p["rwkv_gn_b"][l], chunk=chunk)
    y_diff = _attn(*outs[8:], p["diff_lam_q1"][l], p["diff_lam_k1"][l], p["diff_lam_q2"][l],
                   p["diff_lam_k2"][l], p["diff_subln_w"][l], tq=_pick(t, tq),
                   lambda_init=lambda_init)
    x = _outproj(x, y_rwkv.reshape(m, -1), y_diff.reshape(m, -1), p["w_o"][l], p["mix_post_g"][l],
                 tm=_pick(m, tm_out))
    x = _ffn(x, p["ffn2_pre_g"][l], p["ffn2_post_g"][l], p["ffn2_w_gate"][l], p["ffn2_w_up"][l],
             p["ffn2_w_down"][l], tm=tm_ffn, tf=tf)
    return x.reshape(b, t, d)


def kernel(x, ffn1_pre_g, ffn1_post_g, ffn1_w_gate, ffn1_w_up, ffn1_w_down, mix_pre_g, mix_post_g,
           w_in, shift_mu, w_o, rwkv_w0, rwkv_w_up, rwkv_a0, rwkv_a_up, rwkv_g_up, rwkv_k_k,
           rwkv_k_a, rwkv_r_k, rwkv_gn_w, rwkv_gn_b, diff_lam_q1, diff_lam_k1, diff_lam_q2,
           diff_lam_k2, diff_subln_w, ffn2_pre_g, ffn2_post_g, ffn2_w_gate, ffn2_w_up, ffn2_w_down):
    p = dict(ffn1_pre_g=ffn1_pre_g, ffn1_post_g=ffn1_post_g, ffn1_w_gate=ffn1_w_gate,
             ffn1_w_up=ffn1_w_up, ffn1_w_down=ffn1_w_down, mix_pre_g=mix_pre_g,
             mix_post_g=mix_post_g, w_in=w_in, shift_mu=shift_mu, w_o=w_o, rwkv_w0=rwkv_w0,
             rwkv_w_up=rwkv_w_up, rwkv_a0=rwkv_a0, rwkv_a_up=rwkv_a_up, rwkv_g_up=rwkv_g_up,
             rwkv_k_k=rwkv_k_k, rwkv_k_a=rwkv_k_a, rwkv_r_k=rwkv_r_k, rwkv_gn_w=rwkv_gn_w,
             rwkv_gn_b=rwkv_gn_b, diff_lam_q1=diff_lam_q1, diff_lam_k1=diff_lam_k1,
             diff_lam_q2=diff_lam_q2, diff_lam_k2=diff_lam_k2, diff_subln_w=diff_subln_w,
             ffn2_pre_g=ffn2_pre_g, ffn2_post_g=ffn2_post_g, ffn2_w_gate=ffn2_w_gate,
             ffn2_w_up=ffn2_w_up, ffn2_w_down=ffn2_w_down)
    for l in range(ffn1_pre_g.shape[0]):
        x = _layer(x, l, p)
    return x
```

```python
import functools
import math

import jax
import jax.numpy as jnp
from jax import lax
from jax.experimental import pallas as pl
from jax.experimental.pallas import tpu as pltpu

F32 = jnp.float32
BF16 = jnp.bfloat16

RWKV_HEADS = 8
RWKV_HEAD = 64
RWKV_W = RWKV_HEADS * RWKV_HEAD
DIFF_HEADS = 4
DIFF_QK = 64
DIFF_V = 2 * DIFF_QK
DIFF_W = DIFF_HEADS * DIFF_V
DECAY_LORA = 64
ICLR_LORA = 64
GATE_LORA = 160
LORA_W = DECAY_LORA + ICLR_LORA + GATE_LORA
LANES = 128
LORA_PAD = -(-LORA_W // LANES) * LANES
SHIFT_COLS = 3 * RWKV_W + LORA_W
SHIFT_PAD = 3 * RWKV_W + LORA_PAD
FFN_RES = 0.5
NORM_EPS = 1e-6
GN_EPS = 64e-5
SUBLN_EPS = 1e-5
NEG_BIG = -1e30
VMEM_LIMIT = 56 * 1024 * 1024

_NT = (((1,), (1,)), ((), ()))
_TN = (((0,), (0,)), ((), ()))


def _dot(a, b):
    return jnp.dot(a, b, preferred_element_type=F32)


def _split_dot(x, w_bf16):
    hi = x.astype(BF16)
    lo = (x - hi.astype(F32)).astype(BF16)
    return _dot(hi, w_bf16) + _dot(lo, w_bf16)


def _rms(x, g, eps):
    return x * lax.rsqrt(jnp.mean(x * x, axis=-1, keepdims=True) + eps) * g


def _ffn_body(x_ref, gpre_ref, gpost_ref, wg_ref, wu_ref, wd_ref, o_ref, h_ref, acc_ref):
    f = pl.program_id(1)

    @pl.when(f == 0)
    def _():
        h_ref[...] = _rms(x_ref[...], gpre_ref[...], NORM_EPS).astype(BF16)
        acc_ref[...] = jnp.zeros_like(acc_ref)

    h = h_ref[...]
    g = _dot(h, wg_ref[...])
    u = _dot(h, wu_ref[...])
    a = (g * jax.nn.sigmoid(g) * u).astype(BF16)
    acc_ref[...] += _dot(a, wd_ref[...])

    @pl.when(f == pl.num_programs(1) - 1)
    def _():
        o_ref[...] = x_ref[...] + FFN_RES * _rms(acc_ref[...], gpost_ref[...], NORM_EPS)


def _ffn(x, g_pre, g_post, w_gate, w_up, w_down, *, tm, tf):
    m, d = x.shape
    f = w_gate.shape[1]
    return pl.pallas_call(
        _ffn_body,
        grid=(m // tm, f // tf),
        in_specs=[
            pl.BlockSpec((tm, d), lambda i, j: (i, 0)),
            pl.BlockSpec((1, d), lambda i, j: (0, 0)),
            pl.BlockSpec((1, d), lambda i, j: (0, 0)),
            pl.BlockSpec((d, tf), lambda i, j: (0, j)),
            pl.BlockSpec((d, tf), lambda i, j: (0, j)),
            pl.BlockSpec((tf, d), lambda i, j: (j, 0)),
        ],
        out_specs=pl.BlockSpec((tm, d), lambda i, j: (i, 0)),
        out_shape=jax.ShapeDtypeStruct((m, d), F32),
        scratch_shapes=[pltpu.VMEM((tm, d), BF16), pltpu.VMEM((tm, d), F32)],
        compiler_params=pltpu.CompilerParams(
            dimension_semantics=("parallel", "arbitrary"), vmem_limit_bytes=VMEM_LIMIT),
        name="ffn",
    )(x, g_pre.reshape(1, d), g_post.reshape(1, d), w_gate.astype(BF16), w_up.astype(BF16),
      w_down.astype(BF16))


def _proj_body(x_ref, g_ref, w_ref, mu_ref, wup_ref, ones_ref, w0_ref, a0_ref, kk_ref, ka_ref,
               rk_ref,
               r_out, ld_out, k_out, v_out, kk_out, ic_out, bonus_out, gate_out,
               qd_out, kd_out, vd_out, carry_ref):
    t = pl.program_id(1)

    @pl.when(t == 0)
    def _():
        carry_ref[...] = jnp.zeros_like(carry_ref)

    h = _rms(x_ref[0], g_ref[...], NORM_EPS).astype(BF16)
    p = _dot(h, w_ref[...])
    tm = p.shape[0]

    qd_out[0] = (p[:, SHIFT_PAD:SHIFT_PAD + DIFF_W] * (DIFF_QK ** -0.5)).astype(BF16)
    kd_out[0] = p[:, SHIFT_PAD + DIFF_W:SHIFT_PAD + 2 * DIFF_W].astype(BF16)
    vd_out[0] = p[:, SHIFT_PAD + 2 * DIFF_W:].astype(BF16)

    ps = p[:, :SHIFT_PAD]
    row = lax.broadcasted_iota(jnp.int32, (tm, 1), 0)
    prev = jnp.where(row == 0, carry_ref[...], pltpu.roll(ps, 1, 0))
    carry_ref[...] = ps[tm - 1:tm, :]
    ps = ps + (prev - ps) * mu_ref[...]

    p_r = ps[:, :RWKV_W]
    p_k = ps[:, RWKV_W:2 * RWKV_W]
    p_v = ps[:, 2 * RWKV_W:3 * RWKV_W]
    z = ps[:, 3 * RWKV_W:]
    lane = lax.broadcasted_iota(jnp.int32, (1, LORA_PAD), 1)
    act = jnp.where(lane < DECAY_LORA, jnp.tanh(z),
                    jnp.where(lane < DECAY_LORA + ICLR_LORA, z, jax.nn.sigmoid(z)))
    up = _dot(act.astype(BF16), wup_ref[...])
    zw = -(w0_ref[...] + up[:, :RWKV_W])
    softplus = jnp.maximum(zw, 0.0) + jnp.log(1.0 + jnp.exp(-jnp.abs(zw)))
    ld = -jnp.exp(-softplus - 0.5)
    iclr = jax.nn.sigmoid(a0_ref[...] + up[:, RWKV_W:2 * RWKV_W])
    gate = up[:, 2 * RWKV_W:]

    ones_bd = ones_ref[...]
    kk = p_k * kk_ref[...]
    kk = kk * lax.rsqrt(jnp.maximum(_split_dot(kk * kk, ones_bd), 1e-24))
    k = p_k * (1.0 + (iclr - 1.0) * ka_ref[...])
    bonus = _split_dot(p_r * k * rk_ref[...], ones_bd) * p_v

    r_out[0] = p_r
    ld_out[0] = ld
    k_out[0] = k
    v_out[0] = p_v
    kk_out[0] = kk
    ic_out[0] = iclr
    bonus_out[0] = bonus
    gate_out[0] = gate


def _head_ones(width, head):
    i = jnp.arange(width) // head
    return (i[:, None] == i[None, :]).astype(BF16)


def _proj(x, g, w_in, shift_mu, w_up, a_up, g_up, w0, a0, k_k, k_a, r_k, *, tm):
    b, t, d = x.shape
    pad = SHIFT_PAD - SHIFT_COLS
    w = jnp.concatenate(
        [w_in[:, :SHIFT_COLS], jnp.zeros((d, pad), F32), w_in[:, SHIFT_COLS:]], axis=1).astype(BF16)
    mu = jnp.concatenate([shift_mu, jnp.zeros((pad,), F32)]).reshape(1, SHIFT_PAD)
    wup = jnp.zeros((LORA_PAD, 3 * RWKV_W), F32)
    wup = wup.at[:DECAY_LORA, :RWKV_W].set(w_up)
    wup = wup.at[DECAY_LORA:DECAY_LORA + ICLR_LORA, RWKV_W:2 * RWKV_W].set(a_up)
    wup = wup.at[DECAY_LORA + ICLR_LORA:LORA_W, 2 * RWKV_W:].set(g_up)
    wup = wup.astype(BF16)
    wcols = w.shape[1]
    vec = lambda a: a.reshape(1, RWKV_W)
    const = lambda shape: pl.BlockSpec(shape, lambda i, j: (0,) * len(shape))
    tile = lambda width: pl.BlockSpec((1, tm, width), lambda i, j: (i, j, 0))
    f32_out = jax.ShapeDtypeStruct((b, t, RWKV_W), F32)
    bf_out = jax.ShapeDtypeStruct((b, t, DIFF_W), BF16)
    return pl.pallas_call(
        _proj_body,
        grid=(b, t // tm),
        in_specs=[tile(d), const((1, d)), const((d, wcols)), const((1, SHIFT_PAD)),
                  const((LORA_PAD, 3 * RWKV_W)), const((RWKV_W, RWKV_W))] + [const((1, RWKV_W))] * 5,
        out_specs=[tile(RWKV_W)] * 8 + [tile(DIFF_W)] * 3,
        out_shape=[f32_out] * 8 + [bf_out] * 3,
        scratch_shapes=[pltpu.VMEM((1, SHIFT_PAD), F32)],
        compiler_params=pltpu.CompilerParams(
            dimension_semantics=("arbitrary", "arbitrary"), vmem_limit_bytes=VMEM_LIMIT),
        name="proj",
    )(x, g.reshape(1, d), w, mu, wup, _head_ones(RWKV_W, RWKV_HEAD),
      vec(w0), vec(a0), vec(k_k), vec(k_a), vec(r_k))


def _rwkv_body(r_ref, ld_ref, k_ref, v_ref, kk_ref, ic_ref, bonus_ref, gate_ref, gnw_ref, gnb_ref,
               o_ref, h_ref, *, chunk):
    c = chunk
    c2 = 2 * c

    @pl.when(pl.program_id(1) == 0)
    def _():
        h_ref[...] = jnp.zeros_like(h_ref)

    ld = ld_ref[0]
    ri = lax.broadcasted_iota(jnp.int32, (c, c), 0)
    ci = lax.broadcasted_iota(jnp.int32, (c, c), 1)
    tril = (ci <= ri).astype(F32)
    cum = jnp.dot(tril, ld, preferred_element_type=F32, precision=lax.Precision.HIGHEST)
    tot = cum[c - 1:c, :]
    g_in = jnp.exp(cum)
    g_inv = jnp.exp(-cum)
    g_prev = jnp.exp(cum - ld)
    g_rest = jnp.exp(tot - cum)
    g_tot = jnp.exp(tot)

    kk = kk_ref[0]
    b = kk * ic_ref[0]
    k = k_ref[0]
    at = (-kk * g_prev).astype(BF16)
    rt = (r_ref[0] * g_in).astype(BF16)
    bt = (b * g_inv).astype(BF16)
    kt = (k * g_inv).astype(BF16)
    bh = (b * g_rest).astype(BF16)
    kh = (k * g_rest).astype(BF16)
    v = v_ref[0].astype(BF16)

    lane = lax.broadcasted_iota(jnp.int32, (1, LANES), 1)
    lo = lane < RWKV_HEAD

    def stack(x):
        zero = jnp.zeros_like(x)
        return jnp.concatenate([jnp.where(lo, x, zero), jnp.where(lo, zero, x)], axis=0)

    row2 = lax.broadcasted_iota(jnp.int32, (c2, c2), 0)
    col2 = lax.broadcasted_iota(jnp.int32, (c2, c2), 1)
    same = (row2 // c) == (col2 // c)
    strict = same & (col2 < row2)
    incl = same & (col2 <= row2)
    eye = (row2 == col2).astype(F32)
    levels = int(math.log2(c))
    level_masks = [((row2 >> l) == (col2 >> l)) & ((row2 >> (l - 1)) != (col2 >> (l - 1)))
                   & (col2 < row2) for l in range(1, levels + 1)]
    ones_bd = ((lax.broadcasted_iota(jnp.int32, (LANES, LANES), 0) // RWKV_HEAD)
               == (lax.broadcasted_iota(jnp.int32, (LANES, LANES), 1) // RWKV_HEAD)).astype(BF16)
    nt = lambda x, y: lax.dot_general(x, y, _NT, preferred_element_type=F32)
    tn = lambda x, y: lax.dot_general(x, y, _TN, preferred_element_type=F32)

    pairs = range(RWKV_HEADS // 2)
    sls = [slice(p * LANES, (p + 1) * LANES) for p in pairs]
    each = lambda f: [f(p) for p in pairs]
    a2 = each(lambda p: stack(at[:, sls[p]]))
    r2 = each(lambda p: stack(rt[:, sls[p]]))
    b2 = each(lambda p: stack(bt[:, sls[p]]))
    k2 = each(lambda p: stack(kt[:, sls[p]]))
    bh2 = each(lambda p: stack(bh[:, sls[p]]))
    kh2 = each(lambda p: stack(kh[:, sls[p]]))
    v2 = each(lambda p: stack(v[:, sls[p]]))
    s0 = each(lambda p: h_ref[p])
    s0b = each(lambda p: s0[p].astype(BF16))

    a_ab = each(lambda p: jnp.where(strict, nt(a2[p], b2[p]), 0.0))
    a_ak = each(lambda p: jnp.where(strict, nt(a2[p], k2[p]), 0.0).astype(BF16))
    a_rb = each(lambda p: jnp.where(incl, nt(r2[p], b2[p]), 0.0).astype(BF16))
    a_rk = each(lambda p: jnp.where(incl, nt(r2[p], k2[p]), 0.0).astype(BF16))

    tinv = each(lambda p: eye + jnp.where(level_masks[0], a_ab[p], 0.0))
    for l in range(2, levels + 1):
        e = each(lambda p: jnp.where(level_masks[l - 1], a_ab[p], 0.0).astype(BF16))
        tb = each(lambda p: tinv[p].astype(BF16))
        te = each(lambda p: _dot(tb[p], e[p]).astype(BF16))
        tinv = each(lambda p: tinv[p] + _dot(te[p], tb[p]))
    tb = each(lambda p: tinv[p].astype(BF16))

    rhs = each(lambda p: (nt(a2[p], s0b[p]) + _dot(a_ak[p], v2[p])).astype(BF16))
    u2b = each(lambda p: _dot(tb[p], rhs[p]).astype(BF16))
    y2 = each(lambda p: nt(r2[p], s0b[p]) + _dot(a_rb[p], u2b[p]) + _dot(a_rk[p], v2[p]))
    for p in pairs:
        h_ref[p] = g_tot[:, sls[p]] * s0[p] + tn(u2b[p], bh2[p]) + tn(v2[p], kh2[p])

    for p in pairs:
        sl = sls[p]
        y = y2[p][:c] + y2[p][c:]
        mean = _split_dot(y, ones_bd) * (1.0 / RWKV_HEAD)
        yc = y - mean
        var = _split_dot(yc * yc, ones_bd) * (1.0 / RWKV_HEAD)
        yn = yc * lax.rsqrt(var + GN_EPS) * gnw_ref[:, sl] + gnb_ref[:, sl]
        o_ref[0, :, sl] = ((yn + bonus_ref[0, :, sl]) * gate_ref[0, :, sl]).astype(o_ref.dtype)


def _rwkv(r, ld, k, v, kk, iclr, bonus, gate, gn_w, gn_b, *, chunk):
    b, t, w = r.shape
    tile = pl.BlockSpec((1, chunk, w), lambda i, j: (i, j, 0))
    const = pl.BlockSpec((1, w), lambda i, j: (0, 0))
    return pl.pallas_call(
        functools.partial(_rwkv_body, chunk=chunk),
        grid=(b, t // chunk),
        in_specs=[tile] * 8 + [const] * 2,
        out_specs=tile,
        out_shape=jax.ShapeDtypeStruct((b, t, w), BF16),
        scratch_shapes=[pltpu.VMEM((RWKV_HEADS // 2, LANES, LANES), F32)],
        compiler_params=pltpu.CompilerParams(
            dimension_semantics=("arbitrary", "arbitrary"), vmem_limit_bytes=VMEM_LIMIT),
        name="rwkv",
    )(r, ld, k, v, kk, iclr, bonus, gate, gn_w.reshape(1, w), gn_b.reshape(1, w))


def _attn_body(lq1_ref, lk1_ref, lq2_ref, lk2_ref, q_ref, k_ref, v_ref, sw_ref, o_ref, acc_ref,
               *, tq, lambda_init):
    qi = pl.program_id(2)
    q = q_ref[0]
    lane = lax.broadcasted_iota(jnp.int32, (1, LANES), 1)
    zero = jnp.zeros_like(q)
    qs = (jnp.where(lane < DIFF_QK, q, zero), jnp.where(lane < DIFF_QK, zero, q))
    causal = (lax.broadcasted_iota(jnp.int32, (tq, tq), 1)
              <= lax.broadcasted_iota(jnp.int32, (tq, tq), 0))
    acc_ref[...] = jnp.zeros_like(acc_ref)

    def step(j, carry, diagonal):
        start = pl.multiple_of(j * tq, tq)
        ks = k_ref[0, pl.ds(start, tq), :]
        vs = v_ref[0, pl.ds(start, tq), :]
        out = []
        for c in range(2):
            m, l = carry[c]
            s = lax.dot_general(qs[c], ks, _NT, preferred_element_type=F32)
            if diagonal:
                s = jnp.where(causal, s, NEG_BIG)
            m_new = jnp.maximum(m, jnp.max(s, axis=-1, keepdims=True))
            alpha = jnp.exp(m - m_new)
            pr = jnp.exp(s - m_new)
            l_new = alpha * l + jnp.sum(pr, axis=-1, keepdims=True)
            acc_ref[c] = alpha * acc_ref[c] + _dot(pr.astype(BF16), vs)
            out.append((m_new, l_new))
        return tuple(out)

    init = tuple((jnp.full((tq, 1), NEG_BIG, F32), jnp.zeros((tq, 1), F32)) for _ in range(2))
    carry = lax.fori_loop(0, qi, lambda j, cr: step(j, cr, False), init)
    (_, l1), (_, l2) = step(qi, carry, True)

    lam = (jnp.exp(jnp.sum(lq1_ref[...] * lk1_ref[...], axis=-1, keepdims=True))
           - jnp.exp(jnp.sum(lq2_ref[...] * lk2_ref[...], axis=-1, keepdims=True)) + lambda_init)
    o = acc_ref[0] * (1.0 / l1) - lam * (acc_ref[1] * (1.0 / l2))
    o = _rms(o, sw_ref[...], SUBLN_EPS) * (1.0 - lambda_init)
    o_ref[0] = o.astype(o_ref.dtype)


def _attn(q, k, v, lq1, lk1, lq2, lk2, subln_w, *, tq, lambda_init):
    b, t, w = q.shape
    heads = w // DIFF_V
    lam_spec = pl.BlockSpec((1, DIFF_QK), lambda i, h, j: (0, 0))
    return pl.pallas_call(
        functools.partial(_attn_body, tq=tq, lambda_init=lambda_init),
        grid=(b, heads, t // tq),
        in_specs=[lam_spec] * 4 + [
            pl.BlockSpec((1, tq, DIFF_V), lambda i, h, j: (i, j, h)),
            pl.BlockSpec((1, t, DIFF_V), lambda i, h, j: (i, 0, h)),
            pl.BlockSpec((1, t, DIFF_V), lambda i, h, j: (i, 0, h)),
            pl.BlockSpec((1, DIFF_V), lambda i, h, j: (0, 0)),
        ],
        out_specs=pl.BlockSpec((1, tq, DIFF_V), lambda i, h, j: (i, j, h)),
        out_shape=jax.ShapeDtypeStruct((b, t, w), BF16),
        scratch_shapes=[pltpu.VMEM((2, tq, DIFF_V), F32)],
        compiler_params=pltpu.CompilerParams(
            dimension_semantics=("parallel", "parallel", "arbitrary"), vmem_limit_bytes=VMEM_LIMIT),
        name="attn",
    )(lq1.reshape(1, -1), lk1.reshape(1, -1), lq2.reshape(1, -1), lk2.reshape(1, -1),
      q, k, v, subln_w.reshape(1, -1))


def _outproj_body(x_ref, yr_ref, yd_ref, wr_ref, wd_ref, g_ref, o_ref):
    y = _dot(yr_ref[...], wr_ref[...]) + _dot(yd_ref[...], wd_ref[...])
    o_ref[...] = x_ref[...] + _rms(y, g_ref[...], NORM_EPS)


def _outproj(x, y_rwkv, y_diff, w_o, g, *, tm):
    m, d = x.shape
    w_o = w_o.astype(BF16)
    const = lambda shape: pl.BlockSpec(shape, lambda i: (0, 0))
    return pl.pallas_call(
        _outproj_body,
        grid=(m // tm,),
        in_specs=[pl.BlockSpec((tm, d), lambda i: (i, 0)),
                  pl.BlockSpec((tm, RWKV_W), lambda i: (i, 0)),
                  pl.BlockSpec((tm, DIFF_W), lambda i: (i, 0)),
                  const((RWKV_W, d)), const((DIFF_W, d)), const((1, d))],
        out_specs=pl.BlockSpec((tm, d), lambda i: (i, 0)),
        out_shape=jax.ShapeDtypeStruct((m, d), F32),
        compiler_params=pltpu.CompilerParams(
            dimension_semantics=("parallel",), vmem_limit_bytes=VMEM_LIMIT),
        name="outproj",
    )(x, y_rwkv, y_diff, w_o[:RWKV_W], w_o[RWKV_W:], g.reshape(1, d))


def _pick(n, pref):
    return pref if n % pref == 0 else n


def _layer(x, l, p, *, chunk=64, tm_ffn=1024, tf=256, tm_proj=256, tq=512, tm_out=1024):
    b, t, d = x.shape
    m = b * t
    lambda_init = 0.8 - 0.6 * math.exp(-0.3 * l)
    tm_ffn = _pick(m, tm_ffn)
    tf = _pick(p["ffn1_w_gate"].shape[-1], tf)
    x = _ffn(x.reshape(m, d), p["ffn1_pre_g"][l], p["ffn1_post_g"][l], p["ffn1_w_gate"][l],
             p["ffn1_w_up"][l], p["ffn1_w_down"][l], tm=tm_ffn, tf=tf)
    outs = _proj(x.reshape(b, t, d), p["mix_pre_g"][l], p["w_in"][l], p["shift_mu"][l],
                 p["rwkv_w_up"][l], p["rwkv_a_up"][l], p["rwkv_g_up"][l], p["rwkv_w0"][l],
                 p["rwkv_a0"][l], p["rwkv_k_k"][l], p["rwkv_k_a"][l], p["rwkv_r_k"][l].reshape(-1),
                 tm=_pick(t, tm_proj))
    y_rwkv = _rwkv(*outs[:8], p["rwkv_gn_w"][l], p["rwkv_gn_b"][l], chunk=chunk)
    y_diff = _attn(*outs[8:], p["diff_lam_q1"][l], p["diff_lam_k1"][l], p["diff_lam_q2"][l],
                   p["diff_lam_k2"][l], p["diff_subln_w"][l], tq=_pick(t, tq),
                   lambda_init=lambda_init)
    x = _outproj(x, y_rwkv.reshape(m, -1), y_diff.reshape(m, -1), p["w_o"][l], p["mix_post_g"][l],
                 tm=_pick(m, tm_out))
    x = _ffn(x, p["ffn2_pre_g"][l], p["ffn2_post_g"][l], p["ffn2_w_gate"][l], p["ffn2_w_up"][l],
             p["ffn2_w_down"][l], tm=tm_ffn, tf=tf)
    return x.reshape(b, t, d)


def kernel(x, ffn1_pre_g, ffn1_post_g, ffn1_w_gate, ffn1_w_up, ffn1_w_down, mix_pre_g, mix_post_g,
           w_in, shift_mu, w_o, rwkv_w0, rwkv_w_up, rwkv_a0, rwkv_a_up, rwkv_g_up, rwkv_k_k,
           rwkv_k_a, rwkv_r_k, rwkv_gn_w, rwkv_gn_b, diff_lam_q1, diff_lam_k1, diff_lam_q2,
           diff_lam_k2, diff_subln_w, ffn2_pre_g, ffn2_post_g, ffn2_w_gate, ffn2_w_up, ffn2_w_down):
    p = dict(ffn1_pre_g=ffn1_pre_g, ffn1_post_g=ffn1_post_g, ffn1_w_gate=ffn1_w_gate,
             ffn1_w_up=ffn1_w_up, ffn1_w_down=ffn1_w_down, mix_pre_g=mix_pre_g,
             mix_post_g=mix_post_g, w_in=w_in, shift_mu=shift_mu, w_o=w_o, rwkv_w0=rwkv_w0,
             rwkv_w_up=rwkv_w_up, rwkv_a0=rwkv_a0, rwkv_a_up=rwkv_a_up, rwkv_g_up=rwkv_g_up,
             rwkv_k_k=rwkv_k_k, rwkv_k_a=rwkv_k_a, rwkv_r_k=rwkv_r_k, rwkv_gn_w=rwkv_gn_w,
             rwkv_gn_b=rwkv_gn_b, diff_lam_q1=diff_lam_q1, diff_lam_k1=diff_lam_k1,
             diff_lam_q2=diff_lam_q2, diff_lam_k2=diff_lam_k2, diff_subln_w=diff_subln_w,
             ffn2_pre_g=ffn2_pre_g, ffn2_post_g=ffn2_post_g, ffn2_w_gate=ffn2_w_gate,
             ffn2_w_up=ffn2_w_up, ffn2_w_down=ffn2_w_down)
    for l in range(ffn1_pre_g.shape[0]):
        x = _layer(x, l, p)
    return x
```

```python
import functools
import math

import jax
import jax.numpy as jnp
from jax import lax
from jax.experimental import pallas as pl
from jax.experimental.pallas import tpu as pltpu

F32 = jnp.float32
BF16 = jnp.bfloat16

RWKV_HEADS = 8
RWKV_HEAD = 64
RWKV_W = RWKV_HEADS * RWKV_HEAD
DIFF_HEADS = 4
DIFF_QK = 64
DIFF_V = 2 * DIFF_QK
DIFF_W = DIFF_HEADS * DIFF_V
DECAY_LORA = 64
ICLR_LORA = 64
GATE_LORA = 160
LORA_W = DECAY_LORA + ICLR_LORA + GATE_LORA
LANES = 128
LORA_PAD = -(-LORA_W // LANES) * LANES
SHIFT_COLS = 3 * RWKV_W + LORA_W
SHIFT_PAD = 3 * RWKV_W + LORA_PAD
FFN_RES = 0.5
NORM_EPS = 1e-6
GN_EPS = 64e-5
SUBLN_EPS = 1e-5
NEG_BIG = -1e30
VMEM_LIMIT = 56 * 1024 * 1024

_NT = (((1,), (1,)), ((), ()))
_TN = (((0,), (0,)), ((), ()))


def _dot(a, b):
    return jnp.dot(a, b, preferred_element_type=F32)


def _split_dot(x, w_bf16):
    hi = x.astype(BF16)
    lo = (x - hi.astype(F32)).astype(BF16)
    return _dot(hi, w_bf16) + _dot(lo, w_bf16)


def _rms(x, g, eps):
    return x * lax.rsqrt(jnp.mean(x * x, axis=-1, keepdims=True) + eps) * g


def _ffn_body(x_ref, gpre_ref, gpost_ref, wg_ref, wu_ref, wd_ref, o_ref, h_ref, acc_ref):
    f = pl.program_id(1)

    @pl.when(f == 0)
    def _():
        h_ref[...] = _rms(x_ref[...], gpre_ref[...], NORM_EPS).astype(BF16)
        acc_ref[...] = jnp.zeros_like(acc_ref)

    h = h_ref[...]
    g = _dot(h, wg_ref[...])
    u = _dot(h, wu_ref[...])
    a = (g * jax.nn.sigmoid(g) * u).astype(BF16)
    acc_ref[...] += _dot(a, wd_ref[...])

    @pl.when(f == pl.num_programs(1) - 1)
    def _():
        o_ref[...] = x_ref[...] + FFN_RES * _rms(acc_ref[...], gpost_ref[...], NORM_EPS)


def _ffn(x, g_pre, g_post, w_gate, w_up, w_down, *, tm, tf):
    m, d = x.shape
    f = w_gate.shape[1]
    return pl.pallas_call(
        _ffn_body,
        grid=(m // tm, f // tf),
        in_specs=[
            pl.BlockSpec((tm, d), lambda i, j: (i, 0)),
            pl.BlockSpec((1, d), lambda i, j: (0, 0)),
            pl.BlockSpec((1, d), lambda i, j: (0, 0)),
            pl.BlockSpec((d, tf), lambda i, j: (0, j)),
            pl.BlockSpec((d, tf), lambda i, j: (0, j)),
            pl.BlockSpec((tf, d), lambda i, j: (j, 0)),
        ],
        out_specs=pl.BlockSpec((tm, d), lambda i, j: (i, 0)),
        out_shape=jax.ShapeDtypeStruct((m, d), F32),
        scratch_shapes=[pltpu.VMEM((tm, d), BF16), pltpu.VMEM((tm, d), F32)],
        compiler_params=pltpu.CompilerParams(
            dimension_semantics=("parallel", "arbitrary"), vmem_limit_bytes=VMEM_LIMIT),
        name="ffn",
    )(x, g_pre.reshape(1, d), g_post.reshape(1, d), w_gate.astype(BF16), w_up.astype(BF16),
      w_down.astype(BF16))


def _proj_body(x_ref, g_ref, w_ref, mu_ref, wup_ref, ones_ref, w0_ref, a0_ref, kk_ref, ka_ref,
               rk_ref,
               r_out, ld_out, k_out, v_out, kk_out, ic_out, bonus_out, gate_out,
               qd_out, kd_out, vd_out, carry_ref):
    t = pl.program_id(1)

    @pl.when(t == 0)
    def _():
        carry_ref[...] = jnp.zeros_like(carry_ref)

    h = _rms(x_ref[0], g_ref[...], NORM_EPS).astype(BF16)
    p = _dot(h, w_ref[...])
    tm = p.shape[0]

    qd_out[0] = (p[:, SHIFT_PAD:SHIFT_PAD + DIFF_W] * (DIFF_QK ** -0.5)).astype(BF16)
    kd_out[0] = p[:, SHIFT_PAD + DIFF_W:SHIFT_PAD + 2 * DIFF_W].astype(BF16)
    vd_out[0] = p[:, SHIFT_PAD + 2 * DIFF_W:].astype(BF16)

    ps = p[:, :SHIFT_PAD]
    row = lax.broadcasted_iota(jnp.int32, (tm, 1), 0)
    prev = jnp.where(row == 0, carry_ref[...], pltpu.roll(ps, 1, 0))
    carry_ref[...] = ps[tm - 1:tm, :]
    ps = ps + (prev - ps) * mu_ref[...]

    p_r = ps[:, :RWKV_W]
    p_k = ps[:, RWKV_W:2 * RWKV_W]
    p_v = ps[:, 2 * RWKV_W:3 * RWKV_W]
    z = ps[:, 3 * RWKV_W:]
    lane = lax.broadcasted_iota(jnp.int32, (1, LORA_PAD), 1)
    act = jnp.where(lane < DECAY_LORA, jnp.tanh(z),
                    jnp.where(lane < DECAY_LORA + ICLR_LORA, z, jax.nn.sigmoid(z)))
    up = _dot(act.astype(BF16), wup_ref[...])
    zw = -(w0_ref[...] + up[:, :RWKV_W])
    softplus = jnp.maximum(zw, 0.0) + jnp.log(1.0 + jnp.exp(-jnp.abs(zw)))
    ld = -jnp.exp(-softplus - 0.5)
    iclr = jax.nn.sigmoid(a0_ref[...] + up[:, RWKV_W:2 * RWKV_W])
    gate = up[:, 2 * RWKV_W:]

    ones_bd = ones_ref[...]
    kk = p_k * kk_ref[...]
    kk = kk * lax.rsqrt(jnp.maximum(_split_dot(kk * kk, ones_bd), 1e-24))
    k = p_k * (1.0 + (iclr - 1.0) * ka_ref[...])
    bonus = _split_dot(p_r * k * rk_ref[...], ones_bd) * p_v

    r_out[0] = p_r
    ld_out[0] = ld
    k_out[0] = k
    v_out[0] = p_v
    kk_out[0] = kk
    ic_out[0] = iclr
    bonus_out[0] = bonus
    gate_out[0] = gate


def _head_ones(width, head):
    i = jnp.arange(width) // head
    return (i[:, None] == i[None, :]).astype(BF16)


def _proj(x, g, w_in, shift_mu, w_up, a_up, g_up, w0, a0, k_k, k_a, r_k, *, tm):
    b, t, d = x.shape
    pad = SHIFT_PAD - SHIFT_COLS
    w = jnp.concatenate(
        [w_in[:, :SHIFT_COLS], jnp.zeros((d, pad), F32), w_in[:, SHIFT_COLS:]], axis=1).astype(BF16)
    mu = jnp.concatenate([shift_mu, jnp.zeros((pad,), F32)]).reshape(1, SHIFT_PAD)
    wup = jnp.zeros((LORA_PAD, 3 * RWKV_W), F32)
    wup = wup.at[:DECAY_LORA, :RWKV_W].set(w_up)
    wup = wup.at[DECAY_LORA:DECAY_LORA + ICLR_LORA, RWKV_W:2 * RWKV_W].set(a_up)
    wup = wup.at[DECAY_LORA + ICLR_LORA:LORA_W, 2 * RWKV_W:].set(g_up)
    wup = wup.astype(BF16)
    wcols = w.shape[1]
    vec = lambda a: a.reshape(1, RWKV_W)
    const = lambda shape: pl.BlockSpec(shape, lambda i, j: (0,) * len(shape))
    tile = lambda width: pl.BlockSpec((1, tm, width), lambda i, j: (i, j, 0))
    f32_out = jax.ShapeDtypeStruct((b, t, RWKV_W), F32)
    bf_out = jax.ShapeDtypeStruct((b, t, DIFF_W), BF16)
    return pl.pallas_call(
        _proj_body,
        grid=(b, t // tm),
        in_specs=[tile(d), const((1, d)), const((d, wcols)), const((1, SHIFT_PAD)),
                  const((LORA_PAD, 3 * RWKV_W)), const((RWKV_W, RWKV_W))] + [const((1, RWKV_W))] * 5,
        out_specs=[tile(RWKV_W)] * 8 + [tile(DIFF_W)] * 3,
        out_shape=[f32_out] * 8 + [bf_out] * 3,
        scratch_shapes=[pltpu.VMEM((1, SHIFT_PAD), F32)],
        compiler_params=pltpu.CompilerParams(
            dimension_semantics=("arbitrary", "arbitrary"), vmem_limit_bytes=VMEM_LIMIT),
        name="proj",
    )(x, g.reshape(1, d), w, mu, wup, _head_ones(RWKV_W, RWKV_HEAD),
      vec(w0), vec(a0), vec(k_k), vec(k_a), vec(r_k))


def _rwkv_body(r_ref, ld_ref, k_ref, v_ref, kk_ref, ic_ref, bonus_ref, gate_ref, gnw_ref, gnb_ref,
               o_ref, h_ref, *, chunk):
    c = chunk
    c2 = 2 * c
    nc = r_ref.shape[1] // c

    @pl.when(pl.program_id(1) == 0)
    def _():
        h_ref[...] = jnp.zeros_like(h_ref)

    ri = lax.broadcasted_iota(jnp.int32, (c, c), 0)
    ci = lax.broadcasted_iota(jnp.int32, (c, c), 1)
    tril = (ci <= ri).astype(F32)
    at, rt, bt, kt, bh, kh, v, g_tot = [], [], [], [], [], [], [], []
    for i in range(nc):
        rows = slice(i * c, (i + 1) * c)
        ld = ld_ref[0, rows, :]
        cum = jnp.dot(tril, ld, preferred_element_type=F32, precision=lax.Precision.HIGHEST)
        tot = cum[c - 1:c, :]
        g_inv = jnp.exp(-cum)
        g_rest = jnp.exp(tot - cum)
        kk = kk_ref[0, rows, :]
        b = kk * ic_ref[0, rows, :]
        k = k_ref[0, rows, :]
        at.append((-kk * jnp.exp(cum - ld)).astype(BF16))
        rt.append((r_ref[0, rows, :] * jnp.exp(cum)).astype(BF16))
        bt.append((b * g_inv).astype(BF16))
        kt.append((k * g_inv).astype(BF16))
        bh.append((b * g_rest).astype(BF16))
        kh.append((k * g_rest).astype(BF16))
        v.append(v_ref[0, rows, :].astype(BF16))
        g_tot.append(jnp.exp(tot))

    lane = lax.broadcasted_iota(jnp.int32, (1, LANES), 1)
    lo = lane < RWKV_HEAD

    def stack(x):
        zero = jnp.zeros_like(x)
        return jnp.concatenate([jnp.where(lo, x, zero), jnp.where(lo, zero, x)], axis=0)

    row2 = lax.broadcasted_iota(jnp.int32, (c2, c2), 0)
    col2 = lax.broadcasted_iota(jnp.int32, (c2, c2), 1)
    same = (row2 // c) == (col2 // c)
    strict = same & (col2 < row2)
    incl = same & (col2 <= row2)
    eye = (row2 == col2).astype(F32)
    levels = int(math.log2(c))
    level_masks = [((row2 >> l) == (col2 >> l)) & ((row2 >> (l - 1)) != (col2 >> (l - 1)))
                   & (col2 < row2) for l in range(1, levels + 1)]
    ones_bd = ((lax.broadcasted_iota(jnp.int32, (LANES, LANES), 0) // RWKV_HEAD)
               == (lax.broadcasted_iota(jnp.int32, (LANES, LANES), 1) // RWKV_HEAD)).astype(BF16)
    nt = lambda x, y: lax.dot_general(x, y, _NT, preferred_element_type=F32)
    tn = lambda x, y: lax.dot_general(x, y, _TN, preferred_element_type=F32)

    npairs = RWKV_HEADS // 2
    chains = [(i, p) for i in range(nc) for p in range(npairs)]
    sls = [slice(p * LANES, (p + 1) * LANES) for p in range(npairs)]
    each = lambda f: {ip: f(ip) for ip in chains}
    pick = lambda xs: each(lambda ip: stack(xs[ip[0]][:, sls[ip[1]]]))
    a2, r2, b2, k2, bh2, kh2, v2 = (pick(xs) for xs in (at, rt, bt, kt, bh, kh, v))

    a_ab = each(lambda ip: jnp.where(strict, nt(a2[ip], b2[ip]), 0.0))
    a_ak = each(lambda ip: jnp.where(strict, nt(a2[ip], k2[ip]), 0.0).astype(BF16))
    a_rb = each(lambda ip: jnp.where(incl, nt(r2[ip], b2[ip]), 0.0).astype(BF16))
    a_rk = each(lambda ip: jnp.where(incl, nt(r2[ip], k2[ip]), 0.0).astype(BF16))

    tinv = each(lambda ip: eye + jnp.where(level_masks[0], a_ab[ip], 0.0))
    for l in range(2, levels + 1):
        e = each(lambda ip: jnp.where(level_masks[l - 1], a_ab[ip], 0.0).astype(BF16))
        tb = each(lambda ip: tinv[ip].astype(BF16))
        te = each(lambda ip: _dot(tb[ip], e[ip]).astype(BF16))
        tinv = each(lambda ip: tinv[ip] + _dot(te[ip], tb[ip]))
    tb = each(lambda ip: tinv[ip].astype(BF16))

    akv = each(lambda ip: _dot(a_ak[ip], v2[ip]).astype(BF16))
    wa = each(lambda ip: _dot(tb[ip], a2[ip]).astype(BF16))
    u0 = each(lambda ip: _dot(tb[ip], akv[ip]).astype(BF16))
    pm = each(lambda ip: tn(wa[ip], bh2[ip]).astype(BF16))
    gm = each(lambda ip: tn(u0[ip], bh2[ip]) + tn(v2[ip], kh2[ip]))
    q2 = each(lambda ip: (r2[ip].astype(F32) + _dot(a_rb[ip], wa[ip])).astype(BF16))
    y0 = each(lambda ip: _dot(a_rb[ip], u0[ip]) + _dot(a_rk[ip], v2[ip]))

    state = [h_ref[p] for p in range(npairs)]
    for i in range(nc):
        sb = [s.astype(BF16) for s in state]
        y2 = [nt(q2[i, p], sb[p]) + y0[i, p] for p in range(npairs)]
        state = [g_tot[i][:, sls[p]] * state[p] + _dot(sb[p], pm[i, p]) + gm[i, p]
                 for p in range(npairs)]
        rows = slice(i * c, (i + 1) * c)
        for p in range(npairs):
            sl = sls[p]
            y = y2[p][:c] + y2[p][c:]
            mean = _split_dot(y, ones_bd) * (1.0 / RWKV_HEAD)
            yc = y - mean
            var = _split_dot(yc * yc, ones_bd) * (1.0 / RWKV_HEAD)
            yn = yc * lax.rsqrt(var + GN_EPS) * gnw_ref[:, sl] + gnb_ref[:, sl]
            o_ref[0, rows, sl] = ((yn + bonus_ref[0, rows, sl]) * gate_ref[0, rows, sl]
                                  ).astype(o_ref.dtype)
    for p in range(npairs):
        h_ref[p] = state[p]


def _rwkv(r, ld, k, v, kk, iclr, bonus, gate, gn_w, gn_b, *, chunk, tb):
    b, t, w = r.shape
    tile = pl.BlockSpec((1, tb, w), lambda i, j: (i, j, 0))
    const = pl.BlockSpec((1, w), lambda i, j: (0, 0))
    return pl.pallas_call(
        functools.partial(_rwkv_body, chunk=chunk),
        grid=(b, t // tb),
        in_specs=[tile] * 8 + [const] * 2,
        out_specs=tile,
        out_shape=jax.ShapeDtypeStruct((b, t, w), BF16),
        scratch_shapes=[pltpu.VMEM((RWKV_HEADS // 2, LANES, LANES), F32)],
        compiler_params=pltpu.CompilerParams(
            dimension_semantics=("arbitrary", "arbitrary"), vmem_limit_bytes=VMEM_LIMIT),
        name="rwkv",
    )(r, ld, k, v, kk, iclr, bonus, gate, gn_w.reshape(1, w), gn_b.reshape(1, w))


def _attn_body(lq1_ref, lk1_ref, lq2_ref, lk2_ref, q_ref, k_ref, v_ref, sw_ref, o_ref, acc_ref,
               *, tq, lambda_init):
    qi = pl.program_id(2)
    q = q_ref[0]
    lane = lax.broadcasted_iota(jnp.int32, (1, LANES), 1)
    zero = jnp.zeros_like(q)
    qs = (jnp.where(lane < DIFF_QK, q, zero), jnp.where(lane < DIFF_QK, zero, q))
    causal = (lax.broadcasted_iota(jnp.int32, (tq, tq), 1)
              <= lax.broadcasted_iota(jnp.int32, (tq, tq), 0))
    acc_ref[...] = jnp.zeros_like(acc_ref)

    def step(j, carry, diagonal):
        start = pl.multiple_of(j * tq, tq)
        ks = k_ref[0, pl.ds(start, tq), :]
        vs = v_ref[0, pl.ds(start, tq), :]
        out = []
        for c in range(2):
            m, l = carry[c]
            s = lax.dot_general(qs[c], ks, _NT, preferred_element_type=F32)
            if diagonal:
                s = jnp.where(causal, s, NEG_BIG)
            m_new = jnp.maximum(m, jnp.max(s, axis=-1, keepdims=True))
            alpha = jnp.exp(m - m_new)
            pr = jnp.exp(s - m_new)
            l_new = alpha * l + jnp.sum(pr, axis=-1, keepdims=True)
            acc_ref[c] = alpha * acc_ref[c] + _dot(pr.astype(BF16), vs)
            out.append((m_new, l_new))
        return tuple(out)

    init = tuple((jnp.full((tq, 1), NEG_BIG, F32), jnp.zeros((tq, 1), F32)) for _ in range(2))
    carry = lax.fori_loop(0, qi, lambda j, cr: step(j, cr, False), init)
    (_, l1), (_, l2) = step(qi, carry, True)

    lam = (jnp.exp(jnp.sum(lq1_ref[...] * lk1_ref[...], axis=-1, keepdims=True))
           - jnp.exp(jnp.sum(lq2_ref[...] * lk2_ref[...], axis=-1, keepdims=True)) + lambda_init)
    o = acc_ref[0] * (1.0 / l1) - lam * (acc_ref[1] * (1.0 / l2))
    o = _rms(o, sw_ref[...], SUBLN_EPS) * (1.0 - lambda_init)
    o_ref[0] = o.astype(o_ref.dtype)


def _attn(q, k, v, lq1, lk1, lq2, lk2, subln_w, *, tq, lambda_init):
    b, t, w = q.shape
    heads = w // DIFF_V
    lam_spec = pl.BlockSpec((1, DIFF_QK), lambda i, h, j: (0, 0))
    return pl.pallas_call(
        functools.partial(_attn_body, tq=tq, lambda_init=lambda_init),
        grid=(b, heads, t // tq),
        in_specs=[lam_spec] * 4 + [
            pl.BlockSpec((1, tq, DIFF_V), lambda i, h, j: (i, j, h)),
            pl.BlockSpec((1, t, DIFF_V), lambda i, h, j: (i, 0, h)),
            pl.BlockSpec((1, t, DIFF_V), lambda i, h, j: (i, 0, h)),
            pl.BlockSpec((1, DIFF_V), lambda i, h, j: (0, 0)),
        ],
        out_specs=pl.BlockSpec((1, tq, DIFF_V), lambda i, h, j: (i, j, h)),
        out_shape=jax.ShapeDtypeStruct((b, t, w), BF16),
        scratch_shapes=[pltpu.VMEM((2, tq, DIFF_V), F32)],
        compiler_params=pltpu.CompilerParams(
            dimension_semantics=("parallel", "parallel", "arbitrary"), vmem_limit_bytes=VMEM_LIMIT),
        name="attn",
    )(lq1.reshape(1, -1), lk1.reshape(1, -1), lq2.reshape(1, -1), lk2.reshape(1, -1),
      q, k, v, subln_w.reshape(1, -1))


def _outproj_body(x_ref, yr_ref, yd_ref, wr_ref, wd_ref, g_ref, o_ref):
    y = _dot(yr_ref[...], wr_ref[...]) + _dot(yd_ref[...], wd_ref[...])
    o_ref[...] = x_ref[...] + _rms(y, g_ref[...], NORM_EPS)


def _outproj(x, y_rwkv, y_diff, w_o, g, *, tm):
    m, d = x.shape
    w_o = w_o.astype(BF16)
    const = lambda shape: pl.BlockSpec(shape, lambda i: (0, 0))
    return pl.pallas_call(
        _outproj_body,
        grid=(m // tm,),
        in_specs=[pl.BlockSpec((tm, d), lambda i: (i, 0)),
                  pl.BlockSpec((tm, RWKV_W), lambda i: (i, 0)),
                  pl.BlockSpec((tm, DIFF_W), lambda i: (i, 0)),
                  const((RWKV_W, d)), const((DIFF_W, d)), const((1, d))],
        out_specs=pl.BlockSpec((tm, d), lambda i: (i, 0)),
        out_shape=jax.ShapeDtypeStruct((m, d), F32),
        compiler_params=pltpu.CompilerParams(
            dimension_semantics=("parallel",), vmem_limit_bytes=VMEM_LIMIT),
        name="outproj",
    )(x, y_rwkv, y_diff, w_o[:RWKV_W], w_o[RWKV_W:], g.reshape(1, d))


def _pick(n, pref):
    return pref if n % pref == 0 else n


def _layer(x, l, p, *, chunk=64, tb_rwkv=256, tm_ffn=1024, tf=256, tm_proj=256, tq=512,
           tm_out=1024):
    b, t, d = x.shape
    m = b * t
    lambda_init = 0.8 - 0.6 * math.exp(-0.3 * l)
    tm_ffn = _pick(m, tm_ffn)
    tf = _pick(p["ffn1_w_gate"].shape[-1], tf)
    x = _ffn(x.reshape(m, d), p["ffn1_pre_g"][l], p["ffn1_post_g"][l], p["ffn1_w_gate"][l],
             p["ffn1_w_up"][l], p["ffn1_w_down"][l], tm=tm_ffn, tf=tf)
    outs = _proj(x.reshape(b, t, d), p["mix_pre_g"][l], p["w_in"][l], p["shift_mu"][l],
                 p["rwkv_w_up"][l], p["rwkv_a_up"][l], p["rwkv_g_up"][l], p["rwkv_w0"][l],
                 p["rwkv_a0"][l], p["rwkv_k_k"][l], p["rwkv_k_a"][l], p["rwkv_r_k"][l].reshape(-1),
                 tm=_pick(t, tm_proj))
    y_rwkv = _rwkv(*outs[:8], p["rwkv_gn_w"][l], p["rwkv_gn_b"][l], chunk=chunk,
                   tb=_pick(t, tb_rwkv))
    y_diff = _attn(*outs[8:], p["diff_lam_q1"][l], p["diff_lam_k1"][l], p["diff_lam_q2"][l],
                   p["diff_lam_k2"][l], p["diff_subln_w"][l], tq=_pick(t, tq),
                   lambda_init=lambda_init)
    x = _outproj(x, y_rwkv.reshape(m, -1), y_diff.reshape(m, -1), p["w_o"][l], p["mix_post_g"][l],
                 tm=_pick(m, tm_out))
    x = _ffn(x, p["ffn2_pre_g"][l], p["ffn2_post_g"][l], p["ffn2_w_gate"][l], p["ffn2_w_up"][l],
             p["ffn2_w_down"][l], tm=tm_ffn, tf=tf)
    return x.reshape(b, t, d)


def kernel(x, ffn1_pre_g, ffn1_post_g, ffn1_w_gate, ffn1_w_up, ffn1_w_down, mix_pre_g, mix_post_g,
           w_in, shift_mu, w_o, rwkv_w0, rwkv_w_up, rwkv_a0, rwkv_a_up, rwkv_g_up, rwkv_k_k,
           rwkv_k_a, rwkv_r_k, rwkv_gn_w, rwkv_gn_b, diff_lam_q1, diff_lam_k1, diff_lam_q2,
           diff_lam_k2, diff_subln_w, ffn2_pre_g, ffn2_post_g, ffn2_w_gate, ffn2_w_up, ffn2_w_down):
    p = dict(ffn1_pre_g=ffn1_pre_g, ffn1_post_g=ffn1_post_g, ffn1_w_gate=ffn1_w_gate,
             ffn1_w_up=ffn1_w_up, ffn1_w_down=ffn1_w_down, mix_pre_g=mix_pre_g,
             mix_post_g=mix_post_g, w_in=w_in, shift_mu=shift_mu, w_o=w_o, rwkv_w0=rwkv_w0,
             rwkv_w_up=rwkv_w_up, rwkv_a0=rwkv_a0, rwkv_a_up=rwkv_a_up, rwkv_g_up=rwkv_g_up,
             rwkv_k_k=rwkv_k_k, rwkv_k_a=rwkv_k_a, rwkv_r_k=rwkv_r_k, rwkv_gn_w=rwkv_gn_w,
             rwkv_gn_b=rwkv_gn_b, diff_lam_q1=diff_lam_q1, diff_lam_k1=diff_lam_k1,
             diff_lam_q2=diff_lam_q2, diff_lam_k2=diff_lam_k2, diff_subln_w=diff_subln_w,
             ffn2_pre_g=ffn2_pre_g, ffn2_post_g=ffn2_post_g, ffn2_w_gate=ffn2_w_gate,
             ffn2_w_up=ffn2_w_up, ffn2_w_down=ffn2_w_down)
    for l in range(ffn1_pre_g.shape[0]):
        x = _layer(x, l, p)
    return x
```

```python
import functools
import math

import jax
import jax.numpy as jnp
from jax import lax
from jax.experimental import pallas as pl
from jax.experimental.pallas import tpu as pltpu

F32 = jnp.float32
BF16 = jnp.bfloat16

RWKV_HEADS = 8
RWKV_HEAD = 64
RWKV_W = RWKV_HEADS * RWKV_HEAD
DIFF_HEADS = 4
DIFF_QK = 64
DIFF_V = 2 * DIFF_QK
DIFF_W = DIFF_HEADS * DIFF_V
DECAY_LORA = 64
ICLR_LORA = 64
GATE_LORA = 160
LORA_W = DECAY_LORA + ICLR_LORA + GATE_LORA
LANES = 128
LORA_PAD = -(-LORA_W // LANES) * LANES
SHIFT_COLS = 3 * RWKV_W + LORA_W
SHIFT_PAD = 3 * RWKV_W + LORA_PAD
FFN_RES = 0.5
NORM_EPS = 1e-6
GN_EPS = 64e-5
SUBLN_EPS = 1e-5
NEG_BIG = -1e30
LOG2E = math.log2(math.e)
VMEM_LIMIT = 56 * 1024 * 1024

_NT = (((1,), (1,)), ((), ()))
_TN = (((0,), (0,)), ((), ()))


def _dot(a, b):
    return jnp.dot(a, b, preferred_element_type=F32)


def _split_dot(x, w_bf16):
    hi = x.astype(BF16)
    lo = (x - hi.astype(F32)).astype(BF16)
    return _dot(hi, w_bf16) + _dot(lo, w_bf16)


def _rms(x, g, eps):
    return x * lax.rsqrt(jnp.mean(x * x, axis=-1, keepdims=True) + eps) * g


def _ffn_body(x_ref, gpre_ref, gpost_ref, wg_ref, wu_ref, wd_ref, o_ref, h_ref, acc_ref):
    f = pl.program_id(1)

    @pl.when(f == 0)
    def _():
        h_ref[...] = _rms(x_ref[...], gpre_ref[...], NORM_EPS).astype(BF16)
        acc_ref[...] = jnp.zeros_like(acc_ref)

    h = h_ref[...]
    g = _dot(h, wg_ref[...])
    u = _dot(h, wu_ref[...])
    a = (g * jax.nn.sigmoid(g) * u).astype(BF16)
    acc_ref[...] += _dot(a, wd_ref[...])

    @pl.when(f == pl.num_programs(1) - 1)
    def _():
        o_ref[...] = x_ref[...] + FFN_RES * _rms(acc_ref[...], gpost_ref[...], NORM_EPS)


def _ffn(x, g_pre, g_post, w_gate, w_up, w_down, *, tm, tf):
    m, d = x.shape
    f = w_gate.shape[1]
    return pl.pallas_call(
        _ffn_body,
        grid=(m // tm, f // tf),
        in_specs=[
            pl.BlockSpec((tm, d), lambda i, j: (i, 0)),
            pl.BlockSpec((1, d), lambda i, j: (0, 0)),
            pl.BlockSpec((1, d), lambda i, j: (0, 0)),
            pl.BlockSpec((d, tf), lambda i, j: (0, j)),
            pl.BlockSpec((d, tf), lambda i, j: (0, j)),
            pl.BlockSpec((tf, d), lambda i, j: (j, 0)),
        ],
        out_specs=pl.BlockSpec((tm, d), lambda i, j: (i, 0)),
        out_shape=jax.ShapeDtypeStruct((m, d), F32),
        scratch_shapes=[pltpu.VMEM((tm, d), BF16), pltpu.VMEM((tm, d), F32)],
        compiler_params=pltpu.CompilerParams(
            dimension_semantics=("parallel", "arbitrary"), vmem_limit_bytes=VMEM_LIMIT),
        name="ffn",
    )(x, g_pre.reshape(1, d), g_post.reshape(1, d), w_gate.astype(BF16), w_up.astype(BF16),
      w_down.astype(BF16))


def _proj_body(x_ref, g_ref, w_ref, mu_ref, wup_ref, ones_ref, w0_ref, a0_ref, kk_ref, ka_ref,
               rk_ref,
               r_out, ld_out, k_out, v_out, kk_out, ic_out, bonus_out, gate_out,
               qd_out, kd_out, vd_out, carry_ref):
    t = pl.program_id(1)

    @pl.when(t == 0)
    def _():
        carry_ref[...] = jnp.zeros_like(carry_ref)

    h = _rms(x_ref[0], g_ref[...], NORM_EPS).astype(BF16)
    p = _dot(h, w_ref[...])
    tm = p.shape[0]

    qd_out[0] = (p[:, SHIFT_PAD:SHIFT_PAD + DIFF_W] * (DIFF_QK ** -0.5 * LOG2E)).astype(BF16)
    kd_out[0] = p[:, SHIFT_PAD + DIFF_W:SHIFT_PAD + 2 * DIFF_W].astype(BF16)
    vd_out[0] = p[:, SHIFT_PAD + 2 * DIFF_W:].astype(BF16)

    ps = p[:, :SHIFT_PAD]
    row = lax.broadcasted_iota(jnp.int32, (tm, 1), 0)
    prev = jnp.where(row == 0, carry_ref[...], pltpu.roll(ps, 1, 0))
    carry_ref[...] = ps[tm - 1:tm, :]
    ps = ps + (prev - ps) * mu_ref[...]

    p_r = ps[:, :RWKV_W]
    p_k = ps[:, RWKV_W:2 * RWKV_W]
    p_v = ps[:, 2 * RWKV_W:3 * RWKV_W]
    z = ps[:, 3 * RWKV_W:]
    lane = lax.broadcasted_iota(jnp.int32, (1, LORA_PAD), 1)
    act = jnp.where(lane < DECAY_LORA, jnp.tanh(z),
                    jnp.where(lane < DECAY_LORA + ICLR_LORA, z, jax.nn.sigmoid(z)))
    up = _dot(act.astype(BF16), wup_ref[...])
    zw = -(w0_ref[...] + up[:, :RWKV_W])
    softplus = jnp.maximum(zw, 0.0) + jnp.log(1.0 + jnp.exp(-jnp.abs(zw)))
    ld = -jnp.exp(-softplus - 0.5)
    iclr = jax.nn.sigmoid(a0_ref[...] + up[:, RWKV_W:2 * RWKV_W])
    gate = up[:, 2 * RWKV_W:]

    ones_bd = ones_ref[...]
    kk = p_k * kk_ref[...]
    kk = kk * lax.rsqrt(jnp.maximum(_split_dot(kk * kk, ones_bd), 1e-24))
    k = p_k * (1.0 + (iclr - 1.0) * ka_ref[...])
    bonus = _split_dot(p_r * k * rk_ref[...], ones_bd) * p_v

    r_out[0] = p_r
    ld_out[0] = ld
    k_out[0] = k
    v_out[0] = p_v
    kk_out[0] = kk
    ic_out[0] = iclr
    bonus_out[0] = bonus
    gate_out[0] = gate


def _head_ones(width, head):
    i = jnp.arange(width) // head
    return (i[:, None] == i[None, :]).astype(BF16)


def _proj(x, g, w_in, shift_mu, w_up, a_up, g_up, w0, a0, k_k, k_a, r_k, *, tm):
    b, t, d = x.shape
    pad = SHIFT_PAD - SHIFT_COLS
    w = jnp.concatenate(
        [w_in[:, :SHIFT_COLS], jnp.zeros((d, pad), F32), w_in[:, SHIFT_COLS:]], axis=1).astype(BF16)
    mu = jnp.concatenate([shift_mu, jnp.zeros((pad,), F32)]).reshape(1, SHIFT_PAD)
    wup = jnp.zeros((LORA_PAD, 3 * RWKV_W), F32)
    wup = wup.at[:DECAY_LORA, :RWKV_W].set(w_up)
    wup = wup.at[DECAY_LORA:DECAY_LORA + ICLR_LORA, RWKV_W:2 * RWKV_W].set(a_up)
    wup = wup.at[DECAY_LORA + ICLR_LORA:LORA_W, 2 * RWKV_W:].set(g_up)
    wup = wup.astype(BF16)
    wcols = w.shape[1]
    vec = lambda a: a.reshape(1, RWKV_W)
    const = lambda shape: pl.BlockSpec(shape, lambda i, j: (0,) * len(shape))
    tile = lambda width: pl.BlockSpec((1, tm, width), lambda i, j: (i, j, 0))
    f32_out = jax.ShapeDtypeStruct((b, t, RWKV_W), F32)
    bf_out = jax.ShapeDtypeStruct((b, t, DIFF_W), BF16)
    return pl.pallas_call(
        _proj_body,
        grid=(b, t // tm),
        in_specs=[tile(d), const((1, d)), const((d, wcols)), const((1, SHIFT_PAD)),
                  const((LORA_PAD, 3 * RWKV_W)), const((RWKV_W, RWKV_W))] + [const((1, RWKV_W))] * 5,
        out_specs=[tile(RWKV_W)] * 8 + [tile(DIFF_W)] * 3,
        out_shape=[f32_out] * 8 + [bf_out] * 3,
        scratch_shapes=[pltpu.VMEM((1, SHIFT_PAD), F32)],
        compiler_params=pltpu.CompilerParams(
            dimension_semantics=("arbitrary", "arbitrary"), vmem_limit_bytes=VMEM_LIMIT),
        name="proj",
    )(x, g.reshape(1, d), w, mu, wup, _head_ones(RWKV_W, RWKV_HEAD),
      vec(w0), vec(a0), vec(k_k), vec(k_a), vec(r_k))


def _rwkv_body(r_ref, ld_ref, k_ref, v_ref, kk_ref, ic_ref, bonus_ref, gate_ref, gnw_ref, gnb_ref,
               o_ref, h_ref, *, chunk):
    c = chunk
    c2 = 2 * c
    nc = r_ref.shape[1] // c

    @pl.when(pl.program_id(1) == 0)
    def _():
        h_ref[...] = jnp.zeros_like(h_ref)

    ri = lax.broadcasted_iota(jnp.int32, (c, c), 0)
    ci = lax.broadcasted_iota(jnp.int32, (c, c), 1)
    tril = (ci <= ri).astype(F32)
    at, rt, bt, kt, bh, kh, v, g_tot = [], [], [], [], [], [], [], []
    for i in range(nc):
        rows = slice(i * c, (i + 1) * c)
        ld = ld_ref[0, rows, :]
        cum = jnp.dot(tril, ld, preferred_element_type=F32, precision=lax.Precision.HIGHEST)
        tot = cum[c - 1:c, :]
        g_inv = jnp.exp(-cum)
        g_rest = jnp.exp(tot - cum)
        kk = kk_ref[0, rows, :]
        b = kk * ic_ref[0, rows, :]
        k = k_ref[0, rows, :]
        at.append((-kk * jnp.exp(cum - ld)).astype(BF16))
        rt.append((r_ref[0, rows, :] * jnp.exp(cum)).astype(BF16))
        bt.append((b * g_inv).astype(BF16))
        kt.append((k * g_inv).astype(BF16))
        bh.append((b * g_rest).astype(BF16))
        kh.append((k * g_rest).astype(BF16))
        v.append(v_ref[0, rows, :].astype(BF16))
        g_tot.append(jnp.exp(tot))

    lane = lax.broadcasted_iota(jnp.int32, (1, LANES), 1)
    lo = lane < RWKV_HEAD

    def stack(x):
        zero = jnp.zeros_like(x)
        return jnp.concatenate([jnp.where(lo, x, zero), jnp.where(lo, zero, x)], axis=0)

    row2 = lax.broadcasted_iota(jnp.int32, (c2, c2), 0)
    col2 = lax.broadcasted_iota(jnp.int32, (c2, c2), 1)
    same = (row2 // c) == (col2 // c)
    strict = same & (col2 < row2)
    incl = same & (col2 <= row2)
    eye = (row2 == col2).astype(F32)
    levels = int(math.log2(c))
    level_masks = [((row2 >> l) == (col2 >> l)) & ((row2 >> (l - 1)) != (col2 >> (l - 1)))
                   & (col2 < row2) for l in range(1, levels + 1)]
    ones_bd = ((lax.broadcasted_iota(jnp.int32, (LANES, LANES), 0) // RWKV_HEAD)
               == (lax.broadcasted_iota(jnp.int32, (LANES, LANES), 1) // RWKV_HEAD)).astype(BF16)
    nt = lambda x, y: lax.dot_general(x, y, _NT, preferred_element_type=F32)
    tn = lambda x, y: lax.dot_general(x, y, _TN, preferred_element_type=F32)

    npairs = RWKV_HEADS // 2
    chains = [(i, p) for i in range(nc) for p in range(npairs)]
    sls = [slice(p * LANES, (p + 1) * LANES) for p in range(npairs)]
    each = lambda f: {ip: f(ip) for ip in chains}
    pick = lambda xs: each(lambda ip: stack(xs[ip[0]][:, sls[ip[1]]]))
    a2, r2, b2, k2, bh2, kh2, v2 = (pick(xs) for xs in (at, rt, bt, kt, bh, kh, v))

    a_ab = each(lambda ip: jnp.where(strict, nt(a2[ip], b2[ip]), 0.0))
    a_ak = each(lambda ip: jnp.where(strict, nt(a2[ip], k2[ip]), 0.0).astype(BF16))
    a_rb = each(lambda ip: jnp.where(incl, nt(r2[ip], b2[ip]), 0.0).astype(BF16))
    a_rk = each(lambda ip: jnp.where(incl, nt(r2[ip], k2[ip]), 0.0).astype(BF16))

    tinv = each(lambda ip: eye + jnp.where(level_masks[0], a_ab[ip], 0.0))
    for l in range(2, levels + 1):
        e = each(lambda ip: jnp.where(level_masks[l - 1], a_ab[ip], 0.0).astype(BF16))
        tb = each(lambda ip: tinv[ip].astype(BF16))
        te = each(lambda ip: _dot(tb[ip], e[ip]).astype(BF16))
        tinv = each(lambda ip: tinv[ip] + _dot(te[ip], tb[ip]))
    tb = each(lambda ip: tinv[ip].astype(BF16))

    akv = each(lambda ip: _dot(a_ak[ip], v2[ip]).astype(BF16))
    wa = each(lambda ip: _dot(tb[ip], a2[ip]).astype(BF16))
    u0 = each(lambda ip: _dot(tb[ip], akv[ip]).astype(BF16))
    pm = each(lambda ip: tn(wa[ip], bh2[ip]).astype(BF16))
    gm = each(lambda ip: tn(u0[ip], bh2[ip]) + tn(v2[ip], kh2[ip]))
    q2 = each(lambda ip: (r2[ip].astype(F32) + _dot(a_rb[ip], wa[ip])).astype(BF16))
    y0 = each(lambda ip: _dot(a_rb[ip], u0[ip]) + _dot(a_rk[ip], v2[ip]))

    state = [h_ref[p] for p in range(npairs)]
    for i in range(nc):
        sb = [s.astype(BF16) for s in state]
        y2 = [nt(q2[i, p], sb[p]) + y0[i, p] for p in range(npairs)]
        state = [g_tot[i][:, sls[p]] * state[p] + _dot(sb[p], pm[i, p]) + gm[i, p]
                 for p in range(npairs)]
        rows = slice(i * c, (i + 1) * c)
        for p in range(npairs):
            sl = sls[p]
            y = y2[p][:c] + y2[p][c:]
            mean = _split_dot(y, ones_bd) * (1.0 / RWKV_HEAD)
            yc = y - mean
            var = _split_dot(yc * yc, ones_bd) * (1.0 / RWKV_HEAD)
            yn = yc * lax.rsqrt(var + GN_EPS) * gnw_ref[:, sl] + gnb_ref[:, sl]
            o_ref[0, rows, sl] = ((yn + bonus_ref[0, rows, sl]) * gate_ref[0, rows, sl]
                                  ).astype(o_ref.dtype)
    for p in range(npairs):
        h_ref[p] = state[p]


def _rwkv(r, ld, k, v, kk, iclr, bonus, gate, gn_w, gn_b, *, chunk, tb):
    b, t, w = r.shape
    tile = pl.BlockSpec((1, tb, w), lambda i, j: (i, j, 0))
    const = pl.BlockSpec((1, w), lambda i, j: (0, 0))
    return pl.pallas_call(
        functools.partial(_rwkv_body, chunk=chunk),
        grid=(b, t // tb),
        in_specs=[tile] * 8 + [const] * 2,
        out_specs=tile,
        out_shape=jax.ShapeDtypeStruct((b, t, w), BF16),
        scratch_shapes=[pltpu.VMEM((RWKV_HEADS // 2, LANES, LANES), F32)],
        compiler_params=pltpu.CompilerParams(
            dimension_semantics=("arbitrary", "arbitrary"), vmem_limit_bytes=VMEM_LIMIT),
        name="rwkv",
    )(r, ld, k, v, kk, iclr, bonus, gate, gn_w.reshape(1, w), gn_b.reshape(1, w))


def _attn_body(lq1_ref, lk1_ref, lq2_ref, lk2_ref, q_ref, k_ref, vt_ref, sw_ref, o_ref, acc_ref,
               s_ref, m_ref, l_ref, *, tq, wq, lambda_init):
    qi = pl.program_id(2)
    q = q_ref[0]
    lane = lax.broadcasted_iota(jnp.int32, (1, LANES), 1)
    zero = jnp.zeros_like(q)
    qs = (jnp.where(lane < DIFF_QK, q, zero), jnp.where(lane < DIFF_QK, zero, q))
    acc_ref[...] = jnp.zeros_like(acc_ref)
    m_ref[...] = jnp.full_like(m_ref, NEG_BIG)
    l_ref[...] = jnp.zeros_like(l_ref)
    items = [(c, w) for c in range(2) for w in range(tq // wq)]

    def scores_into(slot, j):
        ks = k_ref[0, pl.ds(pl.multiple_of(j * tq, tq), tq), :]
        for c in range(2):
            s_ref[slot, c] = lax.dot_general(ks, qs[c], _NT, preferred_element_type=F32)

    def consume(slot, j, diagonal):
        start = pl.multiple_of(j * tq, tq)
        for c, w in items:
            nk = (w + 1) * wq if diagonal else tq
            cols = slice(w * wq, (w + 1) * wq)
            s = s_ref[slot, c, 0:nk, cols]
            if diagonal:
                key = lax.broadcasted_iota(jnp.int32, (nk, wq), 0)
                qry = lax.broadcasted_iota(jnp.int32, (nk, wq), 1) + w * wq
                s = jnp.where(key <= qry, s, NEG_BIG)
            m = m_ref[c, :, cols]
            m_new = jnp.maximum(m, jnp.max(s, axis=0, keepdims=True))
            alpha = jnp.exp2(m - m_new)
            pr = jnp.exp2(s - m_new)
            m_ref[c, :, cols] = m_new
            l_ref[c, :, cols] = alpha * l_ref[c, :, cols] + jnp.sum(pr, axis=0, keepdims=True)
            vt = vt_ref[0, :, pl.ds(start, nk)]
            acc_ref[c, :, cols] = alpha * acc_ref[c, :, cols] + _dot(vt, pr.astype(BF16))

    def body(jj, carry):
        j = 2 * jj
        scores_into(1, j + 1)
        consume(0, j, False)
        scores_into(0, j + 2)
        consume(1, j + 1, False)
        return carry

    scores_into(0, 0)
    lax.fori_loop(0, qi // 2, body, 0)

    @pl.when(qi % 2 == 1)
    def _():
        scores_into(1, qi)
        consume(0, qi - 1, False)
        consume(1, qi, True)

    @pl.when(qi % 2 == 0)
    def _():
        consume(0, qi, True)

    l1, l2 = l_ref[0], l_ref[1]

    lam = (jnp.exp(jnp.sum(lq1_ref[...] * lk1_ref[...], axis=-1, keepdims=True))
           - jnp.exp(jnp.sum(lq2_ref[...] * lk2_ref[...], axis=-1, keepdims=True)) + lambda_init)
    o = acc_ref[0] * (1.0 / l1) - lam * (acc_ref[1] * (1.0 / l2))
    o = o * lax.rsqrt(jnp.mean(o * o, axis=0, keepdims=True) + SUBLN_EPS) * sw_ref[...]
    o = o * (1.0 - lambda_init)
    o_ref[0] = o.T.astype(o_ref.dtype)


def _attn(q, k, v, lq1, lk1, lq2, lk2, subln_w, *, tq, wq, lambda_init):
    b, t, w = q.shape
    heads = w // DIFF_V
    lam_spec = pl.BlockSpec((1, DIFF_QK), lambda i, h, j: (0, 0))
    return pl.pallas_call(
        functools.partial(_attn_body, tq=tq, wq=wq, lambda_init=lambda_init),
        grid=(b, heads, t // tq),
        in_specs=[lam_spec] * 4 + [
            pl.BlockSpec((1, tq, DIFF_V), lambda i, h, j: (i, j, h)),
            pl.BlockSpec((1, t, DIFF_V), lambda i, h, j: (i, 0, h)),
            pl.BlockSpec((1, DIFF_V, t), lambda i, h, j: (i, h, 0)),
            pl.BlockSpec((DIFF_V, 1), lambda i, h, j: (0, 0)),
        ],
        out_specs=pl.BlockSpec((1, tq, DIFF_V), lambda i, h, j: (i, j, h)),
        out_shape=jax.ShapeDtypeStruct((b, t, w), BF16),
        scratch_shapes=[pltpu.VMEM((2, DIFF_V, tq), F32), pltpu.VMEM((2, 2, tq, tq), F32),
                        pltpu.VMEM((2, 1, tq), F32), pltpu.VMEM((2, 1, tq), F32)],
        compiler_params=pltpu.CompilerParams(
            dimension_semantics=("parallel", "parallel", "arbitrary"), vmem_limit_bytes=VMEM_LIMIT),
        name="attn",
    )(lq1.reshape(1, -1), lk1.reshape(1, -1), lq2.reshape(1, -1), lk2.reshape(1, -1),
      q, k, jnp.swapaxes(v, 1, 2), subln_w.reshape(-1, 1))


def _outproj_body(x_ref, yr_ref, yd_ref, wr_ref, wd_ref, g_ref, o_ref):
    y = _dot(yr_ref[...], wr_ref[...]) + _dot(yd_ref[...], wd_ref[...])
    o_ref[...] = x_ref[...] + _rms(y, g_ref[...], NORM_EPS)


def _outproj(x, y_rwkv, y_diff, w_o, g, *, tm):
    m, d = x.shape
    w_o = w_o.astype(BF16)
    const = lambda shape: pl.BlockSpec(shape, lambda i: (0, 0))
    return pl.pallas_call(
        _outproj_body,
        grid=(m // tm,),
        in_specs=[pl.BlockSpec((tm, d), lambda i: (i, 0)),
                  pl.BlockSpec((tm, RWKV_W), lambda i: (i, 0)),
                  pl.BlockSpec((tm, DIFF_W), lambda i: (i, 0)),
                  const((RWKV_W, d)), const((DIFF_W, d)), const((1, d))],
        out_specs=pl.BlockSpec((tm, d), lambda i: (i, 0)),
        out_shape=jax.ShapeDtypeStruct((m, d), F32),
        compiler_params=pltpu.CompilerParams(
            dimension_semantics=("parallel",), vmem_limit_bytes=VMEM_LIMIT),
        name="outproj",
    )(x, y_rwkv, y_diff, w_o[:RWKV_W], w_o[RWKV_W:], g.reshape(1, d))


def _pick(n, pref):
    return pref if n % pref == 0 else n


def _layer(x, l, p, *, chunk=64, tb_rwkv=256, tm_ffn=1024, tf=256, tm_proj=256, tq=512,
           wq_attn=256, tm_out=1024):
    b, t, d = x.shape
    m = b * t
    lambda_init = 0.8 - 0.6 * math.exp(-0.3 * l)
    tm_ffn = _pick(m, tm_ffn)
    tf = _pick(p["ffn1_w_gate"].shape[-1], tf)
    x = _ffn(x.reshape(m, d), p["ffn1_pre_g"][l], p["ffn1_post_g"][l], p["ffn1_w_gate"][l],
             p["ffn1_w_up"][l], p["ffn1_w_down"][l], tm=tm_ffn, tf=tf)
    outs = _proj(x.reshape(b, t, d), p["mix_pre_g"][l], p["w_in"][l], p["shift_mu"][l],
                 p["rwkv_w_up"][l], p["rwkv_a_up"][l], p["rwkv_g_up"][l], p["rwkv_w0"][l],
                 p["rwkv_a0"][l], p["rwkv_k_k"][l], p["rwkv_k_a"][l], p["rwkv_r_k"][l].reshape(-1),
                 tm=_pick(t, tm_proj))
    y_rwkv = _rwkv(*outs[:8], p["rwkv_gn_w"][l], p["rwkv_gn_b"][l], chunk=chunk,
                   tb=_pick(t, tb_rwkv))
    y_diff = _attn(*outs[8:], p["diff_lam_q1"][l], p["diff_lam_k1"][l], p["diff_lam_q2"][l],
                   p["diff_lam_k2"][l], p["diff_subln_w"][l], tq=_pick(t, tq), wq=wq_attn,
                   lambda_init=lambda_init)
    x = _outproj(x, y_rwkv.reshape(m, -1), y_diff.reshape(m, -1), p["w_o"][l], p["mix_post_g"][l],
                 tm=_pick(m, tm_out))
    x = _ffn(x, p["ffn2_pre_g"][l], p["ffn2_post_g"][l], p["ffn2_w_gate"][l], p["ffn2_w_up"][l],
             p["ffn2_w_down"][l], tm=tm_ffn, tf=tf)
    return x.reshape(b, t, d)


def kernel(x, ffn1_pre_g, ffn1_post_g, ffn1_w_gate, ffn1_w_up, ffn1_w_down, mix_pre_g, mix_post_g,
           w_in, shift_mu, w_o, rwkv_w0, rwkv_w_up, rwkv_a0, rwkv_a_up, rwkv_g_up, rwkv_k_k,
           rwkv_k_a, rwkv_r_k, rwkv_gn_w, rwkv_gn_b, diff_lam_q1, diff_lam_k1, diff_lam_q2,
           diff_lam_k2, diff_subln_w, ffn2_pre_g, ffn2_post_g, ffn2_w_gate, ffn2_w_up, ffn2_w_down):
    p = dict(ffn1_pre_g=ffn1_pre_g, ffn1_post_g=ffn1_post_g, ffn1_w_gate=ffn1_w_gate,
             ffn1_w_up=ffn1_w_up, ffn1_w_down=ffn1_w_down, mix_pre_g=mix_pre_g,
             mix_post_g=mix_post_g, w_in=w_in, shift_mu=shift_mu, w_o=w_o, rwkv_w0=rwkv_w0,
             rwkv_w_up=rwkv_w_up, rwkv_a0=rwkv_a0, rwkv_a_up=rwkv_a_up, rwkv_g_up=rwkv_g_up,
             rwkv_k_k=rwkv_k_k, rwkv_k_a=rwkv_k_a, rwkv_r_k=rwkv_r_k, rwkv_gn_w=rwkv_gn_w,
             rwkv_gn_b=rwkv_gn_b, diff_lam_q1=diff_lam_q1, diff_lam_k1=diff_lam_k1,
             diff_lam_q2=diff_lam_q2, diff_lam_k2=diff_lam_k2, diff_subln_w=diff_subln_w,
             ffn2_pre_g=ffn2_pre_g, ffn2_post_g=ffn2_post_g, ffn2_w_gate=ffn2_w_gate,
             ffn2_w_up=ffn2_w_up, ffn2_w_down=ffn2_w_down)
    for l in range(ffn1_pre_g.shape[0]):
        x = _layer(x, l, p)
    return x
```

```python
import functools
import math

import jax
import jax.numpy as jnp
from jax import lax
from jax.experimental import pallas as pl
from jax.experimental.pallas import tpu as pltpu

F32 = jnp.float32
BF16 = jnp.bfloat16

RWKV_HEADS = 8
RWKV_HEAD = 64
RWKV_W = RWKV_HEADS * RWKV_HEAD
DIFF_HEADS = 4
DIFF_QK = 64
DIFF_V = 2 * DIFF_QK
DIFF_W = DIFF_HEADS * DIFF_V
DECAY_LORA = 64
ICLR_LORA = 64
GATE_LORA = 160
LORA_W = DECAY_LORA + ICLR_LORA + GATE_LORA
LANES = 128
LORA_PAD = -(-LORA_W // LANES) * LANES
SHIFT_COLS = 3 * RWKV_W + LORA_W
SHIFT_PAD = 3 * RWKV_W + LORA_PAD
FFN_RES = 0.5
NORM_EPS = 1e-6
GN_EPS = 64e-5
SUBLN_EPS = 1e-5
NEG_BIG = -1e30
LOG2E = math.log2(math.e)
VMEM_LIMIT = 56 * 1024 * 1024

_NT = (((1,), (1,)), ((), ()))
_TN = (((0,), (0,)), ((), ()))


def _dot(a, b):
    return jnp.dot(a, b, preferred_element_type=F32)


def _split_dot(x, w_bf16):
    hi = x.astype(BF16)
    lo = (x - hi.astype(F32)).astype(BF16)
    return _dot(hi, w_bf16) + _dot(lo, w_bf16)


def _split_dot_k2(x, w2_bf16):
    hi = x.astype(BF16)
    lo = (x - hi.astype(F32)).astype(BF16)
    return _dot(jnp.concatenate([hi, lo], axis=1), w2_bf16)


def _rms(x, g, eps):
    return x * lax.rsqrt(jnp.mean(x * x, axis=-1, keepdims=True) + eps) * g


def _ffn_body(x_ref, gpre_ref, gpost_ref, wg_ref, wu_ref, wd_ref, o_ref, h_ref, acc_ref):
    f = pl.program_id(1)

    @pl.when(f == 0)
    def _():
        h_ref[...] = _rms(x_ref[...], gpre_ref[...], NORM_EPS).astype(BF16)
        acc_ref[...] = jnp.zeros_like(acc_ref)

    h = h_ref[...]
    g = _dot(h, wg_ref[...])
    u = _dot(h, wu_ref[...])
    a = (g * jax.nn.sigmoid(g) * u).astype(BF16)
    acc_ref[...] += _dot(a, wd_ref[...])

    @pl.when(f == pl.num_programs(1) - 1)
    def _():
        o_ref[...] = x_ref[...] + FFN_RES * _rms(acc_ref[...], gpost_ref[...], NORM_EPS)


def _ffn(x, g_pre, g_post, w_gate, w_up, w_down, *, tm, tf):
    m, d = x.shape
    f = w_gate.shape[1]
    return pl.pallas_call(
        _ffn_body,
        grid=(m // tm, f // tf),
        in_specs=[
            pl.BlockSpec((tm, d), lambda i, j: (i, 0)),
            pl.BlockSpec((1, d), lambda i, j: (0, 0)),
            pl.BlockSpec((1, d), lambda i, j: (0, 0)),
            pl.BlockSpec((d, tf), lambda i, j: (0, j)),
            pl.BlockSpec((d, tf), lambda i, j: (0, j)),
            pl.BlockSpec((tf, d), lambda i, j: (j, 0)),
        ],
        out_specs=pl.BlockSpec((tm, d), lambda i, j: (i, 0)),
        out_shape=jax.ShapeDtypeStruct((m, d), F32),
        scratch_shapes=[pltpu.VMEM((tm, d), BF16), pltpu.VMEM((tm, d), F32)],
        compiler_params=pltpu.CompilerParams(
            dimension_semantics=("parallel", "arbitrary"), vmem_limit_bytes=VMEM_LIMIT),
        name="ffn",
    )(x, g_pre.reshape(1, d), g_post.reshape(1, d), w_gate.astype(BF16), w_up.astype(BF16),
      w_down.astype(BF16))


def _proj_body(x_ref, g_ref, w_ref, mu_ref, wup_ref, ones_ref, w0_ref, a0_ref, kk_ref, ka_ref,
               rk_ref,
               r_out, ld_out, k_out, v_out, kk_out, ic_out, bonus_out, gate_out,
               qd_out, kd_out, vd_out, carry_ref):
    t = pl.program_id(1)

    @pl.when(t == 0)
    def _():
        carry_ref[...] = jnp.zeros_like(carry_ref)

    h = _rms(x_ref[0], g_ref[...], NORM_EPS).astype(BF16)
    p = _dot(h, w_ref[...])
    tm = p.shape[0]

    qd_out[0] = (p[:, SHIFT_PAD:SHIFT_PAD + DIFF_W] * (DIFF_QK ** -0.5 * LOG2E)).astype(BF16)
    kd_out[0] = p[:, SHIFT_PAD + DIFF_W:SHIFT_PAD + 2 * DIFF_W].astype(BF16)
    vd_out[0] = p[:, SHIFT_PAD + 2 * DIFF_W:].astype(BF16)

    ps = p[:, :SHIFT_PAD]
    row = lax.broadcasted_iota(jnp.int32, (tm, 1), 0)
    prev = jnp.where(row == 0, carry_ref[...], pltpu.roll(ps, 1, 0))
    carry_ref[...] = ps[tm - 1:tm, :]
    ps = ps + (prev - ps) * mu_ref[...]

    p_r = ps[:, :RWKV_W]
    p_k = ps[:, RWKV_W:2 * RWKV_W]
    p_v = ps[:, 2 * RWKV_W:3 * RWKV_W]
    z = ps[:, 3 * RWKV_W:]
    lane = lax.broadcasted_iota(jnp.int32, (1, LORA_PAD), 1)
    act = jnp.where(lane < DECAY_LORA, jnp.tanh(z),
                    jnp.where(lane < DECAY_LORA + ICLR_LORA, z, jax.nn.sigmoid(z)))
    up = _dot(act.astype(BF16), wup_ref[...])
    zw = -(w0_ref[...] + up[:, :RWKV_W])
    softplus = jnp.maximum(zw, 0.0) + jnp.log(1.0 + jnp.exp(-jnp.abs(zw)))
    ld = -jnp.exp(-softplus - 0.5)
    iclr = jax.nn.sigmoid(a0_ref[...] + up[:, RWKV_W:2 * RWKV_W])
    gate = up[:, 2 * RWKV_W:]

    ones_bd = ones_ref[...]
    kk = p_k * kk_ref[...]
    kk = kk * lax.rsqrt(jnp.maximum(_split_dot(kk * kk, ones_bd), 1e-24))
    k = p_k * (1.0 + (iclr - 1.0) * ka_ref[...])
    bonus = _split_dot(p_r * k * rk_ref[...], ones_bd) * p_v

    r_out[0] = p_r
    ld_out[0] = ld
    k_out[0] = k
    v_out[0] = p_v
    kk_out[0] = kk
    ic_out[0] = iclr
    bonus_out[0] = bonus
    gate_out[0] = gate


def _head_ones(width, head):
    i = jnp.arange(width) // head
    return (i[:, None] == i[None, :]).astype(BF16)


def _proj(x, g, w_in, shift_mu, w_up, a_up, g_up, w0, a0, k_k, k_a, r_k, *, tm):
    b, t, d = x.shape
    pad = SHIFT_PAD - SHIFT_COLS
    w = jnp.concatenate(
        [w_in[:, :SHIFT_COLS], jnp.zeros((d, pad), F32), w_in[:, SHIFT_COLS:]], axis=1).astype(BF16)
    mu = jnp.concatenate([shift_mu, jnp.zeros((pad,), F32)]).reshape(1, SHIFT_PAD)
    wup = jnp.zeros((LORA_PAD, 3 * RWKV_W), F32)
    wup = wup.at[:DECAY_LORA, :RWKV_W].set(w_up)
    wup = wup.at[DECAY_LORA:DECAY_LORA + ICLR_LORA, RWKV_W:2 * RWKV_W].set(a_up)
    wup = wup.at[DECAY_LORA + ICLR_LORA:LORA_W, 2 * RWKV_W:].set(g_up)
    wup = wup.astype(BF16)
    wcols = w.shape[1]
    vec = lambda a: a.reshape(1, RWKV_W)
    const = lambda shape: pl.BlockSpec(shape, lambda i, j: (0,) * len(shape))
    tile = lambda width: pl.BlockSpec((1, tm, width), lambda i, j: (i, j, 0))
    f32_out = jax.ShapeDtypeStruct((b, t, RWKV_W), F32)
    bf_out = jax.ShapeDtypeStruct((b, t, DIFF_W), BF16)
    return pl.pallas_call(
        _proj_body,
        grid=(b, t // tm),
        in_specs=[tile(d), const((1, d)), const((d, wcols)), const((1, SHIFT_PAD)),
                  const((LORA_PAD, 3 * RWKV_W)), const((RWKV_W, RWKV_W))] + [const((1, RWKV_W))] * 5,
        out_specs=[tile(RWKV_W)] * 8 + [tile(DIFF_W)] * 3,
        out_shape=[f32_out] * 8 + [bf_out] * 3,
        scratch_shapes=[pltpu.VMEM((1, SHIFT_PAD), F32)],
        compiler_params=pltpu.CompilerParams(
            dimension_semantics=("arbitrary", "arbitrary"), vmem_limit_bytes=VMEM_LIMIT),
        name="proj",
    )(x, g.reshape(1, d), w, mu, wup, _head_ones(RWKV_W, RWKV_HEAD),
      vec(w0), vec(a0), vec(k_k), vec(k_a), vec(r_k))


def _rwkv_body(r_ref, ld_ref, k_ref, v_ref, kk_ref, ic_ref, bonus_ref, gate_ref, gnw_ref, gnb_ref,
               o_ref, h_ref, *, chunk):
    c = chunk
    c2 = 2 * c
    nc = r_ref.shape[1] // c

    @pl.when(pl.program_id(1) == 0)
    def _():
        h_ref[...] = jnp.zeros_like(h_ref)

    ri = lax.broadcasted_iota(jnp.int32, (c, c), 0)
    ci = lax.broadcasted_iota(jnp.int32, (c, c), 1)
    tril = (ci <= ri).astype(F32)
    at, rt, bt, kt, bh, kh, v, g_tot = [], [], [], [], [], [], [], []
    for i in range(nc):
        rows = slice(i * c, (i + 1) * c)
        ld = ld_ref[0, rows, :]
        cum = jnp.dot(tril, ld, preferred_element_type=F32, precision=lax.Precision.HIGHEST)
        tot = cum[c - 1:c, :]
        g_inv = jnp.exp(-cum)
        g_rest = jnp.exp(tot - cum)
        kk = kk_ref[0, rows, :]
        b = kk * ic_ref[0, rows, :]
        k = k_ref[0, rows, :]
        at.append((-kk * jnp.exp(cum - ld)).astype(BF16))
        rt.append((r_ref[0, rows, :] * jnp.exp(cum)).astype(BF16))
        bt.append((b * g_inv).astype(BF16))
        kt.append((k * g_inv).astype(BF16))
        bh.append((b * g_rest).astype(BF16))
        kh.append((k * g_rest).astype(BF16))
        v.append(v_ref[0, rows, :].astype(BF16))
        g_tot.append(jnp.exp(tot))

    lane = lax.broadcasted_iota(jnp.int32, (1, LANES), 1)
    lo = lane < RWKV_HEAD

    def stack(x):
        zero = jnp.zeros_like(x)
        return jnp.concatenate([jnp.where(lo, x, zero), jnp.where(lo, zero, x)], axis=0)

    row2 = lax.broadcasted_iota(jnp.int32, (c2, c2), 0)
    col2 = lax.broadcasted_iota(jnp.int32, (c2, c2), 1)
    same = (row2 // c) == (col2 // c)
    strict = same & (col2 < row2)
    incl = same & (col2 <= row2)
    eye = (row2 == col2).astype(F32)
    levels = int(math.log2(c))
    level_masks = [((row2 >> l) == (col2 >> l)) & ((row2 >> (l - 1)) != (col2 >> (l - 1)))
                   & (col2 < row2) for l in range(1, levels + 1)]
    ones_bd = ((lax.broadcasted_iota(jnp.int32, (LANES, LANES), 0) // RWKV_HEAD)
               == (lax.broadcasted_iota(jnp.int32, (LANES, LANES), 1) // RWKV_HEAD)).astype(BF16)
    ones_bd2 = jnp.concatenate([ones_bd, ones_bd], axis=0)
    nt = lambda x, y: lax.dot_general(x, y, _NT, preferred_element_type=F32)
    tn = lambda x, y: lax.dot_general(x, y, _TN, preferred_element_type=F32)

    npairs = RWKV_HEADS // 2
    chains = [(i, p) for i in range(nc) for p in range(npairs)]
    sls = [slice(p * LANES, (p + 1) * LANES) for p in range(npairs)]
    each = lambda f: {ip: f(ip) for ip in chains}
    pick = lambda xs: each(lambda ip: stack(xs[ip[0]][:, sls[ip[1]]]))
    a2, r2, b2, k2, bh2, kh2, v2 = (pick(xs) for xs in (at, rt, bt, kt, bh, kh, v))

    cat0 = lambda *xs: jnp.concatenate(xs, axis=0)
    cat1 = lambda *xs: jnp.concatenate(xs, axis=1)
    s_ar = each(lambda ip: nt(cat0(a2[ip], r2[ip]), cat0(b2[ip], k2[ip])))
    a_ab = each(lambda ip: jnp.where(strict, s_ar[ip][:c2, :c2], 0.0))
    a_ak = each(lambda ip: jnp.where(strict, s_ar[ip][:c2, c2:], 0.0).astype(BF16))
    a_rb = each(lambda ip: jnp.where(incl, s_ar[ip][c2:, :c2], 0.0).astype(BF16))
    a_rk = each(lambda ip: jnp.where(incl, s_ar[ip][c2:, c2:], 0.0).astype(BF16))

    tinv = each(lambda ip: eye + jnp.where(level_masks[0], a_ab[ip], 0.0))
    for l in range(2, levels + 1):
        e = each(lambda ip: jnp.where(level_masks[l - 1], a_ab[ip], 0.0).astype(BF16))
        tb = each(lambda ip: tinv[ip].astype(BF16))
        te = each(lambda ip: _dot(tb[ip], e[ip]).astype(BF16))
        tinv = each(lambda ip: tinv[ip] + _dot(te[ip], tb[ip]))
    tb = each(lambda ip: tinv[ip].astype(BF16))

    av = each(lambda ip: _dot(cat0(a_ak[ip], a_rk[ip]), v2[ip]))
    akv = each(lambda ip: av[ip][:c2].astype(BF16))
    wu = each(lambda ip: _dot(tb[ip], cat1(a2[ip], akv[ip])).astype(BF16))
    pg = each(lambda ip: tn(bh2[ip], wu[ip]))
    gm = each(lambda ip: pg[ip][:, LANES:] + tn(kh2[ip], v2[ip]))
    qy = each(lambda ip: _dot(a_rb[ip], wu[ip]))
    qp = each(lambda ip: cat0((r2[ip].astype(F32) + qy[ip][:, :LANES]).astype(BF16),
                              pg[ip][:, :LANES].astype(BF16)))
    y0 = each(lambda ip: qy[ip][:, LANES:] + av[ip][c2:])
    assert c2 == LANES
    g_col = each(lambda ip: jnp.sum(eye * g_tot[ip[0]][:, sls[ip[1]]], axis=1, keepdims=True))

    state = [h_ref[p] for p in range(npairs)]
    ys = [[] for _ in range(npairs)]
    for i in range(nc):
        for p in range(npairs):
            qh = _dot(qp[i, p], state[p].astype(BF16))
            y2 = qh[:c2] + y0[i, p]
            ys[p].append(y2[:c] + y2[c:])
            state[p] = g_col[i, p] * state[p] + qh[c2:] + gm[i, p]
    for p in range(npairs):
        h_ref[p] = state[p]

    for p in range(npairs):
        sl = sls[p]
        y = cat0(*ys[p])
        mean = _split_dot_k2(y, ones_bd2) * (1.0 / RWKV_HEAD)
        yc = y - mean
        var = _split_dot_k2(yc * yc, ones_bd2) * (1.0 / RWKV_HEAD)
        yn = yc * lax.rsqrt(var + GN_EPS) * gnw_ref[:, sl] + gnb_ref[:, sl]
        o_ref[0, :, sl] = ((yn + bonus_ref[0, :, sl]) * gate_ref[0, :, sl]).astype(o_ref.dtype)


def _rwkv(r, ld, k, v, kk, iclr, bonus, gate, gn_w, gn_b, *, chunk, tb):
    b, t, w = r.shape
    tile = pl.BlockSpec((1, tb, w), lambda i, j: (i, j, 0))
    const = pl.BlockSpec((1, w), lambda i, j: (0, 0))
    return pl.pallas_call(
        functools.partial(_rwkv_body, chunk=chunk),
        grid=(b, t // tb),
        in_specs=[tile] * 8 + [const] * 2,
        out_specs=tile,
        out_shape=jax.ShapeDtypeStruct((b, t, w), BF16),
        scratch_shapes=[pltpu.VMEM((RWKV_HEADS // 2, LANES, LANES), F32)],
        compiler_params=pltpu.CompilerParams(
            dimension_semantics=("arbitrary", "arbitrary"), vmem_limit_bytes=VMEM_LIMIT),
        name="rwkv",
    )(r, ld, k, v, kk, iclr, bonus, gate, gn_w.reshape(1, w), gn_b.reshape(1, w))


def _attn_body(lq1_ref, lk1_ref, lq2_ref, lk2_ref, q_ref, k_ref, vt_ref, sw_ref, o_ref, acc_ref,
               s_ref, m_ref, l_ref, *, tq, wq, lambda_init):
    qi = pl.program_id(2)
    q = q_ref[0]
    lane = lax.broadcasted_iota(jnp.int32, (1, LANES), 1)
    zero = jnp.zeros_like(q)
    qs = (jnp.where(lane < DIFF_QK, q, zero), jnp.where(lane < DIFF_QK, zero, q))
    acc_ref[...] = jnp.zeros_like(acc_ref)
    m_ref[...] = jnp.full_like(m_ref, NEG_BIG)
    l_ref[...] = jnp.zeros_like(l_ref)
    items = [(c, w) for c in range(2) for w in range(tq // wq)]

    def scores_into(slot, j):
        ks = k_ref[0, pl.ds(pl.multiple_of(j * tq, tq), tq), :]
        for c in range(2):
            s_ref[slot, c] = lax.dot_general(ks, qs[c], _NT, preferred_element_type=F32)

    def consume(slot, j, diagonal):
        start = pl.multiple_of(j * tq, tq)
        for c, w in items:
            nk = (w + 1) * wq if diagonal else tq
            cols = slice(w * wq, (w + 1) * wq)
            s = s_ref[slot, c, 0:nk, cols]
            if diagonal:
                key = lax.broadcasted_iota(jnp.int32, (nk, wq), 0)
                qry = lax.broadcasted_iota(jnp.int32, (nk, wq), 1) + w * wq
                s = jnp.where(key <= qry, s, NEG_BIG)
            m = m_ref[c, :, cols]
            m_new = jnp.maximum(m, jnp.max(s, axis=0, keepdims=True))
            alpha = jnp.exp2(m - m_new)
            pr = jnp.exp2(s - m_new)
            m_ref[c, :, cols] = m_new
            l_ref[c, :, cols] = alpha * l_ref[c, :, cols] + jnp.sum(pr, axis=0, keepdims=True)
            vt = vt_ref[0, :, pl.ds(start, nk)]
            acc_ref[c, :, cols] = alpha * acc_ref[c, :, cols] + _dot(vt, pr.astype(BF16))

    def body(jj, carry):
        j = 2 * jj
        scores_into(1, j + 1)
        consume(0, j, False)
        scores_into(0, j + 2)
        consume(1, j + 1, False)
        return carry

    scores_into(0, 0)
    lax.fori_loop(0, qi // 2, body, 0)

    @pl.when(qi % 2 == 1)
    def _():
        scores_into(1, qi)
        consume(0, qi - 1, False)
        consume(1, qi, True)

    @pl.when(qi % 2 == 0)
    def _():
        consume(0, qi, True)

    l1, l2 = l_ref[0], l_ref[1]

    lam = (jnp.exp(jnp.sum(lq1_ref[...] * lk1_ref[...], axis=-1, keepdims=True))
           - jnp.exp(jnp.sum(lq2_ref[...] * lk2_ref[...], axis=-1, keepdims=True)) + lambda_init)
    o = acc_ref[0] * (1.0 / l1) - lam * (acc_ref[1] * (1.0 / l2))
    o = o * lax.rsqrt(jnp.mean(o * o, axis=0, keepdims=True) + SUBLN_EPS) * sw_ref[...]
    o = o * (1.0 - lambda_init)
    o_ref[0] = o.T.astype(o_ref.dtype)


def _attn(q, k, v, lq1, lk1, lq2, lk2, subln_w, *, tq, wq, lambda_init):
    b, t, w = q.shape
    heads = w // DIFF_V
    lam_spec = pl.BlockSpec((1, DIFF_QK), lambda i, h, j: (0, 0))
    return pl.pallas_call(
        functools.partial(_attn_body, tq=tq, wq=wq, lambda_init=lambda_init),
        grid=(b, heads, t // tq),
        in_specs=[lam_spec] * 4 + [
            pl.BlockSpec((1, tq, DIFF_V), lambda i, h, j: (i, j, h)),
            pl.BlockSpec((1, t, DIFF_V), lambda i, h, j: (i, 0, h)),
            pl.BlockSpec((1, DIFF_V, t), lambda i, h, j: (i, h, 0)),
            pl.BlockSpec((DIFF_V, 1), lambda i, h, j: (0, 0)),
        ],
        out_specs=pl.BlockSpec((1, tq, DIFF_V), lambda i, h, j: (i, j, h)),
        out_shape=jax.ShapeDtypeStruct((b, t, w), BF16),
        scratch_shapes=[pltpu.VMEM((2, DIFF_V, tq), F32), pltpu.VMEM((2, 2, tq, tq), F32),
                        pltpu.VMEM((2, 1, tq), F32), pltpu.VMEM((2, 1, tq), F32)],
        compiler_params=pltpu.CompilerParams(
            dimension_semantics=("parallel", "parallel", "arbitrary"), vmem_limit_bytes=VMEM_LIMIT),
        name="attn",
    )(lq1.reshape(1, -1), lk1.reshape(1, -1), lq2.reshape(1, -1), lk2.reshape(1, -1),
      q, k, jnp.swapaxes(v, 1, 2), subln_w.reshape(-1, 1))


def _outproj_body(x_ref, yr_ref, yd_ref, wr_ref, wd_ref, g_ref, o_ref):
    y = _dot(yr_ref[...], wr_ref[...]) + _dot(yd_ref[...], wd_ref[...])
    o_ref[...] = x_ref[...] + _rms(y, g_ref[...], NORM_EPS)


def _outproj(x, y_rwkv, y_diff, w_o, g, *, tm):
    m, d = x.shape
    w_o = w_o.astype(BF16)
    const = lambda shape: pl.BlockSpec(shape, lambda i: (0, 0))
    return pl.pallas_call(
        _outproj_body,
        grid=(m // tm,),
        in_specs=[pl.BlockSpec((tm, d), lambda i: (i, 0)),
                  pl.BlockSpec((tm, RWKV_W), lambda i: (i, 0)),
                  pl.BlockSpec((tm, DIFF_W), lambda i: (i, 0)),
                  const((RWKV_W, d)), const((DIFF_W, d)), const((1, d))],
        out_specs=pl.BlockSpec((tm, d), lambda i: (i, 0)),
        out_shape=jax.ShapeDtypeStruct((m, d), F32),
        compiler_params=pltpu.CompilerParams(
            dimension_semantics=("parallel",), vmem_limit_bytes=VMEM_LIMIT),
        name="outproj",
    )(x, y_rwkv, y_diff, w_o[:RWKV_W], w_o[RWKV_W:], g.reshape(1, d))


def _pick(n, pref):
    return pref if n % pref == 0 else n


def _layer(x, l, p, *, chunk=64, tb_rwkv=256, tm_ffn=1024, tf=256, tm_proj=256, tq=512,
           wq_attn=256, tm_out=1024):
    b, t, d = x.shape
    m = b * t
    lambda_init = 0.8 - 0.6 * math.exp(-0.3 * l)
    tm_ffn = _pick(m, tm_ffn)
    tf = _pick(p["ffn1_w_gate"].shape[-1], tf)
    x = _ffn(x.reshape(m, d), p["ffn1_pre_g"][l], p["ffn1_post_g"][l], p["ffn1_w_gate"][l],
             p["ffn1_w_up"][l], p["ffn1_w_down"][l], tm=tm_ffn, tf=tf)
    outs = _proj(x.reshape(b, t, d), p["mix_pre_g"][l], p["w_in"][l], p["shift_mu"][l],
                 p["rwkv_w_up"][l], p["rwkv_a_up"][l], p["rwkv_g_up"][l], p["rwkv_w0"][l],
                 p["rwkv_a0"][l], p["rwkv_k_k"][l], p["rwkv_k_a"][l], p["rwkv_r_k"][l].reshape(-1),
                 tm=_pick(t, tm_proj))
    y_rwkv = _rwkv(*outs[:8], p["rwkv_gn_w"][l], p["rwkv_gn_b"][l], chunk=chunk,
                   tb=_pick(t, tb_rwkv))
    y_diff = _attn(*outs[8:], p["diff_lam_q1"][l], p["diff_lam_k1"][l], p["diff_lam_q2"][l],
                   p["diff_lam_k2"][l], p["diff_subln_w"][l], tq=_pick(t, tq), wq=wq_attn,
                   lambda_init=lambda_init)
    x = _outproj(x, y_rwkv.reshape(m, -1), y_diff.reshape(m, -1), p["w_o"][l], p["mix_post_g"][l],
                 tm=_pick(m, tm_out))
    x = _ffn(x, p["ffn2_pre_g"][l], p["ffn2_post_g"][l], p["ffn2_w_gate"][l], p["ffn2_w_up"][l],
             p["ffn2_w_down"][l], tm=tm_ffn, tf=tf)
    return x.reshape(b, t, d)


def kernel(x, ffn1_pre_g, ffn1_post_g, ffn1_w_gate, ffn1_w_up, ffn1_w_down, mix_pre_g, mix_post_g,
           w_in, shift_mu, w_o, rwkv_w0, rwkv_w_up, rwkv_a0, rwkv_a_up, rwkv_g_up, rwkv_k_k,
           rwkv_k_a, rwkv_r_k, rwkv_gn_w, rwkv_gn_b, diff_lam_q1, diff_lam_k1, diff_lam_q2,
           diff_lam_k2, diff_subln_w, ffn2_pre_g, ffn2_post_g, ffn2_w_gate, ffn2_w_up, ffn2_w_down):
    p = dict(ffn1_pre_g=ffn1_pre_g, ffn1_post_g=ffn1_post_g, ffn1_w_gate=ffn1_w_gate,
             ffn1_w_up=ffn1_w_up, ffn1_w_down=ffn1_w_down, mix_pre_g=mix_pre_g,
             mix_post_g=mix_post_g, w_in=w_in, shift_mu=shift_mu, w_o=w_o, rwkv_w0=rwkv_w0,
             rwkv_w_up=rwkv_w_up, rwkv_a0=rwkv_a0, rwkv_a_up=rwkv_a_up, rwkv_g_up=rwkv_g_up,
             rwkv_k_k=rwkv_k_k, rwkv_k_a=rwkv_k_a, rwkv_r_k=rwkv_r_k, rwkv_gn_w=rwkv_gn_w,
             rwkv_gn_b=rwkv_gn_b, diff_lam_q1=diff_lam_q1, diff_lam_k1=diff_lam_k1,
             diff_lam_q2=diff_lam_q2, diff_lam_k2=diff_lam_k2, diff_subln_w=diff_subln_w,
             ffn2_pre_g=ffn2_pre_g, ffn2_post_g=ffn2_post_g, ffn2_w_gate=ffn2_w_gate,
             ffn2_w_up=ffn2_w_up, ffn2_w_down=ffn2_w_down)
    for l in range(ffn1_pre_g.shape[0]):
        x = _layer(x, l, p)
    return x
```

```python
import functools
import math

import jax
import jax.numpy as jnp
from jax import lax
from jax.experimental import pallas as pl
from jax.experimental.pallas import tpu as pltpu

F32 = jnp.float32
BF16 = jnp.bfloat16

RWKV_HEADS = 8
RWKV_HEAD = 64
RWKV_W = RWKV_HEADS * RWKV_HEAD
DIFF_HEADS = 4
DIFF_QK = 64
DIFF_V = 2 * DIFF_QK
DIFF_W = DIFF_HEADS * DIFF_V
DECAY_LORA = 64
ICLR_LORA = 64
GATE_LORA = 160
LORA_W = DECAY_LORA + ICLR_LORA + GATE_LORA
LANES = 128
LORA_PAD = -(-LORA_W // LANES) * LANES
SHIFT_COLS = 3 * RWKV_W + LORA_W
SHIFT_PAD = 3 * RWKV_W + LORA_PAD
FFN_RES = 0.5
NORM_EPS = 1e-6
GN_EPS = 64e-5
SUBLN_EPS = 1e-5
NEG_BIG = -1e30
LOG2E = math.log2(math.e)
VMEM_LIMIT = 56 * 1024 * 1024

_NT = (((1,), (1,)), ((), ()))
_TN = (((0,), (0,)), ((), ()))


def _dot(a, b):
    return jnp.dot(a, b, preferred_element_type=F32)


def _split_dot(x, w_bf16):
    hi = x.astype(BF16)
    lo = (x - hi.astype(F32)).astype(BF16)
    return _dot(hi, w_bf16) + _dot(lo, w_bf16)


def _split_dot_k2(x, w2_bf16):
    hi = x.astype(BF16)
    lo = (x - hi.astype(F32)).astype(BF16)
    return _dot(jnp.concatenate([hi, lo], axis=1), w2_bf16)


def _rms(x, g, eps):
    return x * lax.rsqrt(jnp.mean(x * x, axis=-1, keepdims=True) + eps) * g


def _ffn_body(x_ref, gpre_ref, gpost_ref, wg_ref, wu_ref, wd_ref, o_ref, h_ref, acc_ref):
    f = pl.program_id(1)

    @pl.when(f == 0)
    def _():
        h_ref[...] = _rms(x_ref[...], gpre_ref[...], NORM_EPS).astype(BF16)
        acc_ref[...] = jnp.zeros_like(acc_ref)

    h = h_ref[...]
    g = _dot(h, wg_ref[...])
    u = _dot(h, wu_ref[...])
    a = (g * jax.nn.sigmoid(g) * u).astype(BF16)
    acc_ref[...] += _dot(a, wd_ref[...])

    @pl.when(f == pl.num_programs(1) - 1)
    def _():
        o_ref[...] = x_ref[...] + FFN_RES * _rms(acc_ref[...], gpost_ref[...], NORM_EPS)


def _ffn(x, g_pre, g_post, w_gate, w_up, w_down, *, tm, tf):
    m, d = x.shape
    f = w_gate.shape[1]
    return pl.pallas_call(
        _ffn_body,
        grid=(m // tm, f // tf),
        in_specs=[
            pl.BlockSpec((tm, d), lambda i, j: (i, 0)),
            pl.BlockSpec((1, d), lambda i, j: (0, 0)),
            pl.BlockSpec((1, d), lambda i, j: (0, 0)),
            pl.BlockSpec((d, tf), lambda i, j: (0, j)),
            pl.BlockSpec((d, tf), lambda i, j: (0, j)),
            pl.BlockSpec((tf, d), lambda i, j: (j, 0)),
        ],
        out_specs=pl.BlockSpec((tm, d), lambda i, j: (i, 0)),
        out_shape=jax.ShapeDtypeStruct((m, d), F32),
        scratch_shapes=[pltpu.VMEM((tm, d), BF16), pltpu.VMEM((tm, d), F32)],
        compiler_params=pltpu.CompilerParams(
            dimension_semantics=("parallel", "arbitrary"), vmem_limit_bytes=VMEM_LIMIT),
        name="ffn",
    )(x, g_pre.reshape(1, d), g_post.reshape(1, d), w_gate.astype(BF16), w_up.astype(BF16),
      w_down.astype(BF16))


def _proj_body(x_ref, g_ref, w_ref, mu_ref, wup_ref, ones_ref, w0_ref, a0_ref, kk_ref, ka_ref,
               rk_ref,
               r_out, ld_out, k_out, v_out, kk_out, ic_out, bonus_out, gate_out,
               qd_out, kd_out, vd_out, carry_ref):
    t = pl.program_id(1)

    @pl.when(t == 0)
    def _():
        carry_ref[...] = jnp.zeros_like(carry_ref)

    h = _rms(x_ref[0], g_ref[...], NORM_EPS).astype(BF16)
    p = _dot(h, w_ref[...])
    tm = p.shape[0]

    qd_out[0] = (p[:, SHIFT_PAD:SHIFT_PAD + DIFF_W] * (DIFF_QK ** -0.5 * LOG2E)).astype(BF16)
    kd_out[0] = p[:, SHIFT_PAD + DIFF_W:SHIFT_PAD + 2 * DIFF_W].astype(BF16)
    vd_out[0] = p[:, SHIFT_PAD + 2 * DIFF_W:].T.astype(BF16)

    ps = p[:, :SHIFT_PAD]
    row = lax.broadcasted_iota(jnp.int32, (tm, 1), 0)
    prev = jnp.where(row == 0, carry_ref[...], pltpu.roll(ps, 1, 0))
    carry_ref[...] = ps[tm - 1:tm, :]
    ps = ps + (prev - ps) * mu_ref[...]

    p_r = ps[:, :RWKV_W]
    p_k = ps[:, RWKV_W:2 * RWKV_W]
    p_v = ps[:, 2 * RWKV_W:3 * RWKV_W]
    z = ps[:, 3 * RWKV_W:]
    lane = lax.broadcasted_iota(jnp.int32, (1, LORA_PAD), 1)
    act = jnp.where(lane < DECAY_LORA, jnp.tanh(z),
                    jnp.where(lane < DECAY_LORA + ICLR_LORA, z, jax.nn.sigmoid(z)))
    up = _dot(act.astype(BF16), wup_ref[...])
    zw = -(w0_ref[...] + up[:, :RWKV_W])
    softplus = jnp.maximum(zw, 0.0) + jnp.log(1.0 + jnp.exp(-jnp.abs(zw)))
    ld = -jnp.exp(-softplus - 0.5)
    iclr = jax.nn.sigmoid(a0_ref[...] + up[:, RWKV_W:2 * RWKV_W])
    gate = up[:, 2 * RWKV_W:]

    ones_bd = ones_ref[...]
    kk = p_k * kk_ref[...]
    kk = kk * lax.rsqrt(jnp.maximum(_dot((kk * kk).astype(BF16), ones_bd), 1e-24))
    k = p_k * (1.0 + (iclr - 1.0) * ka_ref[...])
    bonus = _dot((p_r * k * rk_ref[...]).astype(BF16), ones_bd) * p_v

    r_out[0] = p_r
    ld_out[0] = ld
    k_out[0] = k
    v_out[0] = p_v
    kk_out[0] = kk
    ic_out[0] = iclr
    bonus_out[0] = bonus
    gate_out[0] = gate


def _head_ones(width, head):
    i = jnp.arange(width) // head
    return (i[:, None] == i[None, :]).astype(BF16)


def _proj(x, g, w_in, shift_mu, w_up, a_up, g_up, w0, a0, k_k, k_a, r_k, *, tm):
    b, t, d = x.shape
    pad = SHIFT_PAD - SHIFT_COLS
    w = jnp.concatenate(
        [w_in[:, :SHIFT_COLS], jnp.zeros((d, pad), F32), w_in[:, SHIFT_COLS:]], axis=1).astype(BF16)
    mu = jnp.concatenate([shift_mu, jnp.zeros((pad,), F32)]).reshape(1, SHIFT_PAD)
    wup = jnp.zeros((LORA_PAD, 3 * RWKV_W), F32)
    wup = wup.at[:DECAY_LORA, :RWKV_W].set(w_up)
    wup = wup.at[DECAY_LORA:DECAY_LORA + ICLR_LORA, RWKV_W:2 * RWKV_W].set(a_up)
    wup = wup.at[DECAY_LORA + ICLR_LORA:LORA_W, 2 * RWKV_W:].set(g_up)
    wup = wup.astype(BF16)
    wcols = w.shape[1]
    vec = lambda a: a.reshape(1, RWKV_W)
    const = lambda shape: pl.BlockSpec(shape, lambda i, j: (0,) * len(shape))
    tile = lambda width: pl.BlockSpec((1, tm, width), lambda i, j: (i, j, 0))
    f32_out = jax.ShapeDtypeStruct((b, t, RWKV_W), F32)
    bf_out = jax.ShapeDtypeStruct((b, t, DIFF_W), BF16)
    return pl.pallas_call(
        _proj_body,
        grid=(b, t // tm),
        in_specs=[tile(d), const((1, d)), const((d, wcols)), const((1, SHIFT_PAD)),
                  const((LORA_PAD, 3 * RWKV_W)), const((RWKV_W, RWKV_W))] + [const((1, RWKV_W))] * 5,
        out_specs=[tile(RWKV_W)] * 8 + [tile(DIFF_W)] * 2
        + [pl.BlockSpec((1, DIFF_W, tm), lambda i, j: (i, 0, j))],
        out_shape=[f32_out] * 8 + [bf_out] * 2 + [jax.ShapeDtypeStruct((b, DIFF_W, t), BF16)],
        scratch_shapes=[pltpu.VMEM((1, SHIFT_PAD), F32)],
        compiler_params=pltpu.CompilerParams(
            dimension_semantics=("arbitrary", "arbitrary"), vmem_limit_bytes=VMEM_LIMIT),
        name="proj",
    )(x, g.reshape(1, d), w, mu, wup, _head_ones(RWKV_W, RWKV_HEAD),
      vec(w0), vec(a0), vec(k_k), vec(k_a), vec(r_k))


def _rwkv_body(r_ref, ld_ref, k_ref, v_ref, kk_ref, ic_ref, bonus_ref, gate_ref, gnw_ref, gnb_ref,
               o_ref, h_ref, *, chunk):
    c = chunk
    c2 = 2 * c
    nc = r_ref.shape[1] // c

    @pl.when(pl.program_id(1) == 0)
    def _():
        h_ref[...] = jnp.zeros_like(h_ref)

    ri = lax.broadcasted_iota(jnp.int32, (c, c), 0)
    ci = lax.broadcasted_iota(jnp.int32, (c, c), 1)
    tril = (ci <= ri).astype(F32)
    at, rt, bt, kt, bh, kh, v, g_tot = [], [], [], [], [], [], [], []
    for i in range(nc):
        rows = slice(i * c, (i + 1) * c)
        ld = ld_ref[0, rows, :]
        cum = jnp.dot(tril, ld, preferred_element_type=F32, precision=lax.Precision.HIGHEST)
        tot = cum[c - 1:c, :]
        g_inv = jnp.exp(-cum)
        g_rest = jnp.exp(tot - cum)
        kk = kk_ref[0, rows, :]
        b = kk * ic_ref[0, rows, :]
        k = k_ref[0, rows, :]
        at.append((-kk * jnp.exp(cum - ld)).astype(BF16))
        rt.append((r_ref[0, rows, :] * jnp.exp(cum)).astype(BF16))
        bt.append((b * g_inv).astype(BF16))
        kt.append((k * g_inv).astype(BF16))
        bh.append((b * g_rest).astype(BF16))
        kh.append((k * g_rest).astype(BF16))
        v.append(v_ref[0, rows, :].astype(BF16))
        g_tot.append(jnp.exp(tot))

    lane = lax.broadcasted_iota(jnp.int32, (1, LANES), 1)
    lo = lane < RWKV_HEAD

    def stack(x):
        zero = jnp.zeros_like(x)
        return jnp.concatenate([jnp.where(lo, x, zero), jnp.where(lo, zero, x)], axis=0)

    row2 = lax.broadcasted_iota(jnp.int32, (c2, c2), 0)
    col2 = lax.broadcasted_iota(jnp.int32, (c2, c2), 1)
    same = (row2 // c) == (col2 // c)
    strict = same & (col2 < row2)
    incl = same & (col2 <= row2)
    eye = (row2 == col2).astype(F32)
    levels = int(math.log2(c))
    level_masks = [((row2 >> l) == (col2 >> l)) & ((row2 >> (l - 1)) != (col2 >> (l - 1)))
                   & (col2 < row2) for l in range(1, levels + 1)]
    ones_bd = ((lax.broadcasted_iota(jnp.int32, (LANES, LANES), 0) // RWKV_HEAD)
               == (lax.broadcasted_iota(jnp.int32, (LANES, LANES), 1) // RWKV_HEAD)).astype(BF16)
    ones_bd2 = jnp.concatenate([ones_bd, ones_bd], axis=0)
    nt = lambda x, y: lax.dot_general(x, y, _NT, preferred_element_type=F32)
    tn = lambda x, y: lax.dot_general(x, y, _TN, preferred_element_type=F32)

    npairs = RWKV_HEADS // 2
    chains = [(i, p) for i in range(nc) for p in range(npairs)]
    sls = [slice(p * LANES, (p + 1) * LANES) for p in range(npairs)]
    each = lambda f: {ip: f(ip) for ip in chains}
    pick = lambda xs: each(lambda ip: stack(xs[ip[0]][:, sls[ip[1]]]))
    a2, r2, b2, k2, bh2, kh2, v2 = (pick(xs) for xs in (at, rt, bt, kt, bh, kh, v))

    cat0 = lambda *xs: jnp.concatenate(xs, axis=0)
    cat1 = lambda *xs: jnp.concatenate(xs, axis=1)
    s_ar = each(lambda ip: nt(cat0(a2[ip], r2[ip]), cat0(b2[ip], k2[ip])))
    a_ab = each(lambda ip: jnp.where(strict, s_ar[ip][:c2, :c2], 0.0))
    a_ak = each(lambda ip: jnp.where(strict, s_ar[ip][:c2, c2:], 0.0).astype(BF16))
    a_rb = each(lambda ip: jnp.where(incl, s_ar[ip][c2:, :c2], 0.0).astype(BF16))
    a_rk = each(lambda ip: jnp.where(incl, s_ar[ip][c2:, c2:], 0.0).astype(BF16))

    tinv = each(lambda ip: eye + jnp.where(level_masks[0], a_ab[ip], 0.0))
    for l in range(2, levels + 1):
        e = each(lambda ip: jnp.where(level_masks[l - 1], a_ab[ip], 0.0).astype(BF16))
        tb = each(lambda ip: tinv[ip].astype(BF16))
        te = each(lambda ip: _dot(tb[ip], e[ip]).astype(BF16))
        tinv = each(lambda ip: tinv[ip] + _dot(te[ip], tb[ip]))
    tb = each(lambda ip: tinv[ip].astype(BF16))

    av = each(lambda ip: _dot(cat0(a_ak[ip], a_rk[ip]), v2[ip]))
    akv = each(lambda ip: av[ip][:c2].astype(BF16))
    wu = each(lambda ip: _dot(tb[ip], cat1(a2[ip], akv[ip])).astype(BF16))
    pg = each(lambda ip: tn(bh2[ip], wu[ip]))
    gm = each(lambda ip: pg[ip][:, LANES:] + tn(kh2[ip], v2[ip]))
    qy = each(lambda ip: _dot(a_rb[ip], wu[ip]))
    qp = each(lambda ip: cat0((r2[ip].astype(F32) + qy[ip][:, :LANES]).astype(BF16),
                              pg[ip][:, :LANES].astype(BF16)))
    y0 = each(lambda ip: qy[ip][:, LANES:] + av[ip][c2:])
    assert c2 == LANES
    g_col = each(lambda ip: jnp.sum(eye * g_tot[ip[0]][:, sls[ip[1]]], axis=1, keepdims=True))

    state = [h_ref[p] for p in range(npairs)]
    ys = [[] for _ in range(npairs)]
    for i in range(nc):
        for p in range(npairs):
            qh = _dot(qp[i, p], state[p].astype(BF16))
            y2 = qh[:c2] + y0[i, p]
            ys[p].append(y2[:c] + y2[c:])
            state[p] = g_col[i, p] * state[p] + qh[c2:] + gm[i, p]
    for p in range(npairs):
        h_ref[p] = state[p]

    for p in range(npairs):
        sl = sls[p]
        y = cat0(*ys[p])
        mean = _split_dot_k2(y, ones_bd2) * (1.0 / RWKV_HEAD)
        yc = y - mean
        var = _split_dot_k2(yc * yc, ones_bd2) * (1.0 / RWKV_HEAD)
        yn = yc * lax.rsqrt(var + GN_EPS) * gnw_ref[:, sl] + gnb_ref[:, sl]
        o_ref[0, :, sl] = ((yn + bonus_ref[0, :, sl]) * gate_ref[0, :, sl]).astype(o_ref.dtype)


def _rwkv(r, ld, k, v, kk, iclr, bonus, gate, gn_w, gn_b, *, chunk, tb):
    b, t, w = r.shape
    tile = pl.BlockSpec((1, tb, w), lambda i, j: (i, j, 0))
    const = pl.BlockSpec((1, w), lambda i, j: (0, 0))
    return pl.pallas_call(
        functools.partial(_rwkv_body, chunk=chunk),
        grid=(b, t // tb),
        in_specs=[tile] * 8 + [const] * 2,
        out_specs=tile,
        out_shape=jax.ShapeDtypeStruct((b, t, w), BF16),
        scratch_shapes=[pltpu.VMEM((RWKV_HEADS // 2, LANES, LANES), F32)],
        compiler_params=pltpu.CompilerParams(
            dimension_semantics=("arbitrary", "arbitrary"), vmem_limit_bytes=VMEM_LIMIT),
        name="rwkv",
    )(r, ld, k, v, kk, iclr, bonus, gate, gn_w.reshape(1, w), gn_b.reshape(1, w))


def _attn_body(lq1_ref, lk1_ref, lq2_ref, lk2_ref, q_ref, k_ref, vt_ref, sw_ref, o_ref, acc_ref,
               s_ref, m_ref, l_ref, *, tq, tk, wq, lambda_init):
    qi = pl.program_id(2)
    q = q_ref[0]
    lane = lax.broadcasted_iota(jnp.int32, (1, LANES), 1)
    zero = jnp.zeros_like(q)
    qs = (jnp.where(lane < DIFF_QK, q, zero), jnp.where(lane < DIFF_QK, zero, q))
    acc_ref[...] = jnp.zeros_like(acc_ref)
    m_ref[...] = jnp.full_like(m_ref, NEG_BIG)
    l_ref[...] = jnp.zeros_like(l_ref)
    items = [(c, w) for c in range(2) for w in range(tq // wq)]
    assert tq == 2 * tk

    def scores_into(slot, j, q_from=0):
        ks = k_ref[0, pl.ds(pl.multiple_of(j * tk, tk), tk), :]
        for c in range(2):
            s_ref[slot, c, :, q_from:] = lax.dot_general(ks, qs[c][q_from:], _NT,
                                                         preferred_element_type=F32)

    def consume(slot, j, diag):
        start = pl.multiple_of(j * tk, tk)
        for c, w in items:
            nk = tk if diag is None else min(tk, max(0, (w + 1) * wq - diag * tk))
            if nk == 0:
                continue
            cols = slice(w * wq, (w + 1) * wq)
            s = s_ref[slot, c, 0:nk, cols]
            if diag is not None and diag * tk + nk - 1 > w * wq:
                key = lax.broadcasted_iota(jnp.int32, (nk, wq), 0) + diag * tk
                qry = lax.broadcasted_iota(jnp.int32, (nk, wq), 1) + w * wq
                s = jnp.where(key <= qry, s, NEG_BIG)
            m = m_ref[c, :, cols]
            m_new = jnp.maximum(m, jnp.max(s, axis=0, keepdims=True))
            alpha = jnp.exp2(m - m_new)
            pr = jnp.exp2(s - m_new)
            m_ref[c, :, cols] = m_new
            l_ref[c, :, cols] = alpha * l_ref[c, :, cols] + jnp.sum(pr, axis=0, keepdims=True)
            vt = vt_ref[0, :, pl.ds(start, nk)]
            acc_ref[c, :, cols] = alpha * acc_ref[c, :, cols] + _dot(vt, pr.astype(BF16))

    def body(jj, carry):
        j = 2 * jj
        scores_into(1, j + 1)
        consume(0, j, None)
        scores_into(0, j + 2)
        consume(1, j + 1, None)
        return carry

    scores_into(0, 0)
    lax.fori_loop(0, qi, body, 0)
    scores_into(1, 2 * qi + 1, q_from=tk)
    consume(0, 2 * qi, 0)
    consume(1, 2 * qi + 1, 1)

    l1, l2 = l_ref[0], l_ref[1]

    lam = (jnp.exp(jnp.sum(lq1_ref[...] * lk1_ref[...], axis=-1, keepdims=True))
           - jnp.exp(jnp.sum(lq2_ref[...] * lk2_ref[...], axis=-1, keepdims=True)) + lambda_init)
    o = acc_ref[0] * (1.0 / l1) - lam * (acc_ref[1] * (1.0 / l2))
    o = o * lax.rsqrt(jnp.mean(o * o, axis=0, keepdims=True) + SUBLN_EPS) * sw_ref[...]
    o = o * (1.0 - lambda_init)
    o_ref[0] = o.T.astype(o_ref.dtype)


def _attn(q, k, vt, lq1, lk1, lq2, lk2, subln_w, *, tq, wq, lambda_init):
    tk = tq // 2
    b, t, w = q.shape
    heads = w // DIFF_V
    lam_spec = pl.BlockSpec((1, DIFF_QK), lambda i, h, j: (0, 0))
    return pl.pallas_call(
        functools.partial(_attn_body, tq=tq, tk=tk, wq=wq, lambda_init=lambda_init),
        grid=(b, heads, t // tq),
        in_specs=[lam_spec] * 4 + [
            pl.BlockSpec((1, tq, DIFF_V), lambda i, h, j: (i, j, h)),
            pl.BlockSpec((1, t, DIFF_V), lambda i, h, j: (i, 0, h)),
            pl.BlockSpec((1, DIFF_V, t), lambda i, h, j: (i, h, 0)),
            pl.BlockSpec((DIFF_V, 1), lambda i, h, j: (0, 0)),
        ],
        out_specs=pl.BlockSpec((1, tq, DIFF_V), lambda i, h, j: (i, j, h)),
        out_shape=jax.ShapeDtypeStruct((b, t, w), BF16),
        scratch_shapes=[pltpu.VMEM((2, DIFF_V, tq), F32), pltpu.VMEM((2, 2, tk, tq), F32),
                        pltpu.VMEM((2, 1, tq), F32), pltpu.VMEM((2, 1, tq), F32)],
        compiler_params=pltpu.CompilerParams(
            dimension_semantics=("parallel", "parallel", "arbitrary"), vmem_limit_bytes=VMEM_LIMIT),
        name="attn",
    )(lq1.reshape(1, -1), lk1.reshape(1, -1), lq2.reshape(1, -1), lk2.reshape(1, -1),
      q, k, vt, subln_w.reshape(-1, 1))


def _outproj_body(x_ref, yr_ref, yd_ref, wr_ref, wd_ref, g_ref, o_ref):
    y = _dot(yr_ref[...], wr_ref[...]) + _dot(yd_ref[...], wd_ref[...])
    o_ref[...] = x_ref[...] + _rms(y, g_ref[...], NORM_EPS)


def _outproj(x, y_rwkv, y_diff, w_o, g, *, tm):
    m, d = x.shape
    w_o = w_o.astype(BF16)
    const = lambda shape: pl.BlockSpec(shape, lambda i: (0, 0))
    return pl.pallas_call(
        _outproj_body,
        grid=(m // tm,),
        in_specs=[pl.BlockSpec((tm, d), lambda i: (i, 0)),
                  pl.BlockSpec((tm, RWKV_W), lambda i: (i, 0)),
                  pl.BlockSpec((tm, DIFF_W), lambda i: (i, 0)),
                  const((RWKV_W, d)), const((DIFF_W, d)), const((1, d))],
        out_specs=pl.BlockSpec((tm, d), lambda i: (i, 0)),
        out_shape=jax.ShapeDtypeStruct((m, d), F32),
        compiler_params=pltpu.CompilerParams(
            dimension_semantics=("parallel",), vmem_limit_bytes=VMEM_LIMIT),
        name="outproj",
    )(x, y_rwkv, y_diff, w_o[:RWKV_W], w_o[RWKV_W:], g.reshape(1, d))


def _pick(n, pref):
    return pref if n % pref == 0 else n


def _layer(x, l, p, *, chunk=64, tb_rwkv=256, tm_ffn=1024, tf=256, tm_proj=256, tq=1024,
           wq_attn=256, tm_out=1024):
    b, t, d = x.shape
    m = b * t
    lambda_init = 0.8 - 0.6 * math.exp(-0.3 * l)
    tm_ffn = _pick(m, tm_ffn)
    tf = _pick(p["ffn1_w_gate"].shape[-1], tf)
    x = _ffn(x.reshape(m, d), p["ffn1_pre_g"][l], p["ffn1_post_g"][l], p["ffn1_w_gate"][l],
             p["ffn1_w_up"][l], p["ffn1_w_down"][l], tm=tm_ffn, tf=tf)
    outs = _proj(x.reshape(b, t, d), p["mix_pre_g"][l], p["w_in"][l], p["shift_mu"][l],
                 p["rwkv_w_up"][l], p["rwkv_a_up"][l], p["rwkv_g_up"][l], p["rwkv_w0"][l],
                 p["rwkv_a0"][l], p["rwkv_k_k"][l], p["rwkv_k_a"][l], p["rwkv_r_k"][l].reshape(-1),
                 tm=_pick(t, tm_proj))
    y_rwkv = _rwkv(*outs[:8], p["rwkv_gn_w"][l], p["rwkv_gn_b"][l], chunk=chunk,
                   tb=_pick(t, tb_rwkv))
    y_diff = _attn(*outs[8:], p["diff_lam_q1"][l], p["diff_lam_k1"][l], p["diff_lam_q2"][l],
                   p["diff_lam_k2"][l], p["diff_subln_w"][l], tq=_pick(t, tq), wq=wq_attn,
                   lambda_init=lambda_init)
    x = _outproj(x, y_rwkv.reshape(m, -1), y_diff.reshape(m, -1), p["w_o"][l], p["mix_post_g"][l],
                 tm=_pick(m, tm_out))
    x = _ffn(x, p["ffn2_pre_g"][l], p["ffn2_post_g"][l], p["ffn2_w_gate"][l], p["ffn2_w_up"][l],
             p["ffn2_w_down"][l], tm=tm_ffn, tf=tf)
    return x.reshape(b, t, d)


def kernel(x, ffn1_pre_g, ffn1_post_g, ffn1_w_gate, ffn1_w_up, ffn1_w_down, mix_pre_g, mix_post_g,
           w_in, shift_mu, w_o, rwkv_w0, rwkv_w_up, rwkv_a0, rwkv_a_up, rwkv_g_up, rwkv_k_k,
           rwkv_k_a, rwkv_r_k, rwkv_gn_w, rwkv_gn_b, diff_lam_q1, diff_lam_k1, diff_lam_q2,
           diff_lam_k2, diff_subln_w, ffn2_pre_g, ffn2_post_g, ffn2_w_gate, ffn2_w_up, ffn2_w_down):
    p = dict(ffn1_pre_g=ffn1_pre_g, ffn1_post_g=ffn1_post_g, ffn1_w_gate=ffn1_w_gate,
             ffn1_w_up=ffn1_w_up, ffn1_w_down=ffn1_w_down, mix_pre_g=mix_pre_g,
             mix_post_g=mix_post_g, w_in=w_in, shift_mu=shift_mu, w_o=w_o, rwkv_w0=rwkv_w0,
             rwkv_w_up=rwkv_w_up, rwkv_a0=rwkv_a0, rwkv_a_up=rwkv_a_up, rwkv_g_up=rwkv_g_up,
             rwkv_k_k=rwkv_k_k, rwkv_k_a=rwkv_k_a, rwkv_r_k=rwkv_r_k, rwkv_gn_w=rwkv_gn_w,
             rwkv_gn_b=rwkv_gn_b, diff_lam_q1=diff_lam_q1, diff_lam_k1=diff_lam_k1,
             diff_lam_q2=diff_lam_q2, diff_lam_k2=diff_lam_k2, diff_subln_w=diff_subln_w,
             ffn2_pre_g=ffn2_pre_g, ffn2_post_g=ffn2_post_g, ffn2_w_gate=ffn2_w_gate,
             ffn2_w_up=ffn2_w_up, ffn2_w_down=ffn2_w_down)
    for l in range(ffn1_pre_g.shape[0]):
        x = _layer(x, l, p)
    return x
```

```python
import functools
import math

import jax
import jax.numpy as jnp
from jax import lax
from jax.experimental import pallas as pl
from jax.experimental.pallas import tpu as pltpu

F32 = jnp.float32
BF16 = jnp.bfloat16

RWKV_HEADS = 8
RWKV_HEAD = 64
RWKV_W = RWKV_HEADS * RWKV_HEAD
DIFF_HEADS = 4
DIFF_QK = 64
DIFF_V = 2 * DIFF_QK
DIFF_W = DIFF_HEADS * DIFF_V
DECAY_LORA = 64
ICLR_LORA = 64
GATE_LORA = 160
LORA_W = DECAY_LORA + ICLR_LORA + GATE_LORA
LANES = 128
LORA_PAD = -(-LORA_W // LANES) * LANES
SHIFT_COLS = 3 * RWKV_W + LORA_W
SHIFT_PAD = 3 * RWKV_W + LORA_PAD
FFN_RES = 0.5
NORM_EPS = 1e-6
GN_EPS = 64e-5
SUBLN_EPS = 1e-5
NEG_BIG = -1e30
LOG2E = math.log2(math.e)
VMEM_LIMIT = 56 * 1024 * 1024

_NT = (((1,), (1,)), ((), ()))
_TN = (((0,), (0,)), ((), ()))


def _dot(a, b):
    return jnp.dot(a, b, preferred_element_type=F32)


def _split_dot_k2(x, w2_bf16):
    hi = x.astype(BF16)
    lo = (x - hi.astype(F32)).astype(BF16)
    return _dot(jnp.concatenate([hi, lo], axis=1), w2_bf16)


def _rms(x, g, eps):
    return x * lax.rsqrt(jnp.mean(x * x, axis=-1, keepdims=True) + eps) * g


def _ffn_body(*refs, tf, mix):
    if mix:
        x_ref, yr_ref, yd_ref, wo_ref, gmix_ref, *refs = refs
    else:
        x_ref, *refs = refs
    gpre_ref, gpost_ref, wg_ref, wu_ref, wd_ref, o_ref, a_ref = refs
    x = x_ref[...]
    if mix:
        nr = yr_ref.shape[1]
        y = _dot(yr_ref[...], wo_ref[:nr]) + _dot(yd_ref[...], wo_ref[nr:])
        x = x + _rms(y, gmix_ref[...], NORM_EPS)
    h = _rms(x, gpre_ref[...], NORM_EPS).astype(BF16)
    for f0 in range(0, wg_ref.shape[1], tf):
        cols = slice(f0, f0 + tf)
        g = _dot(h, wg_ref[:, cols])
        u = _dot(h, wu_ref[:, cols])
        a_ref[:, cols] = (g * jax.nn.sigmoid(g) * u).astype(BF16)
    y = _dot(a_ref[...], wd_ref[...])
    o_ref[...] = x + FFN_RES * _rms(y, gpost_ref[...], NORM_EPS)


def _ffn(x, g_pre, g_post, w_gate, w_up, w_down, *, tm, tf, mix=None):
    m, d = x.shape
    f = w_gate.shape[1]
    rows = lambda width: pl.BlockSpec((tm, width), lambda i: (i, 0))
    resident = lambda shape: pl.BlockSpec(shape, lambda i: (0, 0), pipeline_mode=pl.Buffered(1))
    args, specs = [x], [rows(d)]
    if mix is not None:
        y_rwkv, y_diff, w_o, g_mix = mix
        args += [y_rwkv, y_diff, w_o.astype(BF16), g_mix.reshape(1, d)]
        specs += [rows(y_rwkv.shape[1]), rows(y_diff.shape[1]), resident(w_o.shape), resident((1, d))]
    args += [g_pre.reshape(1, d), g_post.reshape(1, d), w_gate.astype(BF16), w_up.astype(BF16),
             w_down.astype(BF16)]
    specs += [resident((1, d)), resident((1, d)), resident((d, f)), resident((d, f)),
              resident((f, d))]
    return pl.pallas_call(
        functools.partial(_ffn_body, tf=tf, mix=mix is not None),
        grid=(m // tm,),
        in_specs=specs,
        out_specs=rows(d),
        out_shape=jax.ShapeDtypeStruct((m, d), F32),
        scratch_shapes=[pltpu.VMEM((tm, f), BF16)],
        compiler_params=pltpu.CompilerParams(
            dimension_semantics=("parallel",), vmem_limit_bytes=VMEM_LIMIT),
        name="ffn_mix" if mix is not None else "ffn",
    )(*args)


def _proj_body(x_ref, g_ref, w_ref, mu_ref, wup_ref, ones_ref, w0_ref, a0_ref, kk_ref, ka_ref,
               rk_ref,
               r_out, ld_out, k_out, v_out, kk_out, ic_out, bonus_out, gate_out,
               qd_out, kd_out, vd_out, carry_ref):
    t = pl.program_id(1)

    @pl.when(t == 0)
    def _():
        carry_ref[...] = jnp.zeros_like(carry_ref)

    h = _rms(x_ref[0], g_ref[...], NORM_EPS).astype(BF16)
    p = _dot(h, w_ref[...])
    tm = p.shape[0]

    qd_out[0] = (p[:, SHIFT_PAD:SHIFT_PAD + DIFF_W] * (DIFF_QK ** -0.5 * LOG2E)).astype(BF16)
    kd_out[0] = p[:, SHIFT_PAD + DIFF_W:SHIFT_PAD + 2 * DIFF_W].astype(BF16)
    vd_out[0] = p[:, SHIFT_PAD + 2 * DIFF_W:].T.astype(BF16)

    ps = p[:, :SHIFT_PAD]
    row = lax.broadcasted_iota(jnp.int32, (tm, 1), 0)
    prev = jnp.where(row == 0, carry_ref[...], pltpu.roll(ps, 1, 0))
    carry_ref[...] = ps[tm - 1:tm, :]
    ps = ps + (prev - ps) * mu_ref[...]

    p_r = ps[:, :RWKV_W]
    p_k = ps[:, RWKV_W:2 * RWKV_W]
    p_v = ps[:, 2 * RWKV_W:3 * RWKV_W]
    z = ps[:, 3 * RWKV_W:]
    lane = lax.broadcasted_iota(jnp.int32, (1, LORA_PAD), 1)
    act = jnp.where(lane < DECAY_LORA, jnp.tanh(z),
                    jnp.where(lane < DECAY_LORA + ICLR_LORA, z, jax.nn.sigmoid(z)))
    up = _dot(act.astype(BF16), wup_ref[...])
    zw = -(w0_ref[...] + up[:, :RWKV_W])
    softplus = jnp.maximum(zw, 0.0) + jnp.log(1.0 + jnp.exp(-jnp.abs(zw)))
    ld = -jnp.exp(-softplus - 0.5)
    iclr = jax.nn.sigmoid(a0_ref[...] + up[:, RWKV_W:2 * RWKV_W])
    gate = up[:, 2 * RWKV_W:]

    ones_bd = ones_ref[...]
    kk = p_k * kk_ref[...]
    kk = kk * lax.rsqrt(jnp.maximum(_dot((kk * kk).astype(BF16), ones_bd), 1e-24))
    k = p_k * (1.0 + (iclr - 1.0) * ka_ref[...])
    bonus = _dot((p_r * k * rk_ref[...]).astype(BF16), ones_bd) * p_v

    r_out[0] = p_r
    ld_out[0] = ld
    k_out[0] = k
    v_out[0] = p_v
    kk_out[0] = kk
    ic_out[0] = iclr
    bonus_out[0] = bonus
    gate_out[0] = gate


def _head_ones(width, head):
    i = jnp.arange(width) // head
    return (i[:, None] == i[None, :]).astype(BF16)


def _proj(x, g, w_in, shift_mu, w_up, a_up, g_up, w0, a0, k_k, k_a, r_k, *, tm):
    b, t, d = x.shape
    pad = SHIFT_PAD - SHIFT_COLS
    w = jnp.concatenate(
        [w_in[:, :SHIFT_COLS], jnp.zeros((d, pad), F32), w_in[:, SHIFT_COLS:]], axis=1).astype(BF16)
    mu = jnp.concatenate([shift_mu, jnp.zeros((pad,), F32)]).reshape(1, SHIFT_PAD)
    wup = jnp.zeros((LORA_PAD, 3 * RWKV_W), F32)
    wup = wup.at[:DECAY_LORA, :RWKV_W].set(w_up)
    wup = wup.at[DECAY_LORA:DECAY_LORA + ICLR_LORA, RWKV_W:2 * RWKV_W].set(a_up)
    wup = wup.at[DECAY_LORA + ICLR_LORA:LORA_W, 2 * RWKV_W:].set(g_up)
    wup = wup.astype(BF16)
    wcols = w.shape[1]
    vec = lambda a: a.reshape(1, RWKV_W)
    const = lambda shape: pl.BlockSpec(shape, lambda i, j: (0,) * len(shape))
    tile = lambda width: pl.BlockSpec((1, tm, width), lambda i, j: (i, j, 0))
    f32_out = jax.ShapeDtypeStruct((b, t, RWKV_W), F32)
    bf_out = jax.ShapeDtypeStruct((b, t, DIFF_W), BF16)
    return pl.pallas_call(
        _proj_body,
        grid=(b, t // tm),
        in_specs=[tile(d), const((1, d)), const((d, wcols)), const((1, SHIFT_PAD)),
                  const((LORA_PAD, 3 * RWKV_W)), const((RWKV_W, RWKV_W))] + [const((1, RWKV_W))] * 5,
        out_specs=[tile(RWKV_W)] * 8 + [tile(DIFF_W)] * 2
        + [pl.BlockSpec((1, DIFF_W, tm), lambda i, j: (i, 0, j))],
        out_shape=[f32_out] * 8 + [bf_out] * 2 + [jax.ShapeDtypeStruct((b, DIFF_W, t), BF16)],
        scratch_shapes=[pltpu.VMEM((1, SHIFT_PAD), F32)],
        compiler_params=pltpu.CompilerParams(
            dimension_semantics=("arbitrary", "arbitrary"), vmem_limit_bytes=VMEM_LIMIT),
        name="proj",
    )(x, g.reshape(1, d), w, mu, wup, _head_ones(RWKV_W, RWKV_HEAD),
      vec(w0), vec(a0), vec(k_k), vec(k_a), vec(r_k))


def _rwkv_body(r_ref, ld_ref, k_ref, v_ref, kk_ref, ic_ref, bonus_ref, gate_ref, gnw_ref, gnb_ref,
               o_ref, h_ref, *, chunk):
    c = chunk
    c2 = 2 * c
    nc = r_ref.shape[1] // c

    @pl.when(pl.program_id(1) == 0)
    def _():
        h_ref[...] = jnp.zeros_like(h_ref)

    ri = lax.broadcasted_iota(jnp.int32, (c, c), 0)
    ci = lax.broadcasted_iota(jnp.int32, (c, c), 1)
    tril = (ci <= ri).astype(F32)
    at, rt, bt, kt, bh, kh, v, g_tot = [], [], [], [], [], [], [], []
    for i in range(nc):
        rows = slice(i * c, (i + 1) * c)
        ld = ld_ref[0, rows, :]
        cum = jnp.dot(tril, ld, preferred_element_type=F32, precision=lax.Precision.HIGHEST)
        tot = cum[c - 1:c, :]
        g_inv = jnp.exp(-cum)
        g_rest = jnp.exp(tot - cum)
        kk = kk_ref[0, rows, :]
        b = kk * ic_ref[0, rows, :]
        k = k_ref[0, rows, :]
        at.append((-kk * jnp.exp(cum - ld)).astype(BF16))
        rt.append((r_ref[0, rows, :] * jnp.exp(cum)).astype(BF16))
        bt.append((b * g_inv).astype(BF16))
        kt.append((k * g_inv).astype(BF16))
        bh.append((b * g_rest).astype(BF16))
        kh.append((k * g_rest).astype(BF16))
        v.append(v_ref[0, rows, :].astype(BF16))
        g_tot.append(jnp.exp(tot))

    lane = lax.broadcasted_iota(jnp.int32, (1, LANES), 1)
    lo = lane < RWKV_HEAD

    def stack(x):
        zero = jnp.zeros_like(x)
        return jnp.concatenate([jnp.where(lo, x, zero), jnp.where(lo, zero, x)], axis=0)

    row2 = lax.broadcasted_iota(jnp.int32, (c2, c2), 0)
    col2 = lax.broadcasted_iota(jnp.int32, (c2, c2), 1)
    same = (row2 // c) == (col2 // c)
    strict = same & (col2 < row2)
    incl = same & (col2 <= row2)
    eye = (row2 == col2).astype(F32)
    levels = int(math.log2(c))
    level_masks = [((row2 >> l) == (col2 >> l)) & ((row2 >> (l - 1)) != (col2 >> (l - 1)))
                   & (col2 < row2) for l in range(1, levels + 1)]
    ones_bd = ((lax.broadcasted_iota(jnp.int32, (LANES, LANES), 0) // RWKV_HEAD)
               == (lax.broadcasted_iota(jnp.int32, (LANES, LANES), 1) // RWKV_HEAD)).astype(BF16)
    ones_bd2 = jnp.concatenate([ones_bd, ones_bd], axis=0)
    nt = lambda x, y: lax.dot_general(x, y, _NT, preferred_element_type=F32)
    tn = lambda x, y: lax.dot_general(x, y, _TN, preferred_element_type=F32)

    npairs = RWKV_HEADS // 2
    chains = [(i, p) for i in range(nc) for p in range(npairs)]
    sls = [slice(p * LANES, (p + 1) * LANES) for p in range(npairs)]
    each = lambda f: {ip: f(ip) for ip in chains}
    pick = lambda xs: each(lambda ip: stack(xs[ip[0]][:, sls[ip[1]]]))
    a2, r2, b2, k2, bh2, kh2, v2 = (pick(xs) for xs in (at, rt, bt, kt, bh, kh, v))

    cat0 = lambda *xs: jnp.concatenate(xs, axis=0)
    cat1 = lambda *xs: jnp.concatenate(xs, axis=1)
    s_ar = each(lambda ip: nt(cat0(a2[ip], r2[ip]), cat0(b2[ip], k2[ip])))
    a_ab = each(lambda ip: jnp.where(strict, s_ar[ip][:c2, :c2], 0.0))
    a_ak = each(lambda ip: jnp.where(strict, s_ar[ip][:c2, c2:], 0.0).astype(BF16))
    a_rb = each(lambda ip: jnp.where(incl, s_ar[ip][c2:, :c2], 0.0).astype(BF16))
    a_rk = each(lambda ip: jnp.where(incl, s_ar[ip][c2:, c2:], 0.0).astype(BF16))

    tinv = each(lambda ip: eye + jnp.where(level_masks[0], a_ab[ip], 0.0))
    for l in range(2, levels + 1):
        e = each(lambda ip: jnp.where(level_masks[l - 1], a_ab[ip], 0.0).astype(BF16))
        tb = each(lambda ip: tinv[ip].astype(BF16))
        te = each(lambda ip: _dot(tb[ip], e[ip]).astype(BF16))
        tinv = each(lambda ip: tinv[ip] + _dot(te[ip], tb[ip]))
    tb = each(lambda ip: tinv[ip].astype(BF16))

    av = each(lambda ip: _dot(cat0(a_ak[ip], a_rk[ip]), v2[ip]))
    akv = each(lambda ip: av[ip][:c2].astype(BF16))
    wu = each(lambda ip: _dot(tb[ip], cat1(a2[ip], akv[ip])).astype(BF16))
    pg = each(lambda ip: tn(bh2[ip], wu[ip]))
    gm = each(lambda ip: pg[ip][:, LANES:] + tn(kh2[ip], v2[ip]))
    qy = each(lambda ip: _dot(a_rb[ip], wu[ip]))
    qp = each(lambda ip: cat0((r2[ip].astype(F32) + qy[ip][:, :LANES]).astype(BF16),
                              pg[ip][:, :LANES].astype(BF16)))
    y0 = each(lambda ip: qy[ip][:, LANES:] + av[ip][c2:])
    assert c2 == LANES
    g_col = each(lambda ip: jnp.sum(eye * g_tot[ip[0]][:, sls[ip[1]]], axis=1, keepdims=True))

    state = [h_ref[p] for p in range(npairs)]
    ys = [[] for _ in range(npairs)]
    for i in range(nc):
        for p in range(npairs):
            qh = _dot(qp[i, p], state[p].astype(BF16))
            y2 = qh[:c2] + y0[i, p]
            ys[p].append(y2[:c] + y2[c:])
            state[p] = g_col[i, p] * state[p] + qh[c2:] + gm[i, p]
    for p in range(npairs):
        h_ref[p] = state[p]

    for p in range(npairs):
        sl = sls[p]
        y = cat0(*ys[p])
        mean = _split_dot_k2(y, ones_bd2) * (1.0 / RWKV_HEAD)
        yc = y - mean
        var = _split_dot_k2(yc * yc, ones_bd2) * (1.0 / RWKV_HEAD)
        yn = yc * lax.rsqrt(var + GN_EPS) * gnw_ref[:, sl] + gnb_ref[:, sl]
        o_ref[0, :, sl] = ((yn + bonus_ref[0, :, sl]) * gate_ref[0, :, sl]).astype(o_ref.dtype)


def _rwkv(r, ld, k, v, kk, iclr, bonus, gate, gn_w, gn_b, *, chunk, tb):
    b, t, w = r.shape
    tile = pl.BlockSpec((1, tb, w), lambda i, j: (i, j, 0))
    const = pl.BlockSpec((1, w), lambda i, j: (0, 0))
    return pl.pallas_call(
        functools.partial(_rwkv_body, chunk=chunk),
        grid=(b, t // tb),
        in_specs=[tile] * 8 + [const] * 2,
        out_specs=tile,
        out_shape=jax.ShapeDtypeStruct((b, t, w), BF16),
        scratch_shapes=[pltpu.VMEM((RWKV_HEADS // 2, LANES, LANES), F32)],
        compiler_params=pltpu.CompilerParams(
            dimension_semantics=("arbitrary", "arbitrary"), vmem_limit_bytes=VMEM_LIMIT),
        name="rwkv",
    )(r, ld, k, v, kk, iclr, bonus, gate, gn_w.reshape(1, w), gn_b.reshape(1, w))


def _attn_body(lq1_ref, lk1_ref, lq2_ref, lk2_ref, q_ref, k_ref, vt_ref, sw_ref, o_ref, acc_ref,
               s_ref, m_ref, l_ref, *, tq, tk, wq, lambda_init):
    qi = pl.program_id(2)
    q = q_ref[0]
    lane = lax.broadcasted_iota(jnp.int32, (1, LANES), 1)
    zero = jnp.zeros_like(q)
    qs = (jnp.where(lane < DIFF_QK, q, zero), jnp.where(lane < DIFF_QK, zero, q))
    acc_ref[...] = jnp.zeros_like(acc_ref)
    m_ref[...] = jnp.full_like(m_ref, NEG_BIG)
    l_ref[...] = jnp.zeros_like(l_ref)
    items = [(c, w) for c in range(2) for w in range(tq // wq)]
    assert tq == 2 * tk

    def scores_into(slot, j, q_from=0):
        ks = k_ref[0, pl.ds(pl.multiple_of(j * tk, tk), tk), :]
        for c in range(2):
            s_ref[slot, c, :, q_from:] = lax.dot_general(ks, qs[c][q_from:], _NT,
                                                         preferred_element_type=F32)

    def consume(slot, j, diag):
        start = pl.multiple_of(j * tk, tk)
        for c, w in items:
            nk = tk if diag is None else min(tk, max(0, (w + 1) * wq - diag * tk))
            if nk == 0:
                continue
            cols = slice(w * wq, (w + 1) * wq)
            s = s_ref[slot, c, 0:nk, cols]
            if diag is not None and diag * tk + nk - 1 > w * wq:
                key = lax.broadcasted_iota(jnp.int32, (nk, wq), 0) + diag * tk
                qry = lax.broadcasted_iota(jnp.int32, (nk, wq), 1) + w * wq
                s = jnp.where(key <= qry, s, NEG_BIG)
            m = m_ref[c, :, cols]
            m_new = jnp.maximum(m, jnp.max(s, axis=0, keepdims=True))
            alpha = jnp.exp2(m - m_new)
            pr = jnp.exp2(s - m_new)
            m_ref[c, :, cols] = m_new
            l_ref[c, :, cols] = alpha * l_ref[c, :, cols] + jnp.sum(pr, axis=0, keepdims=True)
            vt = vt_ref[0, :, pl.ds(start, nk)]
            acc_ref[c, :, cols] = alpha * acc_ref[c, :, cols] + _dot(vt, pr.astype(BF16))

    def body(jj, carry):
        j = 2 * jj
        scores_into(1, j + 1)
        consume(0, j, None)
        scores_into(0, j + 2)
        consume(1, j + 1, None)
        return carry

    scores_into(0, 0)
    lax.fori_loop(0, qi, body, 0)
    scores_into(1, 2 * qi + 1, q_from=tk)
    consume(0, 2 * qi, 0)
    consume(1, 2 * qi + 1, 1)

    l1, l2 = l_ref[0], l_ref[1]

    lam = (jnp.exp(jnp.sum(lq1_ref[...] * lk1_ref[...], axis=-1, keepdims=True))
           - jnp.exp(jnp.sum(lq2_ref[...] * lk2_ref[...], axis=-1, keepdims=True)) + lambda_init)
    o = acc_ref[0] * (1.0 / l1) - lam * (acc_ref[1] * (1.0 / l2))
    o = o * lax.rsqrt(jnp.mean(o * o, axis=0, keepdims=True) + SUBLN_EPS) * sw_ref[...]
    o = o * (1.0 - lambda_init)
    o_ref[0] = o.T.astype(o_ref.dtype)


def _attn(q, k, vt, lq1, lk1, lq2, lk2, subln_w, *, tq, wq, lambda_init):
    tk = tq // 2
    b, t, w = q.shape
    heads = w // DIFF_V
    lam_spec = pl.BlockSpec((1, DIFF_QK), lambda i, h, j: (0, 0))
    return pl.pallas_call(
        functools.partial(_attn_body, tq=tq, tk=tk, wq=wq, lambda_init=lambda_init),
        grid=(b, heads, t // tq),
        in_specs=[lam_spec] * 4 + [
            pl.BlockSpec((1, tq, DIFF_V), lambda i, h, j: (i, j, h)),
            pl.BlockSpec((1, t, DIFF_V), lambda i, h, j: (i, 0, h)),
            pl.BlockSpec((1, DIFF_V, t), lambda i, h, j: (i, h, 0)),
            pl.BlockSpec((DIFF_V, 1), lambda i, h, j: (0, 0)),
        ],
        out_specs=pl.BlockSpec((1, tq, DIFF_V), lambda i, h, j: (i, j, h)),
        out_shape=jax.ShapeDtypeStruct((b, t, w), BF16),
        scratch_shapes=[pltpu.VMEM((2, DIFF_V, tq), F32), pltpu.VMEM((2, 2, tk, tq), F32),
                        pltpu.VMEM((2, 1, tq), F32), pltpu.VMEM((2, 1, tq), F32)],
        compiler_params=pltpu.CompilerParams(
            dimension_semantics=("parallel", "parallel", "arbitrary"), vmem_limit_bytes=VMEM_LIMIT),
        name="attn",
    )(lq1.reshape(1, -1), lk1.reshape(1, -1), lq2.reshape(1, -1), lk2.reshape(1, -1),
      q, k, vt, subln_w.reshape(-1, 1))


def _pick(n, pref):
    return pref if n % pref == 0 else n


def _layer(x, l, p, *, chunk=64, tb_rwkv=256, tm_ffn=1024, tf=256, tm_proj=256, tq=1024,
           wq_attn=256):
    b, t, d = x.shape
    m = b * t
    lambda_init = 0.8 - 0.6 * math.exp(-0.3 * l)
    tm_ffn = _pick(m, tm_ffn)
    tf = _pick(p["ffn1_w_gate"].shape[-1], tf)
    x = _ffn(x.reshape(m, d), p["ffn1_pre_g"][l], p["ffn1_post_g"][l], p["ffn1_w_gate"][l],
             p["ffn1_w_up"][l], p["ffn1_w_down"][l], tm=tm_ffn, tf=tf)
    outs = _proj(x.reshape(b, t, d), p["mix_pre_g"][l], p["w_in"][l], p["shift_mu"][l],
                 p["rwkv_w_up"][l], p["rwkv_a_up"][l], p["rwkv_g_up"][l], p["rwkv_w0"][l],
                 p["rwkv_a0"][l], p["rwkv_k_k"][l], p["rwkv_k_a"][l], p["rwkv_r_k"][l].reshape(-1),
                 tm=_pick(t, tm_proj))
    y_rwkv = _rwkv(*outs[:8], p["rwkv_gn_w"][l], p["rwkv_gn_b"][l], chunk=chunk,
                   tb=_pick(t, tb_rwkv))
    y_diff = _attn(*outs[8:], p["diff_lam_q1"][l], p["diff_lam_k1"][l], p["diff_lam_q2"][l],
                   p["diff_lam_k2"][l], p["diff_subln_w"][l], tq=_pick(t, tq), wq=wq_attn,
                   lambda_init=lambda_init)
    x = _ffn(x, p["ffn2_pre_g"][l], p["ffn2_post_g"][l], p["ffn2_w_gate"][l], p["ffn2_w_up"][l],
             p["ffn2_w_down"][l], tm=tm_ffn, tf=tf,
             mix=(y_rwkv.reshape(m, -1), y_diff.reshape(m, -1), p["w_o"][l], p["mix_post_g"][l]))
    return x.reshape(b, t, d)


def kernel(x, ffn1_pre_g, ffn1_post_g, ffn1_w_gate, ffn1_w_up, ffn1_w_down, mix_pre_g, mix_post_g,
           w_in, shift_mu, w_o, rwkv_w0, rwkv_w_up, rwkv_a0, rwkv_a_up, rwkv_g_up, rwkv_k_k,
           rwkv_k_a, rwkv_r_k, rwkv_gn_w, rwkv_gn_b, diff_lam_q1, diff_lam_k1, diff_lam_q2,
           diff_lam_k2, diff_subln_w, ffn2_pre_g, ffn2_post_g, ffn2_w_gate, ffn2_w_up, ffn2_w_down):
    p = dict(ffn1_pre_g=ffn1_pre_g, ffn1_post_g=ffn1_post_g, ffn1_w_gate=ffn1_w_gate,
             ffn1_w_up=ffn1_w_up, ffn1_w_down=ffn1_w_down, mix_pre_g=mix_pre_g,
             mix_post_g=mix_post_g, w_in=w_in, shift_mu=shift_mu, w_o=w_o, rwkv_w0=rwkv_w0,
             rwkv_w_up=rwkv_w_up, rwkv_a0=rwkv_a0, rwkv_a_up=rwkv_a_up, rwkv_g_up=rwkv_g_up,
             rwkv_k_k=rwkv_k_k, rwkv_k_a=rwkv_k_a, rwkv_r_k=rwkv_r_k, rwkv_gn_w=rwkv_gn_w,
             rwkv_gn_b=rwkv_gn_b, diff_lam_q1=diff_lam_q1, diff_lam_k1=diff_lam_k1,
             diff_lam_q2=diff_lam_q2, diff_lam_k2=diff_lam_k2, diff_subln_w=diff_subln_w,
             ffn2_pre_g=ffn2_pre_g, ffn2_post_g=ffn2_post_g, ffn2_w_gate=ffn2_w_gate,
             ffn2_w_up=ffn2_w_up, ffn2_w_down=ffn2_w_down)
    for l in range(ffn1_pre_g.shape[0]):
        x = _layer(x, l, p)
    return x
```

```python
import functools
import math

import jax
import jax.numpy as jnp
from jax import lax
from jax.experimental import pallas as pl
from jax.experimental.pallas import tpu as pltpu

F32 = jnp.float32
BF16 = jnp.bfloat16

RWKV_HEADS = 8
RWKV_HEAD = 64
RWKV_W = RWKV_HEADS * RWKV_HEAD
DIFF_HEADS = 4
DIFF_QK = 64
DIFF_V = 2 * DIFF_QK
DIFF_W = DIFF_HEADS * DIFF_V
DECAY_LORA = 64
ICLR_LORA = 64
GATE_LORA = 160
LORA_W = DECAY_LORA + ICLR_LORA + GATE_LORA
LANES = 128
LORA_PAD = -(-LORA_W // LANES) * LANES
SHIFT_COLS = 3 * RWKV_W + LORA_W
SHIFT_PAD = 3 * RWKV_W + LORA_PAD
FFN_RES = 0.5
NORM_EPS = 1e-6
GN_EPS = 64e-5
SUBLN_EPS = 1e-5
NEG_BIG = -1e30
LOG2E = math.log2(math.e)
VMEM_LIMIT = 56 * 1024 * 1024

_NT = (((1,), (1,)), ((), ()))
_TN = (((0,), (0,)), ((), ()))


def _dot(a, b):
    return jnp.dot(a, b, preferred_element_type=F32)


def _split_dot_k2(x, w2_bf16):
    hi = x.astype(BF16)
    lo = (x - hi.astype(F32)).astype(BF16)
    return _dot(jnp.concatenate([hi, lo], axis=1), w2_bf16)


def _rms(x, g, eps):
    return x * lax.rsqrt(jnp.mean(x * x, axis=-1, keepdims=True) + eps) * g


def _ffn_body(*refs, tf, mix):
    if mix:
        x_ref, yr_ref, yd_ref, wo_ref, gmix_ref, *refs = refs
    else:
        x_ref, *refs = refs
    gpre_ref, gpost_ref, wg_ref, wu_ref, wd_ref, o_ref, a_ref = refs
    x = x_ref[...]
    if mix:
        nr = yr_ref.shape[1]
        y = _dot(yr_ref[...], wo_ref[:nr]) + _dot(yd_ref[...], wo_ref[nr:])
        x = x + _rms(y, gmix_ref[...], NORM_EPS)
    h = _rms(x, gpre_ref[...], NORM_EPS).astype(BF16)
    for f0 in range(0, wg_ref.shape[1], tf):
        cols = slice(f0, f0 + tf)
        g = _dot(h, wg_ref[:, cols])
        u = _dot(h, wu_ref[:, cols])
        a_ref[:, cols] = (g * jax.nn.sigmoid(g) * u).astype(BF16)
    y = _dot(a_ref[...], wd_ref[...])
    o_ref[...] = x + FFN_RES * _rms(y, gpost_ref[...], NORM_EPS)


def _ffn(x, g_pre, g_post, w_gate, w_up, w_down, *, tm, tf, mix=None):
    m, d = x.shape
    f = w_gate.shape[1]
    rows = lambda width: pl.BlockSpec((tm, width), lambda i: (i, 0))
    resident = lambda shape: pl.BlockSpec(shape, lambda i: (0, 0), pipeline_mode=pl.Buffered(1))
    args, specs = [x], [rows(d)]
    if mix is not None:
        y_rwkv, y_diff, w_o, g_mix = mix
        args += [y_rwkv, y_diff, w_o.astype(BF16), g_mix.reshape(1, d)]
        specs += [rows(y_rwkv.shape[1]), rows(y_diff.shape[1]), resident(w_o.shape), resident((1, d))]
    args += [g_pre.reshape(1, d), g_post.reshape(1, d), w_gate.astype(BF16), w_up.astype(BF16),
             w_down.astype(BF16)]
    specs += [resident((1, d)), resident((1, d)), resident((d, f)), resident((d, f)),
              resident((f, d))]
    return pl.pallas_call(
        functools.partial(_ffn_body, tf=tf, mix=mix is not None),
        grid=(m // tm,),
        in_specs=specs,
        out_specs=rows(d),
        out_shape=jax.ShapeDtypeStruct((m, d), F32),
        scratch_shapes=[pltpu.VMEM((tm, f), BF16)],
        compiler_params=pltpu.CompilerParams(
            dimension_semantics=("parallel",), vmem_limit_bytes=VMEM_LIMIT),
        name="ffn_mix" if mix is not None else "ffn",
    )(*args)


def _proj_body(x_ref, g_ref, wrkv_ref, wlora_ref, wdiff_ref, mu_ref, wup_ref, ones_ref, w0_ref,
               a0_ref, kk_ref, ka_ref, rk_ref,
               r_out, ld_out, k_out, v_out, kk_out, ic_out, bonus_out, gate_out,
               qd_out, kd_out, vd_out, carry_ref):
    @pl.when(pl.program_id(1) == 0)
    def _():
        carry_ref[...] = jnp.zeros_like(carry_ref)

    tm = x_ref.shape[1]
    nrkv = 3 * RWKV_W
    first_row = lax.broadcasted_iota(jnp.int32, (tm, 1), 0) == 0
    lane = lax.broadcasted_iota(jnp.int32, (1, LORA_PAD), 1)
    ones_bd = ones_ref[...]

    def shift_mix(p, cols):
        prev = jnp.where(first_row, carry_ref[:, cols], pltpu.roll(p, 1, 0))
        carry_ref[:, cols] = p[tm - 1:]
        return p + (prev - p) * mu_ref[:, cols]

    h = _rms(x_ref[0], g_ref[...], NORM_EPS).astype(BF16)
    z = shift_mix(_dot(h, wlora_ref[...]), slice(nrkv, SHIFT_PAD))
    p_k = shift_mix(_dot(h, wrkv_ref[:, RWKV_W:2 * RWKV_W]), slice(RWKV_W, 2 * RWKV_W))
    act = jnp.where(lane < DECAY_LORA, jnp.tanh(z),
                    jnp.where(lane < DECAY_LORA + ICLR_LORA, z, jax.nn.sigmoid(z))).astype(BF16)
    up = _dot(act[:, :LANES], wup_ref[:LANES, :2 * RWKV_W])
    gate_out[0] = _dot(act[:, LANES:], wup_ref[LANES:, 2 * RWKV_W:])
    p_r = shift_mix(_dot(h, wrkv_ref[:, :RWKV_W]), slice(0, RWKV_W))
    p_v = shift_mix(_dot(h, wrkv_ref[:, 2 * RWKV_W:]), slice(2 * RWKV_W, nrkv))
    p_diff = _dot(h, wdiff_ref[...])

    zw = -(w0_ref[...] + up[:, :RWKV_W])
    softplus = jnp.maximum(zw, 0.0) + jnp.log(1.0 + jnp.exp(-jnp.abs(zw)))
    ld_out[0] = -jnp.exp(-softplus - 0.5)
    iclr = jax.nn.sigmoid(a0_ref[...] + up[:, RWKV_W:])
    kk = p_k * kk_ref[...]
    kk = kk * lax.rsqrt(jnp.maximum(_dot((kk * kk).astype(BF16), ones_bd), 1e-24))
    k = p_k * (1.0 + (iclr - 1.0) * ka_ref[...])
    r_out[0] = p_r
    k_out[0] = k
    v_out[0] = p_v
    kk_out[0] = kk
    ic_out[0] = iclr
    bonus_out[0] = _dot((p_r * k * rk_ref[...]).astype(BF16), ones_bd) * p_v

    qd_out[0] = (p_diff[:, :DIFF_W] * (DIFF_QK ** -0.5 * LOG2E)).astype(BF16)
    kd_out[0] = p_diff[:, DIFF_W:2 * DIFF_W].astype(BF16)
    vd_out[0] = p_diff[:, 2 * DIFF_W:].T.astype(BF16)


def _head_ones(width, head):
    i = jnp.arange(width) // head
    return (i[:, None] == i[None, :]).astype(BF16)


def _proj(x, g, w_in, shift_mu, w_up, a_up, g_up, w0, a0, k_k, k_a, r_k, *, tm):
    b, t, d = x.shape
    pad = SHIFT_PAD - SHIFT_COLS
    nrkv = 3 * RWKV_W
    w = w_in.astype(BF16)
    w_lora = jnp.pad(w[:, nrkv:SHIFT_COLS], ((0, 0), (0, pad)))
    w_diff = w[:, SHIFT_COLS:]
    mu = jnp.pad(shift_mu, (0, pad)).reshape(1, SHIFT_PAD)
    wup = jnp.zeros((LORA_PAD, 3 * RWKV_W), F32)
    wup = wup.at[:DECAY_LORA, :RWKV_W].set(w_up)
    wup = wup.at[DECAY_LORA:DECAY_LORA + ICLR_LORA, RWKV_W:2 * RWKV_W].set(a_up)
    wup = wup.at[DECAY_LORA + ICLR_LORA:LORA_W, 2 * RWKV_W:].set(g_up)
    wup = wup.astype(BF16)
    vec = lambda a: a.reshape(1, RWKV_W)
    const = lambda shape: pl.BlockSpec(shape, lambda i, j: (0,) * len(shape))
    tile = lambda width: pl.BlockSpec((1, tm, width), lambda i, j: (i, j, 0))
    f32_out = jax.ShapeDtypeStruct((b, t, RWKV_W), F32)
    bf_out = jax.ShapeDtypeStruct((b, t, DIFF_W), BF16)
    return pl.pallas_call(
        _proj_body,
        grid=(b, t // tm),
        in_specs=[tile(d), const((1, d)), const((d, nrkv)), const((d, LORA_PAD)),
                  const((d, 3 * DIFF_W)), const((1, SHIFT_PAD)),
                  const((LORA_PAD, 3 * RWKV_W)), const((RWKV_W, RWKV_W))] + [const((1, RWKV_W))] * 5,
        out_specs=[tile(RWKV_W)] * 8 + [tile(DIFF_W)] * 2
        + [pl.BlockSpec((1, DIFF_W, tm), lambda i, j: (i, 0, j))],
        out_shape=[f32_out] * 8 + [bf_out] * 2 + [jax.ShapeDtypeStruct((b, DIFF_W, t), BF16)],
        scratch_shapes=[pltpu.VMEM((1, SHIFT_PAD), F32)],
        compiler_params=pltpu.CompilerParams(
            dimension_semantics=("arbitrary", "arbitrary"), vmem_limit_bytes=VMEM_LIMIT),
        name="proj",
    )(x, g.reshape(1, d), w, w_lora, w_diff, mu, wup, _head_ones(RWKV_W, RWKV_HEAD),
      vec(w0), vec(a0), vec(k_k), vec(k_a), vec(r_k))


def _rwkv_body(r_ref, ld_ref, k_ref, v_ref, kk_ref, ic_ref, bonus_ref, gate_ref, gnw_ref, gnb_ref,
               o_ref, h_ref, *, chunk):
    c = chunk
    c2 = 2 * c
    nc = r_ref.shape[1] // c

    @pl.when(pl.program_id(1) == 0)
    def _():
        h_ref[...] = jnp.zeros_like(h_ref)

    ri = lax.broadcasted_iota(jnp.int32, (c, c), 0)
    ci = lax.broadcasted_iota(jnp.int32, (c, c), 1)
    tril3 = jnp.concatenate([(ci <= ri).astype(BF16)] * 3, axis=1)
    at, rt, bt, kt, bh, kh, v, g_tot = [], [], [], [], [], [], [], []
    for i in range(nc):
        rows = slice(i * c, (i + 1) * c)
        ld = ld_ref[0, rows, :]
        ld1 = ld.astype(BF16)
        res = ld - ld1.astype(F32)
        ld2 = res.astype(BF16)
        ld3 = (res - ld2.astype(F32)).astype(BF16)
        cum = _dot(tril3, jnp.concatenate([ld1, ld2, ld3], axis=0))
        tot = cum[c - 1:c, :]
        g_inv = jnp.exp(-cum)
        g_rest = jnp.exp(tot - cum)
        kk = kk_ref[0, rows, :]
        b = kk * ic_ref[0, rows, :]
        k = k_ref[0, rows, :]
        at.append((-kk * jnp.exp(cum - ld)).astype(BF16))
        rt.append((r_ref[0, rows, :] * jnp.exp(cum)).astype(BF16))
        bt.append((b * g_inv).astype(BF16))
        kt.append((k * g_inv).astype(BF16))
        bh.append((b * g_rest).astype(BF16))
        kh.append((k * g_rest).astype(BF16))
        v.append(v_ref[0, rows, :].astype(BF16))
        g_tot.append(jnp.exp(tot))

    lane = lax.broadcasted_iota(jnp.int32, (1, LANES), 1)
    lo = lane < RWKV_HEAD

    def stack(x):
        zero = jnp.zeros_like(x)
        return jnp.concatenate([jnp.where(lo, x, zero), jnp.where(lo, zero, x)], axis=0)

    row2 = lax.broadcasted_iota(jnp.int32, (c2, c2), 0)
    col2 = lax.broadcasted_iota(jnp.int32, (c2, c2), 1)
    same = (row2 // c) == (col2 // c)
    strict = same & (col2 < row2)
    incl = same & (col2 <= row2)
    eye = (row2 == col2).astype(F32)
    levels = int(math.log2(c))
    level_masks = [((row2 >> l) == (col2 >> l)) & ((row2 >> (l - 1)) != (col2 >> (l - 1)))
                   & (col2 < row2) for l in range(1, levels + 1)]
    ones_bd = ((lax.broadcasted_iota(jnp.int32, (LANES, LANES), 0) // RWKV_HEAD)
               == (lax.broadcasted_iota(jnp.int32, (LANES, LANES), 1) // RWKV_HEAD)).astype(BF16)
    ones_bd2 = jnp.concatenate([ones_bd, ones_bd], axis=0)
    nt = lambda x, y: lax.dot_general(x, y, _NT, preferred_element_type=F32)
    tn = lambda x, y: lax.dot_general(x, y, _TN, preferred_element_type=F32)

    npairs = RWKV_HEADS // 2
    chains = [(i, p) for i in range(nc) for p in range(npairs)]
    sls = [slice(p * LANES, (p + 1) * LANES) for p in range(npairs)]
    each = lambda f: {ip: f(ip) for ip in chains}
    pick = lambda xs: each(lambda ip: stack(xs[ip[0]][:, sls[ip[1]]]))
    a2, r2, b2, k2, bh2, kh2, v2 = (pick(xs) for xs in (at, rt, bt, kt, bh, kh, v))

    cat0 = lambda *xs: jnp.concatenate(xs, axis=0)
    cat1 = lambda *xs: jnp.concatenate(xs, axis=1)
    s_ar = each(lambda ip: nt(cat0(a2[ip], r2[ip]), cat0(b2[ip], k2[ip])))
    a_ab = each(lambda ip: jnp.where(strict, s_ar[ip][:c2, :c2], 0.0))
    a_ak = each(lambda ip: jnp.where(strict, s_ar[ip][:c2, c2:], 0.0).astype(BF16))
    a_rb = each(lambda ip: jnp.where(incl, s_ar[ip][c2:, :c2], 0.0).astype(BF16))
    a_rk = each(lambda ip: jnp.where(incl, s_ar[ip][c2:, c2:], 0.0).astype(BF16))

    tinv = each(lambda ip: eye + jnp.where(level_masks[0], a_ab[ip], 0.0))
    for l in range(2, levels + 1):
        e = each(lambda ip: jnp.where(level_masks[l - 1], a_ab[ip], 0.0).astype(BF16))
        tb = each(lambda ip: tinv[ip].astype(BF16))
        te = each(lambda ip: _dot(tb[ip], e[ip]).astype(BF16))
        tinv = each(lambda ip: tinv[ip] + _dot(te[ip], tb[ip]))
    tb = each(lambda ip: tinv[ip].astype(BF16))

    av = each(lambda ip: _dot(cat0(a_ak[ip], a_rk[ip]), v2[ip]))
    akv = each(lambda ip: av[ip][:c2].astype(BF16))
    wu = each(lambda ip: _dot(tb[ip], cat1(a2[ip], akv[ip])).astype(BF16))
    pg = each(lambda ip: tn(bh2[ip], wu[ip]))
    gm = each(lambda ip: pg[ip][:, LANES:] + tn(kh2[ip], v2[ip]))
    qy = each(lambda ip: _dot(a_rb[ip], wu[ip]))
    qp = each(lambda ip: cat0((r2[ip].astype(F32) + qy[ip][:, :LANES]).astype(BF16),
                              pg[ip][:, :LANES].astype(BF16)))
    y0 = each(lambda ip: qy[ip][:, LANES:] + av[ip][c2:])
    assert c2 == LANES
    g_col = each(lambda ip: jnp.sum(eye * g_tot[ip[0]][:, sls[ip[1]]], axis=1, keepdims=True))

    state = [h_ref[p] for p in range(npairs)]
    ys = [[] for _ in range(npairs)]
    for i in range(nc):
        for p in range(npairs):
            qh = _dot(qp[i, p], state[p].astype(BF16))
            y2 = qh[:c2] + y0[i, p]
            ys[p].append(y2[:c] + y2[c:])
            state[p] = g_col[i, p] * state[p] + qh[c2:] + gm[i, p]
    for p in range(npairs):
        h_ref[p] = state[p]

    for p in range(npairs):
        sl = sls[p]
        y = cat0(*ys[p])
        mean = _split_dot_k2(y, ones_bd2) * (1.0 / RWKV_HEAD)
        yc = y - mean
        var = _split_dot_k2(yc * yc, ones_bd2) * (1.0 / RWKV_HEAD)
        yn = yc * lax.rsqrt(var + GN_EPS) * gnw_ref[:, sl] + gnb_ref[:, sl]
        o_ref[0, :, sl] = ((yn + bonus_ref[0, :, sl]) * gate_ref[0, :, sl]).astype(o_ref.dtype)


def _rwkv(r, ld, k, v, kk, iclr, bonus, gate, gn_w, gn_b, *, chunk, tb):
    b, t, w = r.shape
    tile = pl.BlockSpec((1, tb, w), lambda i, j: (i, j, 0))
    const = pl.BlockSpec((1, w), lambda i, j: (0, 0))
    return pl.pallas_call(
        functools.partial(_rwkv_body, chunk=chunk),
        grid=(b, t // tb),
        in_specs=[tile] * 8 + [const] * 2,
        out_specs=tile,
        out_shape=jax.ShapeDtypeStruct((b, t, w), BF16),
        scratch_shapes=[pltpu.VMEM((RWKV_HEADS // 2, LANES, LANES), F32)],
        compiler_params=pltpu.CompilerParams(
            dimension_semantics=("arbitrary", "arbitrary"), vmem_limit_bytes=VMEM_LIMIT),
        name="rwkv",
    )(r, ld, k, v, kk, iclr, bonus, gate, gn_w.reshape(1, w), gn_b.reshape(1, w))


def _attn_body(lq1_ref, lk1_ref, lq2_ref, lk2_ref, q_ref, k_ref, vt_ref, sw_ref, o_ref, acc_ref,
               s_ref, m_ref, l_ref, *, tq, tk, wq, lambda_init):
    qi = pl.program_id(2)
    q = q_ref[0]
    lane = lax.broadcasted_iota(jnp.int32, (1, LANES), 1)
    zero = jnp.zeros_like(q)
    qs = (jnp.where(lane < DIFF_QK, q, zero), jnp.where(lane < DIFF_QK, zero, q))
    acc_ref[...] = jnp.zeros_like(acc_ref)
    m_ref[...] = jnp.full_like(m_ref, NEG_BIG)
    l_ref[...] = jnp.zeros_like(l_ref)
    items = [(c, w) for c in range(2) for w in range(tq // wq)]
    assert tq == 2 * tk

    def scores_into(slot, j, q_from=0):
        ks = k_ref[0, pl.ds(pl.multiple_of(j * tk, tk), tk), :]
        for c in range(2):
            s_ref[slot, c, :, q_from:] = lax.dot_general(ks, qs[c][q_from:], _NT,
                                                         preferred_element_type=F32)

    def consume(slot, j, diag):
        start = pl.multiple_of(j * tk, tk)
        for c, w in items:
            nk = tk if diag is None else min(tk, max(0, (w + 1) * wq - diag * tk))
            if nk == 0:
                continue
            cols = slice(w * wq, (w + 1) * wq)
            s = s_ref[slot, c, 0:nk, cols]
            if diag is not None and diag * tk + nk - 1 > w * wq:
                key = lax.broadcasted_iota(jnp.int32, (nk, wq), 0) + diag * tk
                qry = lax.broadcasted_iota(jnp.int32, (nk, wq), 1) + w * wq
                s = jnp.where(key <= qry, s, NEG_BIG)
            m = m_ref[c, :, cols]
            m_new = jnp.maximum(m, jnp.max(s, axis=0, keepdims=True))
            alpha = jnp.exp2(m - m_new)
            pr = jnp.exp2(s - m_new)
            m_ref[c, :, cols] = m_new
            l_ref[c, :, cols] = alpha * l_ref[c, :, cols] + jnp.sum(pr, axis=0, keepdims=True)
            vt = vt_ref[0, :, pl.ds(start, nk)]
            acc_ref[c, :, cols] = alpha * acc_ref[c, :, cols] + _dot(vt, pr.astype(BF16))

    def body(jj, carry):
        j = 2 * jj
        scores_into(1, j + 1)
        consume(0, j, None)
        scores_into(0, j + 2)
        consume(1, j + 1, None)
        return carry

    scores_into(0, 0)
    lax.fori_loop(0, qi, body, 0)
    scores_into(1, 2 * qi + 1, q_from=tk)
    consume(0, 2 * qi, 0)
    consume(1, 2 * qi + 1, 1)

    l1, l2 = l_ref[0], l_ref[1]

    lam = (jnp.exp(jnp.sum(lq1_ref[...] * lk1_ref[...], axis=-1, keepdims=True))
           - jnp.exp(jnp.sum(lq2_ref[...] * lk2_ref[...], axis=-1, keepdims=True)) + lambda_init)
    o = acc_ref[0] * (1.0 / l1) - lam * (acc_ref[1] * (1.0 / l2))
    o = o * lax.rsqrt(jnp.mean(o * o, axis=0, keepdims=True) + SUBLN_EPS) * sw_ref[...]
    o = o * (1.0 - lambda_init)
    o_ref[0] = o.T.astype(o_ref.dtype)


def _attn(q, k, vt, lq1, lk1, lq2, lk2, subln_w, *, tq, wq, lambda_init):
    tk = tq // 2
    b, t, w = q.shape
    heads = w // DIFF_V
    lam_spec = pl.BlockSpec((1, DIFF_QK), lambda i, h, j: (0, 0))
    return pl.pallas_call(
        functools.partial(_attn_body, tq=tq, tk=tk, wq=wq, lambda_init=lambda_init),
        grid=(b, heads, t // tq),
        in_specs=[lam_spec] * 4 + [
            pl.BlockSpec((1, tq, DIFF_V), lambda i, h, j: (i, j, h)),
            pl.BlockSpec((1, t, DIFF_V), lambda i, h, j: (i, 0, h)),
            pl.BlockSpec((1, DIFF_V, t), lambda i, h, j: (i, h, 0)),
            pl.BlockSpec((DIFF_V, 1), lambda i, h, j: (0, 0)),
        ],
        out_specs=pl.BlockSpec((1, tq, DIFF_V), lambda i, h, j: (i, j, h)),
        out_shape=jax.ShapeDtypeStruct((b, t, w), BF16),
        scratch_shapes=[pltpu.VMEM((2, DIFF_V, tq), F32), pltpu.VMEM((2, 2, tk, tq), F32),
                        pltpu.VMEM((2, 1, tq), F32), pltpu.VMEM((2, 1, tq), F32)],
        compiler_params=pltpu.CompilerParams(
            dimension_semantics=("parallel", "parallel", "arbitrary"), vmem_limit_bytes=VMEM_LIMIT),
        name="attn",
    )(lq1.reshape(1, -1), lk1.reshape(1, -1), lq2.reshape(1, -1), lk2.reshape(1, -1),
      q, k, vt, subln_w.reshape(-1, 1))


def _pick(n, pref):
    return pref if n % pref == 0 else n


def _layer(x, l, p, *, chunk=64, tb_rwkv=256, tm_ffn=1024, tf=256, tm_proj=512, tq=1024,
           wq_attn=256):
    b, t, d = x.shape
    m = b * t
    lambda_init = 0.8 - 0.6 * math.exp(-0.3 * l)
    tm_ffn = _pick(m, tm_ffn)
    tf = _pick(p["ffn1_w_gate"].shape[-1], tf)
    x = _ffn(x.reshape(m, d), p["ffn1_pre_g"][l], p["ffn1_post_g"][l], p["ffn1_w_gate"][l],
             p["ffn1_w_up"][l], p["ffn1_w_down"][l], tm=tm_ffn, tf=tf)
    outs = _proj(x.reshape(b, t, d), p["mix_pre_g"][l], p["w_in"][l], p["shift_mu"][l],
                 p["rwkv_w_up"][l], p["rwkv_a_up"][l], p["rwkv_g_up"][l], p["rwkv_w0"][l],
                 p["rwkv_a0"][l], p["rwkv_k_k"][l], p["rwkv_k_a"][l], p["rwkv_r_k"][l].reshape(-1),
                 tm=_pick(t, tm_proj))
    y_rwkv = _rwkv(*outs[:8], p["rwkv_gn_w"][l], p["rwkv_gn_b"][l], chunk=chunk,
                   tb=_pick(t, tb_rwkv))
    y_diff = _attn(*outs[8:], p["diff_lam_q1"][l], p["diff_lam_k1"][l], p["diff_lam_q2"][l],
                   p["diff_lam_k2"][l], p["diff_subln_w"][l], tq=_pick(t, tq), wq=wq_attn,
                   lambda_init=lambda_init)
    x = _ffn(x, p["ffn2_pre_g"][l], p["ffn2_post_g"][l], p["ffn2_w_gate"][l], p["ffn2_w_up"][l],
             p["ffn2_w_down"][l], tm=tm_ffn, tf=tf,
             mix=(y_rwkv.reshape(m, -1), y_diff.reshape(m, -1), p["w_o"][l], p["mix_post_g"][l]))
    return x.reshape(b, t, d)


def kernel(x, ffn1_pre_g, ffn1_post_g, ffn1_w_gate, ffn1_w_up, ffn1_w_down, mix_pre_g, mix_post_g,
           w_in, shift_mu, w_o, rwkv_w0, rwkv_w_up, rwkv_a0, rwkv_a_up, rwkv_g_up, rwkv_k_k,
           rwkv_k_a, rwkv_r_k, rwkv_gn_w, rwkv_gn_b, diff_lam_q1, diff_lam_k1, diff_lam_q2,
           diff_lam_k2, diff_subln_w, ffn2_pre_g, ffn2_post_g, ffn2_w_gate, ffn2_w_up, ffn2_w_down):
    p = dict(ffn1_pre_g=ffn1_pre_g, ffn1_post_g=ffn1_post_g, ffn1_w_gate=ffn1_w_gate,
             ffn1_w_up=ffn1_w_up, ffn1_w_down=ffn1_w_down, mix_pre_g=mix_pre_g,
             mix_post_g=mix_post_g, w_in=w_in, shift_mu=shift_mu, w_o=w_o, rwkv_w0=rwkv_w0,
             rwkv_w_up=rwkv_w_up, rwkv_a0=rwkv_a0, rwkv_a_up=rwkv_a_up, rwkv_g_up=rwkv_g_up,
             rwkv_k_k=rwkv_k_k, rwkv_k_a=rwkv_k_a, rwkv_r_k=rwkv_r_k, rwkv_gn_w=rwkv_gn_w,
             rwkv_gn_b=rwkv_gn_b, diff_lam_q1=diff_lam_q1, diff_lam_k1=diff_lam_k1,
             diff_lam_q2=diff_lam_q2, diff_lam_k2=diff_lam_k2, diff_subln_w=diff_subln_w,
             ffn2_pre_g=ffn2_pre_g, ffn2_post_g=ffn2_post_g, ffn2_w_gate=ffn2_w_gate,
             ffn2_w_up=ffn2_w_up, ffn2_w_down=ffn2_w_down)
    for l in range(ffn1_pre_g.shape[0]):
        x = _layer(x, l, p)
    return x
```

```python
import functools
import math

import jax
import jax.numpy as jnp
from jax import lax
from jax.experimental import pallas as pl
from jax.experimental.pallas import tpu as pltpu

F32 = jnp.float32
BF16 = jnp.bfloat16

RWKV_HEADS = 8
RWKV_HEAD = 64
RWKV_W = RWKV_HEADS * RWKV_HEAD
DIFF_HEADS = 4
DIFF_QK = 64
DIFF_V = 2 * DIFF_QK
DIFF_W = DIFF_HEADS * DIFF_V
DECAY_LORA = 64
ICLR_LORA = 64
GATE_LORA = 160
LORA_W = DECAY_LORA + ICLR_LORA + GATE_LORA
LANES = 128
BF16_ROWS = 16
LORA_PAD = -(-LORA_W // LANES) * LANES
SHIFT_COLS = 3 * RWKV_W + LORA_W
SHIFT_PAD = 3 * RWKV_W + LORA_PAD
FFN_RES = 0.5
NORM_EPS = 1e-6
GN_EPS = 64e-5
SUBLN_EPS = 1e-5
NEG_BIG = -1e30
LOG2E = math.log2(math.e)
VMEM_LIMIT = 56 * 1024 * 1024

_NT = (((1,), (1,)), ((), ()))
_TN = (((0,), (0,)), ((), ()))


def _dot(a, b):
    return jnp.dot(a, b, preferred_element_type=F32)


def _split_dot_k2(x, w2_bf16):
    hi = x.astype(BF16)
    lo = (x - hi.astype(F32)).astype(BF16)
    return _dot(jnp.concatenate([hi, lo], axis=1), w2_bf16)


def _rms(x, g, eps):
    return x * lax.rsqrt(jnp.mean(x * x, axis=-1, keepdims=True) + eps) * g


def _ffn_body(*refs, tf, mix):
    if mix:
        x_ref, yr_ref, yd_ref, wo_ref, gmix_ref, *refs = refs
    else:
        x_ref, *refs = refs
    gpre_ref, gpost_ref, wg_ref, wu_ref, wd_ref, o_ref, a_ref = refs
    x = x_ref[...]
    if mix:
        nr = yr_ref.shape[1]
        y = _dot(yr_ref[...], wo_ref[:nr]) + _dot(yd_ref[...], wo_ref[nr:])
        x = x + _rms(y, gmix_ref[...], NORM_EPS)
    h = _rms(x, gpre_ref[...], NORM_EPS).astype(BF16)
    for f0 in range(0, wg_ref.shape[1], tf):
        cols = slice(f0, f0 + tf)
        g = _dot(h, wg_ref[:, cols])
        u = _dot(h, wu_ref[:, cols])
        a_ref[:, cols] = (g * jax.nn.sigmoid(g) * u).astype(BF16)
    y = _dot(a_ref[...], wd_ref[...])
    o_ref[...] = x + FFN_RES * _rms(y, gpost_ref[...], NORM_EPS)


def _ffn(x, g_pre, g_post, w_gate, w_up, w_down, *, tm, tf, mix=None):
    m, d = x.shape
    f = w_gate.shape[1]
    rows = lambda width: pl.BlockSpec((tm, width), lambda i: (i, 0))
    resident = lambda shape: pl.BlockSpec(shape, lambda i: (0, 0), pipeline_mode=pl.Buffered(1))
    args, specs = [x], [rows(d)]
    if mix is not None:
        y_rwkv, y_diff, w_o, g_mix = mix
        args += [y_rwkv, y_diff, w_o.astype(BF16), g_mix.reshape(1, d)]
        specs += [rows(y_rwkv.shape[1]), rows(y_diff.shape[1]), resident(w_o.shape), resident((1, d))]
    args += [g_pre.reshape(1, d), g_post.reshape(1, d), w_gate.astype(BF16), w_up.astype(BF16),
             w_down.astype(BF16)]
    specs += [resident((1, d)), resident((1, d)), resident((d, f)), resident((d, f)),
              resident((f, d))]
    return pl.pallas_call(
        functools.partial(_ffn_body, tf=tf, mix=mix is not None),
        grid=(m // tm,),
        in_specs=specs,
        out_specs=rows(d),
        out_shape=jax.ShapeDtypeStruct((m, d), F32),
        scratch_shapes=[pltpu.VMEM((tm, f), BF16)],
        compiler_params=pltpu.CompilerParams(
            dimension_semantics=("parallel",), vmem_limit_bytes=VMEM_LIMIT),
        name="ffn_mix" if mix is not None else "ffn",
    )(*args)


def _proj_body(x_ref, g_ref, wrkv_ref, wlora_ref, wdiff_ref, mu_ref, wup_ref, ones_ref, w0_ref,
               a0_ref, kk_ref, ka_ref, rk_ref,
               r_out, ld_out, k_out, v_out, kk_out, ic_out, bonus_out, gate_out,
               qd_out, kd_out, vd_out, carry_ref):
    @pl.when(pl.program_id(1) == 0)
    def _():
        carry_ref[...] = jnp.zeros_like(carry_ref)

    tm = x_ref.shape[1]
    nrkv = 3 * RWKV_W
    first_row = lax.broadcasted_iota(jnp.int32, (tm, 1), 0) == 0
    lane = lax.broadcasted_iota(jnp.int32, (1, LORA_PAD), 1)
    ones_bd = ones_ref[...]

    def shift_mix(p, cols):
        prev = jnp.where(first_row, carry_ref[:, cols], pltpu.roll(p, 1, 0))
        carry_ref[:, cols] = p[tm - 1:]
        return p + (prev - p) * mu_ref[:, cols]

    h = _rms(x_ref[0], g_ref[...], NORM_EPS).astype(BF16)
    z = shift_mix(_dot(h, wlora_ref[...]), slice(nrkv, SHIFT_PAD))
    p_k = shift_mix(_dot(h, wrkv_ref[:, RWKV_W:2 * RWKV_W]), slice(RWKV_W, 2 * RWKV_W))
    act = jnp.where(lane < DECAY_LORA, jnp.tanh(z),
                    jnp.where(lane < DECAY_LORA + ICLR_LORA, z, jax.nn.sigmoid(z))).astype(BF16)
    up = _dot(act[:, :LANES], wup_ref[:LANES, :2 * RWKV_W])
    gate_out[0] = _dot(act[:, LANES:], wup_ref[LANES:, 2 * RWKV_W:])
    p_r = shift_mix(_dot(h, wrkv_ref[:, :RWKV_W]), slice(0, RWKV_W))
    p_v = shift_mix(_dot(h, wrkv_ref[:, 2 * RWKV_W:]), slice(2 * RWKV_W, nrkv))
    p_diff = _dot(h, wdiff_ref[...])

    zw = -(w0_ref[...] + up[:, :RWKV_W])
    softplus = jnp.maximum(zw, 0.0) + jnp.log(1.0 + jnp.exp(-jnp.abs(zw)))
    ld_out[0] = -jnp.exp(-softplus - 0.5)
    iclr = jax.nn.sigmoid(a0_ref[...] + up[:, RWKV_W:])
    kk = p_k * kk_ref[...]
    kk = kk * lax.rsqrt(jnp.maximum(_dot((kk * kk).astype(BF16), ones_bd), 1e-24))
    k = p_k * (1.0 + (iclr - 1.0) * ka_ref[...])
    r_out[0] = p_r
    k_out[0] = k
    v_out[0] = p_v
    kk_out[0] = kk
    ic_out[0] = iclr
    bonus_out[0] = _dot((p_r * k * rk_ref[...]).astype(BF16), ones_bd) * p_v

    qd_out[0] = (p_diff[:, :DIFF_W] * (DIFF_QK ** -0.5 * LOG2E)).astype(BF16)
    kd_out[0] = p_diff[:, DIFF_W:2 * DIFF_W].astype(BF16)
    vd_out[0] = p_diff[:, 2 * DIFF_W:].T.astype(BF16)


def _head_ones(width, head):
    i = jnp.arange(width) // head
    return (i[:, None] == i[None, :]).astype(BF16)


def _proj(x, g, w_in, shift_mu, w_up, a_up, g_up, w0, a0, k_k, k_a, r_k, *, tm):
    b, t, d = x.shape
    pad = SHIFT_PAD - SHIFT_COLS
    nrkv = 3 * RWKV_W
    w = w_in.astype(BF16)
    w_lora = jnp.pad(w[:, nrkv:SHIFT_COLS], ((0, 0), (0, pad)))
    w_diff = w[:, SHIFT_COLS:]
    mu = jnp.pad(shift_mu, (0, pad)).reshape(1, SHIFT_PAD)
    wup = jnp.zeros((LORA_PAD, 3 * RWKV_W), F32)
    wup = wup.at[:DECAY_LORA, :RWKV_W].set(w_up)
    wup = wup.at[DECAY_LORA:DECAY_LORA + ICLR_LORA, RWKV_W:2 * RWKV_W].set(a_up)
    wup = wup.at[DECAY_LORA + ICLR_LORA:LORA_W, 2 * RWKV_W:].set(g_up)
    wup = wup.astype(BF16)
    vec = lambda a: a.reshape(1, RWKV_W)
    const = lambda shape: pl.BlockSpec(shape, lambda i, j: (0,) * len(shape))
    tile = lambda width: pl.BlockSpec((1, tm, width), lambda i, j: (i, j, 0))
    f32_out = jax.ShapeDtypeStruct((b, t, RWKV_W), F32)
    bf_out = jax.ShapeDtypeStruct((b, t, DIFF_W), BF16)
    return pl.pallas_call(
        _proj_body,
        grid=(b, t // tm),
        in_specs=[tile(d), const((1, d)), const((d, nrkv)), const((d, LORA_PAD)),
                  const((d, 3 * DIFF_W)), const((1, SHIFT_PAD)),
                  const((LORA_PAD, 3 * RWKV_W)), const((RWKV_W, RWKV_W))] + [const((1, RWKV_W))] * 5,
        out_specs=[tile(RWKV_W)] * 8 + [tile(DIFF_W)] * 2
        + [pl.BlockSpec((1, DIFF_W, tm), lambda i, j: (i, 0, j))],
        out_shape=[f32_out] * 8 + [bf_out] * 2 + [jax.ShapeDtypeStruct((b, DIFF_W, t), BF16)],
        scratch_shapes=[pltpu.VMEM((1, SHIFT_PAD), F32)],
        compiler_params=pltpu.CompilerParams(
            dimension_semantics=("arbitrary", "arbitrary"), vmem_limit_bytes=VMEM_LIMIT),
        name="proj",
    )(x, g.reshape(1, d), w, w_lora, w_diff, mu, wup, _head_ones(RWKV_W, RWKV_HEAD),
      vec(w0), vec(a0), vec(k_k), vec(k_a), vec(r_k))


def _rwkv_body(r_ref, ld_ref, k_ref, v_ref, kk_ref, ic_ref, bonus_ref, gate_ref, gnw_ref, gnb_ref,
               o_ref, h_ref, *, chunk):
    c = chunk
    c2 = 2 * c
    nc = r_ref.shape[1] // c

    @pl.when(pl.program_id(1) == 0)
    def _():
        h_ref[...] = jnp.zeros_like(h_ref)

    ri = lax.broadcasted_iota(jnp.int32, (c, c), 0)
    ci = lax.broadcasted_iota(jnp.int32, (c, c), 1)
    tril3 = jnp.concatenate([(ci <= ri).astype(BF16)] * 3, axis=1)
    at, rt, bt, kt, bh, kh, v, g_tot = [], [], [], [], [], [], [], []
    for i in range(nc):
        rows = slice(i * c, (i + 1) * c)
        ld = ld_ref[0, rows, :]
        ld1 = ld.astype(BF16)
        res = ld - ld1.astype(F32)
        ld2 = res.astype(BF16)
        ld3 = (res - ld2.astype(F32)).astype(BF16)
        cum = _dot(tril3, jnp.concatenate([ld1, ld2, ld3], axis=0))
        tot = cum[c - 1:c, :]
        g_inv = jnp.exp(-cum)
        g_rest = jnp.exp(tot - cum)
        kk = kk_ref[0, rows, :]
        b = kk * ic_ref[0, rows, :]
        k = k_ref[0, rows, :]
        at.append((-kk * jnp.exp(cum - ld)).astype(BF16))
        rt.append((r_ref[0, rows, :] * jnp.exp(cum)).astype(BF16))
        bt.append((b * g_inv).astype(BF16))
        kt.append((k * g_inv).astype(BF16))
        bh.append((b * g_rest).astype(BF16))
        kh.append((k * g_rest).astype(BF16))
        v.append(v_ref[0, rows, :].astype(BF16))
        g_tot.append(jnp.exp(tot))

    lane = lax.broadcasted_iota(jnp.int32, (1, LANES), 1)
    lo = lane < RWKV_HEAD

    def stack(x):
        zero = jnp.zeros_like(x)
        return jnp.concatenate([jnp.where(lo, x, zero), jnp.where(lo, zero, x)], axis=0)

    row2 = lax.broadcasted_iota(jnp.int32, (c2, c2), 0)
    col2 = lax.broadcasted_iota(jnp.int32, (c2, c2), 1)
    same = (row2 // c) == (col2 // c)
    strict = same & (col2 < row2)
    incl = same & (col2 <= row2)
    eye = (row2 == col2).astype(F32)
    levels = int(math.log2(c))
    level_masks = [((row2 >> l) == (col2 >> l)) & ((row2 >> (l - 1)) != (col2 >> (l - 1)))
                   & (col2 < row2) for l in range(1, levels + 1)]
    ones_bd = ((lax.broadcasted_iota(jnp.int32, (LANES, LANES), 0) // RWKV_HEAD)
               == (lax.broadcasted_iota(jnp.int32, (LANES, LANES), 1) // RWKV_HEAD)).astype(BF16)
    ones_bd2 = jnp.concatenate([ones_bd, ones_bd], axis=0)
    nt = lambda x, y: lax.dot_general(x, y, _NT, preferred_element_type=F32)
    tn = lambda x, y: lax.dot_general(x, y, _TN, preferred_element_type=F32)

    npairs = RWKV_HEADS // 2
    chains = [(i, p) for i in range(nc) for p in range(npairs)]
    sls = [slice(p * LANES, (p + 1) * LANES) for p in range(npairs)]
    each = lambda f: {ip: f(ip) for ip in chains}
    pick = lambda xs: each(lambda ip: stack(xs[ip[0]][:, sls[ip[1]]]))
    a2, r2, b2, k2, bh2, kh2, v2 = (pick(xs) for xs in (at, rt, bt, kt, bh, kh, v))

    cat0 = lambda *xs: jnp.concatenate(xs, axis=0)
    cat1 = lambda *xs: jnp.concatenate(xs, axis=1)
    s_ar = each(lambda ip: nt(cat0(a2[ip], r2[ip]), cat0(b2[ip], k2[ip])))
    a_ab = each(lambda ip: jnp.where(strict, s_ar[ip][:c2, :c2], 0.0))
    a_ak = each(lambda ip: jnp.where(strict, s_ar[ip][:c2, c2:], 0.0).astype(BF16))
    a_rb = each(lambda ip: jnp.where(incl, s_ar[ip][c2:, :c2], 0.0).astype(BF16))
    a_rk = each(lambda ip: jnp.where(incl, s_ar[ip][c2:, c2:], 0.0).astype(BF16))

    tinv = each(lambda ip: eye + jnp.where(level_masks[0], a_ab[ip], 0.0))
    for l in range(2, levels + 1):
        e = each(lambda ip: jnp.where(level_masks[l - 1], a_ab[ip], 0.0).astype(BF16))
        tb = each(lambda ip: tinv[ip].astype(BF16))
        te = each(lambda ip: _dot(tb[ip], e[ip]).astype(BF16))
        tinv = each(lambda ip: tinv[ip] + _dot(te[ip], tb[ip]))
    tb = each(lambda ip: tinv[ip].astype(BF16))

    av = each(lambda ip: _dot(cat0(a_ak[ip], a_rk[ip]), v2[ip]))
    akv = each(lambda ip: av[ip][:c2].astype(BF16))
    wu = each(lambda ip: _dot(tb[ip], cat1(a2[ip], akv[ip])).astype(BF16))
    pg = each(lambda ip: tn(bh2[ip], wu[ip]))
    gm = each(lambda ip: pg[ip][:, LANES:] + tn(kh2[ip], v2[ip]))
    qy = each(lambda ip: _dot(a_rb[ip], wu[ip]))
    qp = each(lambda ip: cat0((r2[ip].astype(F32) + qy[ip][:, :LANES]).astype(BF16),
                              pg[ip][:, :LANES].astype(BF16)))
    y0 = each(lambda ip: qy[ip][:, LANES:] + av[ip][c2:])
    assert c2 == LANES
    g_col = each(lambda ip: jnp.sum(eye * g_tot[ip[0]][:, sls[ip[1]]], axis=1, keepdims=True))

    state = [h_ref[p] for p in range(npairs)]
    ys = [[] for _ in range(npairs)]
    for i in range(nc):
        for p in range(npairs):
            qh = _dot(qp[i, p], state[p].astype(BF16))
            y2 = qh[:c2] + y0[i, p]
            ys[p].append(y2[:c] + y2[c:])
            state[p] = g_col[i, p] * state[p] + qh[c2:] + gm[i, p]
    for p in range(npairs):
        h_ref[p] = state[p]

    for p in range(npairs):
        sl = sls[p]
        y = cat0(*ys[p])
        mean = _split_dot_k2(y, ones_bd2) * (1.0 / RWKV_HEAD)
        yc = y - mean
        var = _split_dot_k2(yc * yc, ones_bd2) * (1.0 / RWKV_HEAD)
        yn = yc * lax.rsqrt(var + GN_EPS) * gnw_ref[:, sl] + gnb_ref[:, sl]
        o_ref[0, :, sl] = ((yn + bonus_ref[0, :, sl]) * gate_ref[0, :, sl]).astype(o_ref.dtype)


def _rwkv(r, ld, k, v, kk, iclr, bonus, gate, gn_w, gn_b, *, chunk, tb):
    b, t, w = r.shape
    tile = pl.BlockSpec((1, tb, w), lambda i, j: (i, j, 0))
    const = pl.BlockSpec((1, w), lambda i, j: (0, 0))
    return pl.pallas_call(
        functools.partial(_rwkv_body, chunk=chunk),
        grid=(b, t // tb),
        in_specs=[tile] * 8 + [const] * 2,
        out_specs=tile,
        out_shape=jax.ShapeDtypeStruct((b, t, w), BF16),
        scratch_shapes=[pltpu.VMEM((RWKV_HEADS // 2, LANES, LANES), F32)],
        compiler_params=pltpu.CompilerParams(
            dimension_semantics=("arbitrary", "arbitrary"), vmem_limit_bytes=VMEM_LIMIT),
        name="rwkv",
    )(r, ld, k, v, kk, iclr, bonus, gate, gn_w.reshape(1, w), gn_b.reshape(1, w))


def _attn_body(lq1_ref, lk1_ref, lq2_ref, lk2_ref, q_ref, k_ref, vt_ref, sw_ref, o_ref, acc_ref,
               s_ref, m_ref, *, tq, tk, wq, lambda_init):
    assert tq == 2 * tk
    nq = q_ref.shape[1] // tq
    lane = lax.broadcasted_iota(jnp.int32, (1, LANES), 1)
    lam = (jnp.exp(jnp.sum(lq1_ref[...] * lk1_ref[...], axis=-1, keepdims=True))
           - jnp.exp(jnp.sum(lq2_ref[...] * lk2_ref[...], axis=-1, keepdims=True)) + lambda_init)
    strips = [(c, w) for c in range(2) for w in range(tq // wq)]
    items = [(qi, j) for qi in range(nq) for j in range(2 * qi + 2)]
    ones_rows = jnp.ones((BF16_ROWS, tk), BF16)
    qs = {}

    def masked_q(qi):
        if qi not in qs:
            q = q_ref[0, qi * tq:(qi + 1) * tq]
            zero = jnp.zeros_like(q)
            qs[qi] = (jnp.where(lane < DIFF_QK, q, zero), jnp.where(lane < DIFF_QK, zero, q))
        return qs[qi]

    def scores_into(slot, qi, j):
        q_from = max(0, j * tk - qi * tq)
        ks = k_ref[0, j * tk:(j + 1) * tk, :]
        for c in range(2):
            s_ref[slot, c, :, q_from:] = lax.dot_general(ks, masked_q(qi)[c][q_from:], _NT,
                                                         preferred_element_type=F32)

    def consume(slot, qi, j):
        off = j * tk - qi * tq
        for c, w in strips:
            nk = min(tk, max(0, (w + 1) * wq - off))
            if nk == 0:
                continue
            cols = slice(w * wq, (w + 1) * wq)
            s = s_ref[slot, c, 0:nk, cols]
            if off + nk - 1 > w * wq:
                key = lax.broadcasted_iota(jnp.int32, (nk, wq), 0) + off
                qry = lax.broadcasted_iota(jnp.int32, (nk, wq), 1) + w * wq
                s = jnp.where(key <= qry, s, NEG_BIG)
            m = m_ref[qi, c, :, cols]
            m_new = jnp.maximum(m, jnp.max(s, axis=0, keepdims=True))
            alpha = jnp.exp2(m - m_new)
            pr = jnp.exp2(s - m_new)
            m_ref[qi, c, :, cols] = m_new
            vt = jnp.concatenate([vt_ref[0, :, j * tk:j * tk + nk], ones_rows[:, :nk]], axis=0)
            acc_ref[qi, c, :, cols] = alpha * acc_ref[qi, c, :, cols] + _dot(vt, pr.astype(BF16))

    def finish(qi):
        l1, l2 = (acc_ref[qi, c, DIFF_V:DIFF_V + 1] for c in range(2))
        o = (acc_ref[qi, 0, :DIFF_V] * (1.0 / l1)
             - lam * (acc_ref[qi, 1, :DIFF_V] * (1.0 / l2)))
        o = o * lax.rsqrt(jnp.mean(o * o, axis=0, keepdims=True) + SUBLN_EPS) * sw_ref[...]
        o = o * (1.0 - lambda_init)
        o_ref[0, qi * tq:(qi + 1) * tq] = o.T.astype(o_ref.dtype)

    acc_ref[...] = jnp.zeros_like(acc_ref)
    m_ref[...] = jnp.full_like(m_ref, NEG_BIG)
    scores_into(0, *items[0])
    for n, (qi, j) in enumerate(items):
        if n + 1 < len(items):
            scores_into((n + 1) % 2, *items[n + 1])
        consume(n % 2, qi, j)
        if j == 2 * qi + 1:
            finish(qi)


def _attn(q, k, vt, lq1, lk1, lq2, lk2, subln_w, *, tq, wq, lambda_init):
    tk = tq // 2
    b, t, w = q.shape
    heads = w // DIFF_V
    nq = t // tq
    lam_spec = pl.BlockSpec((1, DIFF_QK), lambda i, h: (0, 0))
    return pl.pallas_call(
        functools.partial(_attn_body, tq=tq, tk=tk, wq=wq, lambda_init=lambda_init),
        grid=(b, heads),
        in_specs=[lam_spec] * 4 + [
            pl.BlockSpec((1, t, DIFF_V), lambda i, h: (i, 0, h)),
            pl.BlockSpec((1, t, DIFF_V), lambda i, h: (i, 0, h)),
            pl.BlockSpec((1, DIFF_V, t), lambda i, h: (i, h, 0)),
            pl.BlockSpec((DIFF_V, 1), lambda i, h: (0, 0)),
        ],
        out_specs=pl.BlockSpec((1, t, DIFF_V), lambda i, h: (i, 0, h)),
        out_shape=jax.ShapeDtypeStruct((b, t, w), BF16),
        scratch_shapes=[pltpu.VMEM((nq, 2, DIFF_V + BF16_ROWS, tq), F32),
                        pltpu.VMEM((2, 2, tk, tq), F32), pltpu.VMEM((nq, 2, 1, tq), F32)],
        compiler_params=pltpu.CompilerParams(
            dimension_semantics=("parallel", "parallel"), vmem_limit_bytes=VMEM_LIMIT),
        name="attn",
    )(lq1.reshape(1, -1), lk1.reshape(1, -1), lq2.reshape(1, -1), lk2.reshape(1, -1),
      q, k, vt, subln_w.reshape(-1, 1))


def _pick(n, pref):
    return pref if n % pref == 0 else n


def _layer(x, l, p, *, chunk=64, tb_rwkv=256, tm_ffn=1024, tf=256, tm_proj=512, tq=1024,
           wq_attn=256):
    b, t, d = x.shape
    m = b * t
    lambda_init = 0.8 - 0.6 * math.exp(-0.3 * l)
    tm_ffn = _pick(m, tm_ffn)
    tf = _pick(p["ffn1_w_gate"].shape[-1], tf)
    x = _ffn(x.reshape(m, d), p["ffn1_pre_g"][l], p["ffn1_post_g"][l], p["ffn1_w_gate"][l],
             p["ffn1_w_up"][l], p["ffn1_w_down"][l], tm=tm_ffn, tf=tf)
    outs = _proj(x.reshape(b, t, d), p["mix_pre_g"][l], p["w_in"][l], p["shift_mu"][l],
                 p["rwkv_w_up"][l], p["rwkv_a_up"][l], p["rwkv_g_up"][l], p["rwkv_w0"][l],
                 p["rwkv_a0"][l], p["rwkv_k_k"][l], p["rwkv_k_a"][l], p["rwkv_r_k"][l].reshape(-1),
                 tm=_pick(t, tm_proj))
    y_rwkv = _rwkv(*outs[:8], p["rwkv_gn_w"][l], p["rwkv_gn_b"][l], chunk=chunk,
                   tb=_pick(t, tb_rwkv))
    y_diff = _attn(*outs[8:], p["diff_lam_q1"][l], p["diff_lam_k1"][l], p["diff_lam_q2"][l],
                   p["diff_lam_k2"][l], p["diff_subln_w"][l], tq=_pick(t, tq), wq=wq_attn,
                   lambda_init=lambda_init)
    x = _ffn(x, p["ffn2_pre_g"][l], p["ffn2_post_g"][l], p["ffn2_w_gate"][l], p["ffn2_w_up"][l],
             p["ffn2_w_down"][l], tm=tm_ffn, tf=tf,
             mix=(y_rwkv.reshape(m, -1), y_diff.reshape(m, -1), p["w_o"][l], p["mix_post_g"][l]))
    return x.reshape(b, t, d)


def kernel(x, ffn1_pre_g, ffn1_post_g, ffn1_w_gate, ffn1_w_up, ffn1_w_down, mix_pre_g, mix_post_g,
           w_in, shift_mu, w_o, rwkv_w0, rwkv_w_up, rwkv_a0, rwkv_a_up, rwkv_g_up, rwkv_k_k,
           rwkv_k_a, rwkv_r_k, rwkv_gn_w, rwkv_gn_b, diff_lam_q1, diff_lam_k1, diff_lam_q2,
           diff_lam_k2, diff_subln_w, ffn2_pre_g, ffn2_post_g, ffn2_w_gate, ffn2_w_up, ffn2_w_down):
    p = dict(ffn1_pre_g=ffn1_pre_g, ffn1_post_g=ffn1_post_g, ffn1_w_gate=ffn1_w_gate,
             ffn1_w_up=ffn1_w_up, ffn1_w_down=ffn1_w_down, mix_pre_g=mix_pre_g,
             mix_post_g=mix_post_g, w_in=w_in, shift_mu=shift_mu, w_o=w_o, rwkv_w0=rwkv_w0,
             rwkv_w_up=rwkv_w_up, rwkv_a0=rwkv_a0, rwkv_a_up=rwkv_a_up, rwkv_g_up=rwkv_g_up,
             rwkv_k_k=rwkv_k_k, rwkv_k_a=rwkv_k_a, rwkv_r_k=rwkv_r_k, rwkv_gn_w=rwkv_gn_w,
             rwkv_gn_b=rwkv_gn_b, diff_lam_q1=diff_lam_q1, diff_lam_k1=diff_lam_k1,
             diff_lam_q2=diff_lam_q2, diff_lam_k2=diff_lam_k2, diff_subln_w=diff_subln_w,
             ffn2_pre_g=ffn2_pre_g, ffn2_post_g=ffn2_post_g, ffn2_w_gate=ffn2_w_gate,
             ffn2_w_up=ffn2_w_up, ffn2_w_down=ffn2_w_down)
    for l in range(ffn1_pre_g.shape[0]):
        x = _layer(x, l, p)
    return x
```

```python
import functools
import math

import jax
import jax.numpy as jnp
from jax import lax
from jax.experimental import pallas as pl
from jax.experimental.pallas import tpu as pltpu

F32 = jnp.float32
BF16 = jnp.bfloat16

RWKV_HEADS = 8
RWKV_HEAD = 64
RWKV_W = RWKV_HEADS * RWKV_HEAD
DIFF_HEADS = 4
DIFF_QK = 64
DIFF_V = 2 * DIFF_QK
DIFF_W = DIFF_HEADS * DIFF_V
DECAY_LORA = 64
ICLR_LORA = 64
GATE_LORA = 160
LORA_W = DECAY_LORA + ICLR_LORA + GATE_LORA
LANES = 128
BF16_ROWS = 16
RWKV_GROUP = 4
LORA_PAD = -(-LORA_W // LANES) * LANES
SHIFT_COLS = 3 * RWKV_W + LORA_W
SHIFT_PAD = 3 * RWKV_W + LORA_PAD
FFN_RES = 0.5
NORM_EPS = 1e-6
GN_EPS = 64e-5
SUBLN_EPS = 1e-5
NEG_BIG = -1e30
LOG2E = math.log2(math.e)
VMEM_LIMIT = 56 * 1024 * 1024

_NT = (((1,), (1,)), ((), ()))
_TN = (((0,), (0,)), ((), ()))


def _dot(a, b):
    return jnp.dot(a, b, preferred_element_type=F32)


def _split_dot_k2(x, w2_bf16):
    hi = x.astype(BF16)
    lo = (x - hi.astype(F32)).astype(BF16)
    return _dot(jnp.concatenate([hi, lo], axis=1), w2_bf16)


def _rms(x, g, eps):
    return x * lax.rsqrt(jnp.mean(x * x, axis=-1, keepdims=True) + eps) * g


def _ffn_body(*refs, tf, mix):
    if mix:
        x_ref, yr_ref, yd_ref, wo_ref, gmix_ref, *refs = refs
    else:
        x_ref, *refs = refs
    gpre_ref, gpost_ref, wg_ref, wu_ref, wd_ref, o_ref, a_ref = refs
    x = x_ref[...]
    if mix:
        nr = yr_ref.shape[1]
        y = _dot(yr_ref[...], wo_ref[:nr]) + _dot(yd_ref[...], wo_ref[nr:])
        x = x + _rms(y, gmix_ref[...], NORM_EPS)
    h = _rms(x, gpre_ref[...], NORM_EPS).astype(BF16)
    for f0 in range(0, wg_ref.shape[1], tf):
        cols = slice(f0, f0 + tf)
        g = _dot(h, wg_ref[:, cols])
        u = _dot(h, wu_ref[:, cols])
        a_ref[:, cols] = (g * jax.nn.sigmoid(g) * u).astype(BF16)
    y = _dot(a_ref[...], wd_ref[...])
    o_ref[...] = x + FFN_RES * _rms(y, gpost_ref[...], NORM_EPS)


def _ffn(x, g_pre, g_post, w_gate, w_up, w_down, *, tm, tf, mix=None):
    m, d = x.shape
    f = w_gate.shape[1]
    rows = lambda width: pl.BlockSpec((tm, width), lambda i: (i, 0))
    resident = lambda shape: pl.BlockSpec(shape, lambda i: (0, 0), pipeline_mode=pl.Buffered(1))
    args, specs = [x], [rows(d)]
    if mix is not None:
        y_rwkv, y_diff, w_o, g_mix = mix
        args += [y_rwkv, y_diff, w_o.astype(BF16), g_mix.reshape(1, d)]
        specs += [rows(y_rwkv.shape[1]), rows(y_diff.shape[1]), resident(w_o.shape), resident((1, d))]
    args += [g_pre.reshape(1, d), g_post.reshape(1, d), w_gate.astype(BF16), w_up.astype(BF16),
             w_down.astype(BF16)]
    specs += [resident((1, d)), resident((1, d)), resident((d, f)), resident((d, f)),
              resident((f, d))]
    return pl.pallas_call(
        functools.partial(_ffn_body, tf=tf, mix=mix is not None),
        grid=(m // tm,),
        in_specs=specs,
        out_specs=rows(d),
        out_shape=jax.ShapeDtypeStruct((m, d), F32),
        scratch_shapes=[pltpu.VMEM((tm, f), BF16)],
        compiler_params=pltpu.CompilerParams(
            dimension_semantics=("parallel",), vmem_limit_bytes=VMEM_LIMIT),
        name="ffn_mix" if mix is not None else "ffn",
    )(*args)


def _proj_body(x_ref, g_ref, wrkv_ref, wlora_ref, wdiff_ref, mu_ref, wup_ref, ones_ref, w0_ref,
               a0_ref, kk_ref, ka_ref, rk_ref,
               at_out, rt_out, bt_out, kt_out, bh_out, kh_out, v_out, gt_out, bonus_out, gate_out,
               qd_out, kd_out, vd_out, carry_ref, *, chunk):
    @pl.when(pl.program_id(1) == 0)
    def _():
        carry_ref[...] = jnp.zeros_like(carry_ref)

    tm = x_ref.shape[1]
    nrkv = 3 * RWKV_W
    first_row = lax.broadcasted_iota(jnp.int32, (tm, 1), 0) == 0
    lane = lax.broadcasted_iota(jnp.int32, (1, LORA_PAD), 1)
    ones_bd = ones_ref[...]

    def shift_mix(p, cols):
        prev = jnp.where(first_row, carry_ref[:, cols], pltpu.roll(p, 1, 0))
        carry_ref[:, cols] = p[tm - 1:]
        return p + (prev - p) * mu_ref[:, cols]

    h = _rms(x_ref[0], g_ref[...], NORM_EPS).astype(BF16)
    z = shift_mix(_dot(h, wlora_ref[...]), slice(nrkv, SHIFT_PAD))
    p_k = shift_mix(_dot(h, wrkv_ref[:, RWKV_W:2 * RWKV_W]), slice(RWKV_W, 2 * RWKV_W))
    act = jnp.where(lane < DECAY_LORA, jnp.tanh(z),
                    jnp.where(lane < DECAY_LORA + ICLR_LORA, z, jax.nn.sigmoid(z))).astype(BF16)
    up = _dot(act[:, :LANES], wup_ref[:LANES, :2 * RWKV_W])
    gate_out[0] = _dot(act[:, LANES:], wup_ref[LANES:, 2 * RWKV_W:])
    p_r = shift_mix(_dot(h, wrkv_ref[:, :RWKV_W]), slice(0, RWKV_W))
    p_v = shift_mix(_dot(h, wrkv_ref[:, 2 * RWKV_W:]), slice(2 * RWKV_W, nrkv))
    p_diff = _dot(h, wdiff_ref[...])

    zw = -(w0_ref[...] + up[:, :RWKV_W])
    softplus = jnp.maximum(zw, 0.0) + jnp.log(1.0 + jnp.exp(-jnp.abs(zw)))
    ld = -jnp.exp(-softplus - 0.5)
    iclr = jax.nn.sigmoid(a0_ref[...] + up[:, RWKV_W:])
    kk = p_k * kk_ref[...]
    kk = kk * lax.rsqrt(jnp.maximum(_dot((kk * kk).astype(BF16), ones_bd), 1e-24))
    k = p_k * (1.0 + (iclr - 1.0) * ka_ref[...])
    b = kk * iclr
    v_out[0] = p_v.astype(BF16)
    bonus_out[0] = _dot((p_r * k * rk_ref[...]).astype(BF16), ones_bd) * p_v

    ri = lax.broadcasted_iota(jnp.int32, (chunk, chunk), 0)
    ci = lax.broadcasted_iota(jnp.int32, (chunk, chunk), 1)
    tril3 = jnp.concatenate([(ci <= ri).astype(BF16)] * 3, axis=1)
    ld1 = ld.astype(BF16)
    res = ld - ld1.astype(F32)
    ld2 = res.astype(BF16)
    ld3 = (res - ld2.astype(F32)).astype(BF16)
    cums, tots = [], []
    for i in range(tm // chunk):
        rows = slice(i * chunk, (i + 1) * chunk)
        cum = _dot(tril3, jnp.concatenate([ld1[rows], ld2[rows], ld3[rows]], axis=0))
        cums.append(cum)
        tots.append(jnp.broadcast_to(cum[chunk - 1:], cum.shape))
    gt_out[0] = jnp.exp(jnp.concatenate([t[:1] for t in tots], axis=0))
    cum = jnp.concatenate(cums, axis=0)
    g_inv = jnp.exp(-cum)
    g_rest = jnp.exp(jnp.concatenate(tots, axis=0) - cum)
    at_out[0] = (-kk * jnp.exp(cum - ld)).astype(BF16)
    rt_out[0] = (p_r * jnp.exp(cum)).astype(BF16)
    bt_out[0] = (b * g_inv).astype(BF16)
    kt_out[0] = (k * g_inv).astype(BF16)
    bh_out[0] = (b * g_rest).astype(BF16)
    kh_out[0] = (k * g_rest).astype(BF16)

    qd_out[0] = (p_diff[:, :DIFF_W] * (DIFF_QK ** -0.5 * LOG2E)).astype(BF16)
    kd_out[0] = p_diff[:, DIFF_W:2 * DIFF_W].astype(BF16)
    vd_out[0] = p_diff[:, 2 * DIFF_W:].T.astype(BF16)


def _head_ones(width, head):
    i = jnp.arange(width) // head
    return (i[:, None] == i[None, :]).astype(BF16)


def _proj(x, g, w_in, shift_mu, w_up, a_up, g_up, w0, a0, k_k, k_a, r_k, *, tm, chunk):
    b, t, d = x.shape
    pad = SHIFT_PAD - SHIFT_COLS
    nrkv = 3 * RWKV_W
    w = w_in.astype(BF16)
    w_lora = jnp.pad(w[:, nrkv:SHIFT_COLS], ((0, 0), (0, pad)))
    w_diff = w[:, SHIFT_COLS:]
    mu = jnp.pad(shift_mu, (0, pad)).reshape(1, SHIFT_PAD)
    wup = jnp.zeros((LORA_PAD, 3 * RWKV_W), F32)
    wup = wup.at[:DECAY_LORA, :RWKV_W].set(w_up)
    wup = wup.at[DECAY_LORA:DECAY_LORA + ICLR_LORA, RWKV_W:2 * RWKV_W].set(a_up)
    wup = wup.at[DECAY_LORA + ICLR_LORA:LORA_W, 2 * RWKV_W:].set(g_up)
    wup = wup.astype(BF16)
    vec = lambda a: a.reshape(1, RWKV_W)
    const = lambda shape: pl.BlockSpec(shape, lambda i, j: (0,) * len(shape))
    tile = lambda width: pl.BlockSpec((1, tm, width), lambda i, j: (i, j, 0))
    f32_out = jax.ShapeDtypeStruct((b, t, RWKV_W), F32)
    bf_out = jax.ShapeDtypeStruct((b, t, DIFF_W), BF16)
    assert RWKV_W == DIFF_W
    return pl.pallas_call(
        functools.partial(_proj_body, chunk=chunk),
        grid=(b, t // tm),
        in_specs=[tile(d), const((1, d)), const((d, nrkv)), const((d, LORA_PAD)),
                  const((d, 3 * DIFF_W)), const((1, SHIFT_PAD)),
                  const((LORA_PAD, 3 * RWKV_W)), const((RWKV_W, RWKV_W))] + [const((1, RWKV_W))] * 5,
        out_specs=[tile(RWKV_W)] * 7 + [pl.BlockSpec((1, tm // chunk, RWKV_W), lambda i, j: (i, j, 0))]
        + [tile(RWKV_W)] * 2 + [tile(DIFF_W)] * 2
        + [pl.BlockSpec((1, DIFF_W, tm), lambda i, j: (i, 0, j))],
        out_shape=[bf_out] * 7 + [jax.ShapeDtypeStruct((b, t // chunk, RWKV_W), F32)]
        + [f32_out] * 2 + [bf_out] * 2 + [jax.ShapeDtypeStruct((b, DIFF_W, t), BF16)],
        scratch_shapes=[pltpu.VMEM((1, SHIFT_PAD), F32)],
        compiler_params=pltpu.CompilerParams(
            dimension_semantics=("arbitrary", "arbitrary"), vmem_limit_bytes=VMEM_LIMIT),
        name="proj",
    )(x, g.reshape(1, d), w, w_lora, w_diff, mu, wup, _head_ones(RWKV_W, RWKV_HEAD),
      vec(w0), vec(a0), vec(k_k), vec(k_a), vec(r_k))


def _rwkv_body(at_ref, rt_ref, bt_ref, kt_ref, bh_ref, kh_ref, v_ref, gt_ref, bonus_ref, gate_ref,
               gnw_ref, gnb_ref, o_ref, h_ref, *, chunk, group_size):
    c = chunk
    c2 = 2 * c
    nc = at_ref.shape[1] // c

    @pl.when(pl.program_id(1) == 0)
    def _():
        h_ref[...] = jnp.zeros_like(h_ref)

    chunk_of = lambda ref: {i: ref[0, i * c:(i + 1) * c, :] for i in range(nc)}
    at, rt, bt, kt, bh, kh, v = (chunk_of(ref) for ref in
                                 (at_ref, rt_ref, bt_ref, kt_ref, bh_ref, kh_ref, v_ref))
    g_tot = {i: gt_ref[0, i:i + 1, :] for i in range(nc)}

    lane = lax.broadcasted_iota(jnp.int32, (1, LANES), 1)
    lo = lane < RWKV_HEAD

    def stack(x):
        zero = jnp.zeros_like(x)
        return jnp.concatenate([jnp.where(lo, x, zero), jnp.where(lo, zero, x)], axis=0)

    row2 = lax.broadcasted_iota(jnp.int32, (c2, c2), 0)
    col2 = lax.broadcasted_iota(jnp.int32, (c2, c2), 1)
    same = (row2 // c) == (col2 // c)
    strict = same & (col2 < row2)
    incl = same & (col2 <= row2)
    eye = (row2 == col2).astype(F32)
    levels = int(math.log2(c))
    level_masks = [((row2 >> l) == (col2 >> l)) & ((row2 >> (l - 1)) != (col2 >> (l - 1)))
                   & (col2 < row2) for l in range(1, levels + 1)]
    ones_bd = ((lax.broadcasted_iota(jnp.int32, (LANES, LANES), 0) // RWKV_HEAD)
               == (lax.broadcasted_iota(jnp.int32, (LANES, LANES), 1) // RWKV_HEAD)).astype(BF16)
    ones_bd2 = jnp.concatenate([ones_bd, ones_bd], axis=0)
    nt = lambda x, y: lax.dot_general(x, y, _NT, preferred_element_type=F32)
    tn = lambda x, y: lax.dot_general(x, y, _TN, preferred_element_type=F32)

    npairs = RWKV_HEADS // 2
    sls = [slice(p * LANES, (p + 1) * LANES) for p in range(npairs)]
    cat0 = lambda *xs: jnp.concatenate(xs, axis=0)
    cat1 = lambda *xs: jnp.concatenate(xs, axis=1)
    assert c2 == LANES
    qp, y0, gm, g_col = {}, {}, {}, {}

    for group in [range(g, min(g + group_size, nc)) for g in range(0, nc, group_size)]:
        chains = [(i, p) for i in group for p in range(npairs)]
        each = lambda f: {ip: f(ip) for ip in chains}
        pick = lambda xs: each(lambda ip: stack(xs[ip[0]][:, sls[ip[1]]]))
        a2, r2, b2, k2, bh2, kh2, v2 = (pick(xs) for xs in (at, rt, bt, kt, bh, kh, v))

        s_ar = each(lambda ip: nt(cat0(a2[ip], r2[ip]), cat0(b2[ip], k2[ip])))
        a_ab = each(lambda ip: jnp.where(strict, s_ar[ip][:c2, :c2], 0.0))
        a_ak = each(lambda ip: jnp.where(strict, s_ar[ip][:c2, c2:], 0.0).astype(BF16))
        a_rb = each(lambda ip: jnp.where(incl, s_ar[ip][c2:, :c2], 0.0).astype(BF16))
        a_rk = each(lambda ip: jnp.where(incl, s_ar[ip][c2:, c2:], 0.0).astype(BF16))

        tinv = each(lambda ip: eye + jnp.where(level_masks[0], a_ab[ip], 0.0))
        for l in range(2, levels + 1):
            e = each(lambda ip: jnp.where(level_masks[l - 1], a_ab[ip], 0.0).astype(BF16))
            tb = each(lambda ip: tinv[ip].astype(BF16))
            te = each(lambda ip: _dot(tb[ip], e[ip]).astype(BF16))
            tinv = each(lambda ip: tinv[ip] + _dot(te[ip], tb[ip]))
        tb = each(lambda ip: tinv[ip].astype(BF16))

        av = each(lambda ip: _dot(cat0(a_ak[ip], a_rk[ip]), v2[ip]))
        akv = each(lambda ip: av[ip][:c2].astype(BF16))
        wu = each(lambda ip: _dot(tb[ip], cat1(a2[ip], akv[ip])).astype(BF16))
        pg = each(lambda ip: tn(bh2[ip], wu[ip]))
        qy = each(lambda ip: _dot(a_rb[ip], wu[ip]))
        gm.update(each(lambda ip: pg[ip][:, LANES:] + tn(kh2[ip], v2[ip])))
        qp.update(each(lambda ip: cat0((r2[ip].astype(F32) + qy[ip][:, :LANES]).astype(BF16),
                                       pg[ip][:, :LANES].astype(BF16))))
        y0.update(each(lambda ip: qy[ip][:, LANES:] + av[ip][c2:]))
        g_col.update(each(
            lambda ip: jnp.sum(eye * g_tot[ip[0]][:, sls[ip[1]]], axis=1, keepdims=True)))

    state = [h_ref[p] for p in range(npairs)]
    ys = [[] for _ in range(npairs)]
    for i in range(nc):
        for p in range(npairs):
            qh = _dot(qp[i, p], state[p].astype(BF16))
            y2 = qh[:c2] + y0[i, p]
            ys[p].append(y2[:c] + y2[c:])
            state[p] = g_col[i, p] * state[p] + qh[c2:] + gm[i, p]
    for p in range(npairs):
        h_ref[p] = state[p]

    for p in range(npairs):
        sl = sls[p]
        y = cat0(*ys[p])
        mean = _split_dot_k2(y, ones_bd2) * (1.0 / RWKV_HEAD)
        yc = y - mean
        var = _split_dot_k2(yc * yc, ones_bd2) * (1.0 / RWKV_HEAD)
        yn = yc * lax.rsqrt(var + GN_EPS) * gnw_ref[:, sl] + gnb_ref[:, sl]
        o_ref[0, :, sl] = ((yn + bonus_ref[0, :, sl]) * gate_ref[0, :, sl]).astype(o_ref.dtype)


def _rwkv(at, rt, bt, kt, bh, kh, v, gt, bonus, gate, gn_w, gn_b, *, chunk, tb):
    b, t, w = at.shape
    tile = pl.BlockSpec((1, tb, w), lambda i, j: (i, j, 0))
    const = pl.BlockSpec((1, w), lambda i, j: (0, 0))
    return pl.pallas_call(
        functools.partial(_rwkv_body, chunk=chunk, group_size=RWKV_GROUP),
        grid=(b, t // tb),
        in_specs=[tile] * 7 + [pl.BlockSpec((1, tb // chunk, w), lambda i, j: (i, j, 0))]
        + [tile] * 2 + [const] * 2,
        out_specs=tile,
        out_shape=jax.ShapeDtypeStruct((b, t, w), BF16),
        scratch_shapes=[pltpu.VMEM((RWKV_HEADS // 2, LANES, LANES), F32)],
        compiler_params=pltpu.CompilerParams(
            dimension_semantics=("arbitrary", "arbitrary"), vmem_limit_bytes=VMEM_LIMIT),
        name="rwkv",
    )(at, rt, bt, kt, bh, kh, v, gt, bonus, gate, gn_w.reshape(1, w), gn_b.reshape(1, w))


def _attn_body(lq1_ref, lk1_ref, lq2_ref, lk2_ref, q_ref, k_ref, vt_ref, sw_ref, o_ref, acc_ref,
               s_ref, m_ref, *, tq, tk, wq, lambda_init):
    nq = q_ref.shape[1] // tq
    kpq = tq // tk
    lane = lax.broadcasted_iota(jnp.int32, (1, LANES), 1)
    lam = (jnp.exp(jnp.sum(lq1_ref[...] * lk1_ref[...], axis=-1, keepdims=True))
           - jnp.exp(jnp.sum(lq2_ref[...] * lk2_ref[...], axis=-1, keepdims=True)) + lambda_init)
    strips = [(c, w) for c in range(2) for w in range(tq // wq)]
    items = [(qi, j) for qi in range(nq) for j in range(kpq * (qi + 1))]
    ones_rows = jnp.ones((BF16_ROWS, tk), BF16)
    qs = {}

    def masked_q(qi):
        if qi not in qs:
            q = q_ref[0, qi * tq:(qi + 1) * tq]
            zero = jnp.zeros_like(q)
            qs[qi] = (jnp.where(lane < DIFF_QK, q, zero), jnp.where(lane < DIFF_QK, zero, q))
        return qs[qi]

    def scores_into(slot, qi, j):
        q_from = max(0, j * tk - qi * tq)
        ks = k_ref[0, j * tk:(j + 1) * tk, :]
        for c in range(2):
            s_ref[slot, c, :, q_from:] = lax.dot_general(ks, masked_q(qi)[c][q_from:], _NT,
                                                         preferred_element_type=F32)

    def consume(slot, qi, j):
        off = j * tk - qi * tq
        for c, w in strips:
            nk = min(tk, max(0, (w + 1) * wq - off))
            if nk == 0:
                continue
            cols = slice(w * wq, (w + 1) * wq)
            s = s_ref[slot, c, 0:nk, cols]
            if off + nk - 1 > w * wq:
                key = lax.broadcasted_iota(jnp.int32, (nk, wq), 0) + off
                qry = lax.broadcasted_iota(jnp.int32, (nk, wq), 1) + w * wq
                s = jnp.where(key <= qry, s, NEG_BIG)
            m = m_ref[qi, c, :, cols]
            m_new = jnp.maximum(m, jnp.max(s, axis=0, keepdims=True))
            alpha = jnp.exp2(m - m_new)
            pr = jnp.exp2(s - m_new)
            m_ref[qi, c, :, cols] = m_new
            vt = jnp.concatenate([vt_ref[0, :, j * tk:j * tk + nk], ones_rows[:, :nk]], axis=0)
            acc_ref[qi, c, :, cols] = alpha * acc_ref[qi, c, :, cols] + _dot(vt, pr.astype(BF16))

    def finish(qi):
        l1, l2 = (acc_ref[qi, c, DIFF_V:DIFF_V + 1] for c in range(2))
        o = (acc_ref[qi, 0, :DIFF_V] * (1.0 / l1)
             - lam * (acc_ref[qi, 1, :DIFF_V] * (1.0 / l2)))
        o = o * lax.rsqrt(jnp.mean(o * o, axis=0, keepdims=True) + SUBLN_EPS) * sw_ref[...]
        o = o * (1.0 - lambda_init)
        o_ref[0, qi * tq:(qi + 1) * tq] = o.T.astype(o_ref.dtype)

    acc_ref[...] = jnp.zeros_like(acc_ref)
    m_ref[...] = jnp.full_like(m_ref, NEG_BIG)
    scores_into(0, *items[0])
    for n, (qi, j) in enumerate(items):
        if n + 1 < len(items):
            scores_into((n + 1) % 2, *items[n + 1])
        consume(n % 2, qi, j)
        if j == kpq * (qi + 1) - 1:
            finish(qi)


def _attn(q, k, vt, lq1, lk1, lq2, lk2, subln_w, *, tq, tk, wq, lambda_init):
    b, t, w = q.shape
    heads = w // DIFF_V
    nq = t // tq
    lam_spec = pl.BlockSpec((1, DIFF_QK), lambda i, h: (0, 0))
    return pl.pallas_call(
        functools.partial(_attn_body, tq=tq, tk=tk, wq=wq, lambda_init=lambda_init),
        grid=(b, heads),
        in_specs=[lam_spec] * 4 + [
            pl.BlockSpec((1, t, DIFF_V), lambda i, h: (i, 0, h)),
            pl.BlockSpec((1, t, DIFF_V), lambda i, h: (i, 0, h)),
            pl.BlockSpec((1, DIFF_V, t), lambda i, h: (i, h, 0)),
            pl.BlockSpec((DIFF_V, 1), lambda i, h: (0, 0)),
        ],
        out_specs=pl.BlockSpec((1, t, DIFF_V), lambda i, h: (i, 0, h)),
        out_shape=jax.ShapeDtypeStruct((b, t, w), BF16),
        scratch_shapes=[pltpu.VMEM((nq, 2, DIFF_V + BF16_ROWS, tq), F32),
                        pltpu.VMEM((2, 2, tk, tq), F32), pltpu.VMEM((nq, 2, 1, tq), F32)],
        compiler_params=pltpu.CompilerParams(
            dimension_semantics=("parallel", "parallel"), vmem_limit_bytes=VMEM_LIMIT),
        name="attn",
    )(lq1.reshape(1, -1), lk1.reshape(1, -1), lq2.reshape(1, -1), lk2.reshape(1, -1),
      q, k, vt, subln_w.reshape(-1, 1))


def _pick(n, pref):
    return pref if n % pref == 0 else n


def _layer(x, l, p, *, chunk=64, tb_rwkv=512, tm_ffn=1024, tf=256, tm_proj=512, tq=1024,
           tk=512, wq_attn=256):
    b, t, d = x.shape
    m = b * t
    lambda_init = 0.8 - 0.6 * math.exp(-0.3 * l)
    tm_ffn = _pick(m, tm_ffn)
    tf = _pick(p["ffn1_w_gate"].shape[-1], tf)
    x = _ffn(x.reshape(m, d), p["ffn1_pre_g"][l], p["ffn1_post_g"][l], p["ffn1_w_gate"][l],
             p["ffn1_w_up"][l], p["ffn1_w_down"][l], tm=tm_ffn, tf=tf)
    outs = _proj(x.reshape(b, t, d), p["mix_pre_g"][l], p["w_in"][l], p["shift_mu"][l],
                 p["rwkv_w_up"][l], p["rwkv_a_up"][l], p["rwkv_g_up"][l], p["rwkv_w0"][l],
                 p["rwkv_a0"][l], p["rwkv_k_k"][l], p["rwkv_k_a"][l], p["rwkv_r_k"][l].reshape(-1),
                 tm=_pick(t, tm_proj), chunk=chunk)
    y_rwkv = _rwkv(*outs[:10], p["rwkv_gn_w"][l], p["rwkv_gn_b"][l], chunk=chunk,
                   tb=_pick(t, tb_rwkv))
    y_diff = _attn(*outs[10:], p["diff_lam_q1"][l], p["diff_lam_k1"][l], p["diff_lam_q2"][l],
                   p["diff_lam_k2"][l], p["diff_subln_w"][l], tq=_pick(t, tq), tk=_pick(t, tk), wq=wq_attn,
                   lambda_init=lambda_init)
    x = _ffn(x, p["ffn2_pre_g"][l], p["ffn2_post_g"][l], p["ffn2_w_gate"][l], p["ffn2_w_up"][l],
             p["ffn2_w_down"][l], tm=tm_ffn, tf=tf,
             mix=(y_rwkv.reshape(m, -1), y_diff.reshape(m, -1), p["w_o"][l], p["mix_post_g"][l]))
    return x.reshape(b, t, d)


def kernel(x, ffn1_pre_g, ffn1_post_g, ffn1_w_gate, ffn1_w_up, ffn1_w_down, mix_pre_g, mix_post_g,
           w_in, shift_mu, w_o, rwkv_w0, rwkv_w_up, rwkv_a0, rwkv_a_up, rwkv_g_up, rwkv_k_k,
           rwkv_k_a, rwkv_r_k, rwkv_gn_w, rwkv_gn_b, diff_lam_q1, diff_lam_k1, diff_lam_q2,
           diff_lam_k2, diff_subln_w, ffn2_pre_g, ffn2_post_g, ffn2_w_gate, ffn2_w_up, ffn2_w_down):
    p = dict(ffn1_pre_g=ffn1_pre_g, ffn1_post_g=ffn1_post_g, ffn1_w_gate=ffn1_w_gate,
             ffn1_w_up=ffn1_w_up, ffn1_w_down=ffn1_w_down, mix_pre_g=mix_pre_g,
             mix_post_g=mix_post_g, w_in=w_in, shift_mu=shift_mu, w_o=w_o, rwkv_w0=rwkv_w0,
             rwkv_w_up=rwkv_w_up, rwkv_a0=rwkv_a0, rwkv_a_up=rwkv_a_up, rwkv_g_up=rwkv_g_up,
             rwkv_k_k=rwkv_k_k, rwkv_k_a=rwkv_k_a, rwkv_r_k=rwkv_r_k, rwkv_gn_w=rwkv_gn_w,
             rwkv_gn_b=rwkv_gn_b, diff_lam_q1=diff_lam_q1, diff_lam_k1=diff_lam_k1,
             diff_lam_q2=diff_lam_q2, diff_lam_k2=diff_lam_k2, diff_subln_w=diff_subln_w,
             ffn2_pre_g=ffn2_pre_g, ffn2_post_g=ffn2_post_g, ffn2_w_gate=ffn2_w_gate,
             ffn2_w_up=ffn2_w_up, ffn2_w_down=ffn2_w_down)
    for l in range(ffn1_pre_g.shape[0]):
        x = _layer(x, l, p)
    return x
```

```python
import functools
import math

import jax
import jax.numpy as jnp
from jax import lax
from jax.experimental import pallas as pl
from jax.experimental.pallas import tpu as pltpu

F32 = jnp.float32
BF16 = jnp.bfloat16

RWKV_HEADS = 8
RWKV_HEAD = 64
RWKV_W = RWKV_HEADS * RWKV_HEAD
DIFF_HEADS = 4
DIFF_QK = 64
DIFF_V = 2 * DIFF_QK
DIFF_W = DIFF_HEADS * DIFF_V
DECAY_LORA = 64
ICLR_LORA = 64
GATE_LORA = 160
LORA_W = DECAY_LORA + ICLR_LORA + GATE_LORA
LANES = 128
SUBLANES = 8
BF16_ROWS = 16
RWKV_GROUP = 4
LORA_PAD = -(-LORA_W // LANES) * LANES
SHIFT_COLS = 3 * RWKV_W + LORA_W
SHIFT_PAD = 3 * RWKV_W + LORA_PAD
FFN_RES = 0.5
NORM_EPS = 1e-6
GN_EPS = 64e-5
SUBLN_EPS = 1e-5
NEG_BIG = -1e30
LOG2E = math.log2(math.e)
EXP_NEG_HALF = math.exp(-0.5)
VMEM_LIMIT = 56 * 1024 * 1024

_NT = (((1,), (1,)), ((), ()))
_TN = (((0,), (0,)), ((), ()))


def _dot(a, b):
    return jnp.dot(a, b, preferred_element_type=F32)


def _split_dot_k2(x, w2_bf16):
    hi = x.astype(BF16)
    lo = (x - hi.astype(F32)).astype(BF16)
    return _dot(jnp.concatenate([hi, lo], axis=1), w2_bf16)


def _rms(x, g, eps):
    return x * lax.rsqrt(jnp.mean(x * x, axis=-1, keepdims=True) + eps) * g


def _ffn_body(*refs, tf, mix):
    if mix:
        x_ref, yr_ref, yd_ref, wo_ref, gmix_ref, *refs = refs
    else:
        x_ref, *refs = refs
    gpre_ref, gpost_ref, wg_ref, wu_ref, wd_ref, o_ref, a_ref = refs
    x = x_ref[...]
    if mix:
        nr = yr_ref.shape[1]
        y = _dot(yr_ref[...], wo_ref[:nr]) + _dot(yd_ref[...], wo_ref[nr:])
        x = x + _rms(y, gmix_ref[...], NORM_EPS)
    h = _rms(x, gpre_ref[...], NORM_EPS).astype(BF16)
    for f0 in range(0, wg_ref.shape[1], tf):
        cols = slice(f0, f0 + tf)
        g = _dot(h, wg_ref[:, cols])
        u = _dot(h, wu_ref[:, cols])
        a_ref[:, cols] = (g * jax.nn.sigmoid(g) * u).astype(BF16)
    y = _dot(a_ref[...], wd_ref[...])
    o_ref[...] = x + FFN_RES * _rms(y, gpost_ref[...], NORM_EPS)


def _ffn(x, g_pre, g_post, w_gate, w_up, w_down, *, tm, tf, mix=None):
    m, d = x.shape
    f = w_gate.shape[1]
    rows = lambda width: pl.BlockSpec((tm, width), lambda i: (i, 0))
    resident = lambda shape: pl.BlockSpec(shape, lambda i: (0, 0), pipeline_mode=pl.Buffered(1))
    args, specs = [x], [rows(d)]
    if mix is not None:
        y_rwkv, y_diff, w_o, g_mix = mix
        args += [y_rwkv, y_diff, w_o.astype(BF16), g_mix.reshape(1, d)]
        specs += [rows(y_rwkv.shape[1]), rows(y_diff.shape[1]), resident(w_o.shape), resident((1, d))]
    args += [g_pre.reshape(1, d), g_post.reshape(1, d), w_gate.astype(BF16), w_up.astype(BF16),
             w_down.astype(BF16)]
    specs += [resident((1, d)), resident((1, d)), resident((d, f)), resident((d, f)),
              resident((f, d))]
    return pl.pallas_call(
        functools.partial(_ffn_body, tf=tf, mix=mix is not None),
        grid=(m // tm,),
        in_specs=specs,
        out_specs=rows(d),
        out_shape=jax.ShapeDtypeStruct((m, d), F32),
        scratch_shapes=[pltpu.VMEM((tm, f), BF16)],
        compiler_params=pltpu.CompilerParams(
            dimension_semantics=("parallel",), vmem_limit_bytes=VMEM_LIMIT),
        name="ffn_mix" if mix is not None else "ffn",
    )(*args)


def _proj_body(x_ref, g_ref, wrkv_ref, wlora_ref, wdiff_ref, mu_ref, wup_ref, ones_ref, w0_ref,
               a0_ref, kk_ref, ka_ref, rk_ref,
               at_out, rt_out, bt_out, kt_out, bh_out, kh_out, v_out, gt_out, bonus_out, gate_out,
               qd_out, kd_out, vd_out, carry_ref, *, chunk):
    @pl.when(pl.program_id(1) == 0)
    def _():
        carry_ref[...] = jnp.zeros_like(carry_ref)

    tm = x_ref.shape[1]
    nrkv = 3 * RWKV_W
    first_row = lax.broadcasted_iota(jnp.int32, (SUBLANES, 1), 0) == 0
    lane = lax.broadcasted_iota(jnp.int32, (1, LORA_PAD), 1)
    ones_bd = ones_ref[...]

    def shift_mix(p, cols):
        prev = pltpu.roll(p, 1, 0)
        prev = jnp.concatenate(
            [jnp.where(first_row, carry_ref[:, cols], prev[:SUBLANES]), prev[SUBLANES:]], axis=0)
        carry_ref[:, cols] = p[tm - 1:]
        return p + (prev - p) * mu_ref[:, cols]

    h = _rms(x_ref[0], g_ref[...], NORM_EPS).astype(BF16)
    z = shift_mix(_dot(h, wlora_ref[...]), slice(nrkv, SHIFT_PAD))
    p_k = shift_mix(_dot(h, wrkv_ref[:, RWKV_W:2 * RWKV_W]), slice(RWKV_W, 2 * RWKV_W))
    act = jnp.where(lane < DECAY_LORA, jnp.tanh(z),
                    jnp.where(lane < DECAY_LORA + ICLR_LORA, z, jax.nn.sigmoid(z))).astype(BF16)
    up = _dot(act[:, :LANES], wup_ref[:LANES, :2 * RWKV_W])
    gate_out[0] = _dot(act[:, LANES:], wup_ref[LANES:, 2 * RWKV_W:])
    p_r = shift_mix(_dot(h, wrkv_ref[:, :RWKV_W]), slice(0, RWKV_W))
    p_v = shift_mix(_dot(h, wrkv_ref[:, 2 * RWKV_W:]), slice(2 * RWKV_W, nrkv))
    p_diff = _dot(h, wdiff_ref[...])

    ld = -EXP_NEG_HALF * jax.nn.sigmoid(w0_ref[...] + up[:, :RWKV_W])
    iclr = jax.nn.sigmoid(a0_ref[...] + up[:, RWKV_W:])
    kk = p_k * kk_ref[...]
    kk = kk * lax.rsqrt(jnp.maximum(_dot((kk * kk).astype(BF16), ones_bd), 1e-24))
    k = p_k * (iclr * ka_ref[...] + (1.0 - ka_ref[...]))
    b = kk * iclr
    v_out[0] = p_v.astype(BF16)
    bonus_out[0] = _dot((p_r * k * rk_ref[...]).astype(BF16), ones_bd) * p_v

    ri = lax.broadcasted_iota(jnp.int32, (chunk, chunk), 0)
    ci = lax.broadcasted_iota(jnp.int32, (chunk, chunk), 1)
    tril3 = jnp.concatenate([(ci <= ri).astype(BF16)] * 3, axis=1)
    ld1 = ld.astype(BF16)
    res = ld - ld1.astype(F32)
    ld2 = res.astype(BF16)
    ld3 = (res - ld2.astype(F32)).astype(BF16)
    cums, tots = [], []
    for i in range(tm // chunk):
        rows = slice(i * chunk, (i + 1) * chunk)
        cum = _dot(tril3, jnp.concatenate([ld1[rows], ld2[rows], ld3[rows]], axis=0))
        cums.append(cum)
        tots.append(jnp.broadcast_to(cum[chunk - 1:], cum.shape))
    gt_out[0] = jnp.exp(jnp.concatenate([t[:1] for t in tots], axis=0))
    cum = jnp.concatenate(cums, axis=0)
    g_inv = jnp.exp(-cum)
    g_rest = jnp.exp(jnp.concatenate(tots, axis=0) - cum)
    at_out[0] = (-kk * jnp.exp(cum - ld)).astype(BF16)
    rt_out[0] = (p_r * jnp.exp(cum)).astype(BF16)
    bt_out[0] = (b * g_inv).astype(BF16)
    kt_out[0] = (k * g_inv).astype(BF16)
    bh_out[0] = (b * g_rest).astype(BF16)
    kh_out[0] = (k * g_rest).astype(BF16)

    qd_out[0] = (p_diff[:, :DIFF_W] * (DIFF_QK ** -0.5 * LOG2E)).astype(BF16)
    kd_out[0] = p_diff[:, DIFF_W:2 * DIFF_W].astype(BF16)
    vd_out[0] = p_diff[:, 2 * DIFF_W:].T.astype(BF16)


def _head_ones(width, head):
    i = jnp.arange(width) // head
    return (i[:, None] == i[None, :]).astype(BF16)


def _proj(x, g, w_in, shift_mu, w_up, a_up, g_up, w0, a0, k_k, k_a, r_k, *, tm, chunk):
    b, t, d = x.shape
    pad = SHIFT_PAD - SHIFT_COLS
    nrkv = 3 * RWKV_W
    w = w_in.astype(BF16)
    w_lora = jnp.pad(w[:, nrkv:SHIFT_COLS], ((0, 0), (0, pad)))
    w_diff = w[:, SHIFT_COLS:]
    mu = jnp.pad(shift_mu, (0, pad)).reshape(1, SHIFT_PAD)
    wup = jnp.zeros((LORA_PAD, 3 * RWKV_W), F32)
    wup = wup.at[:DECAY_LORA, :RWKV_W].set(w_up)
    wup = wup.at[DECAY_LORA:DECAY_LORA + ICLR_LORA, RWKV_W:2 * RWKV_W].set(a_up)
    wup = wup.at[DECAY_LORA + ICLR_LORA:LORA_W, 2 * RWKV_W:].set(g_up)
    wup = wup.astype(BF16)
    vec = lambda a: a.reshape(1, RWKV_W)
    const = lambda shape: pl.BlockSpec(shape, lambda i, j: (0,) * len(shape))
    tile = lambda width: pl.BlockSpec((1, tm, width), lambda i, j: (i, j, 0))
    f32_out = jax.ShapeDtypeStruct((b, t, RWKV_W), F32)
    bf_out = jax.ShapeDtypeStruct((b, t, DIFF_W), BF16)
    assert RWKV_W == DIFF_W
    return pl.pallas_call(
        functools.partial(_proj_body, chunk=chunk),
        grid=(b, t // tm),
        in_specs=[tile(d), const((1, d)), const((d, nrkv)), const((d, LORA_PAD)),
                  const((d, 3 * DIFF_W)), const((1, SHIFT_PAD)),
                  const((LORA_PAD, 3 * RWKV_W)), const((RWKV_W, RWKV_W))] + [const((1, RWKV_W))] * 5,
        out_specs=[tile(RWKV_W)] * 7 + [pl.BlockSpec((1, tm // chunk, RWKV_W), lambda i, j: (i, j, 0))]
        + [tile(RWKV_W)] * 2 + [tile(DIFF_W)] * 2
        + [pl.BlockSpec((1, DIFF_W, tm), lambda i, j: (i, 0, j))],
        out_shape=[bf_out] * 7 + [jax.ShapeDtypeStruct((b, t // chunk, RWKV_W), F32)]
        + [f32_out] * 2 + [bf_out] * 2 + [jax.ShapeDtypeStruct((b, DIFF_W, t), BF16)],
        scratch_shapes=[pltpu.VMEM((1, SHIFT_PAD), F32)],
        compiler_params=pltpu.CompilerParams(
            dimension_semantics=("arbitrary", "arbitrary"), vmem_limit_bytes=VMEM_LIMIT),
        name="proj",
    )(x, g.reshape(1, d), w, w_lora, w_diff, mu, wup, _head_ones(RWKV_W, RWKV_HEAD),
      vec(w0), vec(a0), vec(k_k), vec(k_a), vec(r_k))


def _rwkv_body(at_ref, rt_ref, bt_ref, kt_ref, bh_ref, kh_ref, v_ref, gt_ref, bonus_ref, gate_ref,
               gnw_ref, gnb_ref, o_ref, h_ref, *, chunk, group_size):
    c = chunk
    c2 = 2 * c
    nc = at_ref.shape[1] // c

    @pl.when(pl.program_id(1) == 0)
    def _():
        h_ref[...] = jnp.zeros_like(h_ref)

    chunk_of = lambda ref: {i: ref[0, i * c:(i + 1) * c, :] for i in range(nc)}
    at, rt, bt, kt, bh, kh, v = (chunk_of(ref) for ref in
                                 (at_ref, rt_ref, bt_ref, kt_ref, bh_ref, kh_ref, v_ref))
    g_tot = {i: gt_ref[0, i:i + 1, :] for i in range(nc)}

    lane = lax.broadcasted_iota(jnp.int32, (1, LANES), 1)
    lo = lane < RWKV_HEAD

    def stack(x):
        zero = jnp.zeros_like(x)
        return jnp.concatenate([jnp.where(lo, x, zero), jnp.where(lo, zero, x)], axis=0)

    row2 = lax.broadcasted_iota(jnp.int32, (c2, c2), 0)
    col2 = lax.broadcasted_iota(jnp.int32, (c2, c2), 1)
    same = (row2 // c) == (col2 // c)
    strict = same & (col2 < row2)
    incl = same & (col2 <= row2)
    eye = (row2 == col2).astype(F32)
    levels = int(math.log2(c))
    level_masks = [((row2 >> l) == (col2 >> l)) & ((row2 >> (l - 1)) != (col2 >> (l - 1)))
                   & (col2 < row2) for l in range(1, levels + 1)]
    ones_bd = ((lax.broadcasted_iota(jnp.int32, (LANES, LANES), 0) // RWKV_HEAD)
               == (lax.broadcasted_iota(jnp.int32, (LANES, LANES), 1) // RWKV_HEAD)).astype(BF16)
    ones_bd2 = jnp.concatenate([ones_bd, ones_bd], axis=0)
    nt = lambda x, y: lax.dot_general(x, y, _NT, preferred_element_type=F32)
    tn = lambda x, y: lax.dot_general(x, y, _TN, preferred_element_type=F32)

    npairs = RWKV_HEADS // 2
    sls = [slice(p * LANES, (p + 1) * LANES) for p in range(npairs)]
    cat0 = lambda *xs: jnp.concatenate(xs, axis=0)
    cat1 = lambda *xs: jnp.concatenate(xs, axis=1)
    assert c2 == LANES
    qp, y0, gm, g_col = {}, {}, {}, {}

    for group in [range(g, min(g + group_size, nc)) for g in range(0, nc, group_size)]:
        chains = [(i, p) for i in group for p in range(npairs)]
        each = lambda f: {ip: f(ip) for ip in chains}
        pick = lambda xs: each(lambda ip: stack(xs[ip[0]][:, sls[ip[1]]]))
        a2, r2, b2, k2, bh2, kh2, v2 = (pick(xs) for xs in (at, rt, bt, kt, bh, kh, v))

        s_ar = each(lambda ip: nt(cat0(a2[ip], r2[ip]), cat0(b2[ip], k2[ip])))
        a_ab = each(lambda ip: jnp.where(strict, s_ar[ip][:c2, :c2], 0.0))
        a_ak = each(lambda ip: jnp.where(strict, s_ar[ip][:c2, c2:], 0.0).astype(BF16))
        a_rb = each(lambda ip: jnp.where(incl, s_ar[ip][c2:, :c2], 0.0).astype(BF16))
        a_rk = each(lambda ip: jnp.where(incl, s_ar[ip][c2:, c2:], 0.0).astype(BF16))

        tinv = each(lambda ip: eye + jnp.where(level_masks[0], a_ab[ip], 0.0))
        for l in range(2, levels + 1):
            e = each(lambda ip: jnp.where(level_masks[l - 1], a_ab[ip], 0.0).astype(BF16))
            tb = each(lambda ip: tinv[ip].astype(BF16))
            te = each(lambda ip: _dot(tb[ip], e[ip]).astype(BF16))
            tinv = each(lambda ip: tinv[ip] + _dot(te[ip], tb[ip]))
        tb = each(lambda ip: tinv[ip].astype(BF16))

        av = each(lambda ip: _dot(cat0(a_ak[ip], a_rk[ip]), v2[ip]))
        akv = each(lambda ip: av[ip][:c2].astype(BF16))
        wu = each(lambda ip: _dot(tb[ip], cat1(a2[ip], akv[ip])).astype(BF16))
        pg = each(lambda ip: tn(bh2[ip], wu[ip]))
        qy = each(lambda ip: _dot(a_rb[ip], wu[ip]))
        gm.update(each(lambda ip: pg[ip][:, LANES:] + tn(kh2[ip], v2[ip])))
        qp.update(each(lambda ip: cat0((r2[ip].astype(F32) + qy[ip][:, :LANES]).astype(BF16),
                                       pg[ip][:, :LANES].astype(BF16))))
        y0.update(each(lambda ip: qy[ip][:, LANES:] + av[ip][c2:]))
        g_col.update(each(
            lambda ip: jnp.sum(eye * g_tot[ip[0]][:, sls[ip[1]]], axis=1, keepdims=True)))

    state = [h_ref[p] for p in range(npairs)]
    ys = [[] for _ in range(npairs)]
    for i in range(nc):
        for p in range(npairs):
            qh = _dot(qp[i, p], state[p].astype(BF16))
            y2 = qh[:c2] + y0[i, p]
            ys[p].append(y2[:c] + y2[c:])
            state[p] = g_col[i, p] * state[p] + qh[c2:] + gm[i, p]
    for p in range(npairs):
        h_ref[p] = state[p]

    for p in range(npairs):
        sl = sls[p]
        y = cat0(*ys[p])
        mean = _split_dot_k2(y, ones_bd2) * (1.0 / RWKV_HEAD)
        yc = y - mean
        var = _split_dot_k2(yc * yc, ones_bd2) * (1.0 / RWKV_HEAD)
        yn = yc * lax.rsqrt(var + GN_EPS) * gnw_ref[:, sl] + gnb_ref[:, sl]
        o_ref[0, :, sl] = ((yn + bonus_ref[0, :, sl]) * gate_ref[0, :, sl]).astype(o_ref.dtype)


def _rwkv(at, rt, bt, kt, bh, kh, v, gt, bonus, gate, gn_w, gn_b, *, chunk, tb):
    b, t, w = at.shape
    tile = pl.BlockSpec((1, tb, w), lambda i, j: (i, j, 0))
    const = pl.BlockSpec((1, w), lambda i, j: (0, 0))
    return pl.pallas_call(
        functools.partial(_rwkv_body, chunk=chunk, group_size=RWKV_GROUP),
        grid=(b, t // tb),
        in_specs=[tile] * 7 + [pl.BlockSpec((1, tb // chunk, w), lambda i, j: (i, j, 0))]
        + [tile] * 2 + [const] * 2,
        out_specs=tile,
        out_shape=jax.ShapeDtypeStruct((b, t, w), BF16),
        scratch_shapes=[pltpu.VMEM((RWKV_HEADS // 2, LANES, LANES), F32)],
        compiler_params=pltpu.CompilerParams(
            dimension_semantics=("arbitrary", "arbitrary"), vmem_limit_bytes=VMEM_LIMIT),
        name="rwkv",
    )(at, rt, bt, kt, bh, kh, v, gt, bonus, gate, gn_w.reshape(1, w), gn_b.reshape(1, w))


def _attn_body(lq1_ref, lk1_ref, lq2_ref, lk2_ref, q_ref, k_ref, vt_ref, sw_ref, o_ref, acc_ref,
               s_ref, m_ref, *, tq, tk, wq, lambda_init):
    nq = q_ref.shape[1] // tq
    kpq = tq // tk
    lane = lax.broadcasted_iota(jnp.int32, (1, LANES), 1)
    lam = (jnp.exp(jnp.sum(lq1_ref[...] * lk1_ref[...], axis=-1, keepdims=True))
           - jnp.exp(jnp.sum(lq2_ref[...] * lk2_ref[...], axis=-1, keepdims=True)) + lambda_init)
    strips = [(c, w) for c in range(2) for w in range(tq // wq)]
    items = [(qi, j) for qi in range(nq) for j in range(kpq * (qi + 1))]
    ones_rows = jnp.ones((BF16_ROWS, tk), BF16)
    qs = {}

    def masked_q(qi):
        if qi not in qs:
            q = q_ref[0, qi * tq:(qi + 1) * tq]
            zero = jnp.zeros_like(q)
            qs[qi] = (jnp.where(lane < DIFF_QK, q, zero), jnp.where(lane < DIFF_QK, zero, q))
        return qs[qi]

    def scores_into(slot, qi, j):
        q_from = max(0, j * tk - qi * tq)
        ks = k_ref[0, j * tk:(j + 1) * tk, :]
        for c in range(2):
            s_ref[slot, c, :, q_from:] = lax.dot_general(ks, masked_q(qi)[c][q_from:], _NT,
                                                         preferred_element_type=F32)

    def consume(slot, qi, j):
        off = j * tk - qi * tq
        for c, w in strips:
            nk = min(tk, max(0, (w + 1) * wq - off))
            if nk == 0:
                continue
            cols = slice(w * wq, (w + 1) * wq)
            s = s_ref[slot, c, 0:nk, cols]
            if off + nk - 1 > w * wq:
                key = lax.broadcasted_iota(jnp.int32, (nk, wq), 0) + off
                qry = lax.broadcasted_iota(jnp.int32, (nk, wq), 1) + w * wq
                s = jnp.where(key <= qry, s, NEG_BIG)
            m = m_ref[qi, c, :, cols]
            m_new = jnp.maximum(m, jnp.max(s, axis=0, keepdims=True))
            alpha = jnp.exp2(m - m_new)
            pr = jnp.exp2(s - m_new)
            m_ref[qi, c, :, cols] = m_new
            vt = jnp.concatenate([vt_ref[0, :, j * tk:j * tk + nk], ones_rows[:, :nk]], axis=0)
            acc_ref[qi, c, :, cols] = alpha * acc_ref[qi, c, :, cols] + _dot(vt, pr.astype(BF16))

    def finish(qi):
        l1, l2 = (acc_ref[qi, c, DIFF_V:DIFF_V + 1] for c in range(2))
        o = (acc_ref[qi, 0, :DIFF_V] * (1.0 / l1)
             - lam * (acc_ref[qi, 1, :DIFF_V] * (1.0 / l2)))
        o = o * lax.rsqrt(jnp.mean(o * o, axis=0, keepdims=True) + SUBLN_EPS) * sw_ref[...]
        o = o * (1.0 - lambda_init)
        o_ref[0, qi * tq:(qi + 1) * tq] = o.T.astype(o_ref.dtype)

    acc_ref[...] = jnp.zeros_like(acc_ref)
    m_ref[...] = jnp.full_like(m_ref, NEG_BIG)
    scores_into(0, *items[0])
    for n, (qi, j) in enumerate(items):
        if n + 1 < len(items):
            scores_into((n + 1) % 2, *items[n + 1])
        consume(n % 2, qi, j)
        if j == kpq * (qi + 1) - 1:
            finish(qi)


def _attn(q, k, vt, lq1, lk1, lq2, lk2, subln_w, *, tq, tk, wq, lambda_init):
    b, t, w = q.shape
    heads = w // DIFF_V
    nq = t // tq
    lam_spec = pl.BlockSpec((1, DIFF_QK), lambda i, h: (0, 0))
    return pl.pallas_call(
        functools.partial(_attn_body, tq=tq, tk=tk, wq=wq, lambda_init=lambda_init),
        grid=(b, heads),
        in_specs=[lam_spec] * 4 + [
            pl.BlockSpec((1, t, DIFF_V), lambda i, h: (i, 0, h)),
            pl.BlockSpec((1, t, DIFF_V), lambda i, h: (i, 0, h)),
            pl.BlockSpec((1, DIFF_V, t), lambda i, h: (i, h, 0)),
            pl.BlockSpec((DIFF_V, 1), lambda i, h: (0, 0)),
        ],
        out_specs=pl.BlockSpec((1, t, DIFF_V), lambda i, h: (i, 0, h)),
        out_shape=jax.ShapeDtypeStruct((b, t, w), BF16),
        scratch_shapes=[pltpu.VMEM((nq, 2, DIFF_V + BF16_ROWS, tq), F32),
                        pltpu.VMEM((2, 2, tk, tq), F32), pltpu.VMEM((nq, 2, 1, tq), F32)],
        compiler_params=pltpu.CompilerParams(
            dimension_semantics=("parallel", "parallel"), vmem_limit_bytes=VMEM_LIMIT),
        name="attn",
    )(lq1.reshape(1, -1), lk1.reshape(1, -1), lq2.reshape(1, -1), lk2.reshape(1, -1),
      q, k, vt, subln_w.reshape(-1, 1))


def _pick(n, pref):
    return pref if n % pref == 0 else n


def _layer(x, l, p, *, chunk=64, tb_rwkv=512, tm_ffn=1024, tf=256, tm_proj=512, tq=1024,
           tk=512, wq_attn=256):
    b, t, d = x.shape
    m = b * t
    lambda_init = 0.8 - 0.6 * math.exp(-0.3 * l)
    tm_ffn = _pick(m, tm_ffn)
    tf = _pick(p["ffn1_w_gate"].shape[-1], tf)
    x = _ffn(x.reshape(m, d), p["ffn1_pre_g"][l], p["ffn1_post_g"][l], p["ffn1_w_gate"][l],
             p["ffn1_w_up"][l], p["ffn1_w_down"][l], tm=tm_ffn, tf=tf)
    outs = _proj(x.reshape(b, t, d), p["mix_pre_g"][l], p["w_in"][l], p["shift_mu"][l],
                 p["rwkv_w_up"][l], p["rwkv_a_up"][l], p["rwkv_g_up"][l], p["rwkv_w0"][l],
                 p["rwkv_a0"][l], p["rwkv_k_k"][l], p["rwkv_k_a"][l], p["rwkv_r_k"][l].reshape(-1),
                 tm=_pick(t, tm_proj), chunk=chunk)
    y_rwkv = _rwkv(*outs[:10], p["rwkv_gn_w"][l], p["rwkv_gn_b"][l], chunk=chunk,
                   tb=_pick(t, tb_rwkv))
    y_diff = _attn(*outs[10:], p["diff_lam_q1"][l], p["diff_lam_k1"][l], p["diff_lam_q2"][l],
                   p["diff_lam_k2"][l], p["diff_subln_w"][l], tq=_pick(t, tq), tk=_pick(t, tk), wq=wq_attn,
                   lambda_init=lambda_init)
    x = _ffn(x, p["ffn2_pre_g"][l], p["ffn2_post_g"][l], p["ffn2_w_gate"][l], p["ffn2_w_up"][l],
             p["ffn2_w_down"][l], tm=tm_ffn, tf=tf,
             mix=(y_rwkv.reshape(m, -1), y_diff.reshape(m, -1), p["w_o"][l], p["mix_post_g"][l]))
    return x.reshape(b, t, d)


def kernel(x, ffn1_pre_g, ffn1_post_g, ffn1_w_gate, ffn1_w_up, ffn1_w_down, mix_pre_g, mix_post_g,
           w_in, shift_mu, w_o, rwkv_w0, rwkv_w_up, rwkv_a0, rwkv_a_up, rwkv_g_up, rwkv_k_k,
           rwkv_k_a, rwkv_r_k, rwkv_gn_w, rwkv_gn_b, diff_lam_q1, diff_lam_k1, diff_lam_q2,
           diff_lam_k2, diff_subln_w, ffn2_pre_g, ffn2_post_g, ffn2_w_gate, ffn2_w_up, ffn2_w_down):
    p = dict(ffn1_pre_g=ffn1_pre_g, ffn1_post_g=ffn1_post_g, ffn1_w_gate=ffn1_w_gate,
             ffn1_w_up=ffn1_w_up, ffn1_w_down=ffn1_w_down, mix_pre_g=mix_pre_g,
             mix_post_g=mix_post_g, w_in=w_in, shift_mu=shift_mu, w_o=w_o, rwkv_w0=rwkv_w0,
             rwkv_w_up=rwkv_w_up, rwkv_a0=rwkv_a0, rwkv_a_up=rwkv_a_up, rwkv_g_up=rwkv_g_up,
             rwkv_k_k=rwkv_k_k, rwkv_k_a=rwkv_k_a, rwkv_r_k=rwkv_r_k, rwkv_gn_w=rwkv_gn_w,
             rwkv_gn_b=rwkv_gn_b, diff_lam_q1=diff_lam_q1, diff_lam_k1=diff_lam_k1,
             diff_lam_q2=diff_lam_q2, diff_lam_k2=diff_lam_k2, diff_subln_w=diff_subln_w,
             ffn2_pre_g=ffn2_pre_g, ffn2_post_g=ffn2_post_g, ffn2_w_gate=ffn2_w_gate,
             ffn2_w_up=ffn2_w_up, ffn2_w_down=ffn2_w_down)
    for l in range(ffn1_pre_g.shape[0]):
        x = _layer(x, l, p)
    return x
```

```python
import functools
import math

import jax
import jax.numpy as jnp
from jax import lax
from jax.experimental import pallas as pl
from jax.experimental.pallas import tpu as pltpu

F32 = jnp.float32
BF16 = jnp.bfloat16

RWKV_HEADS = 8
RWKV_HEAD = 64
RWKV_W = RWKV_HEADS * RWKV_HEAD
DIFF_HEADS = 4
DIFF_QK = 64
DIFF_V = 2 * DIFF_QK
DIFF_W = DIFF_HEADS * DIFF_V
DECAY_LORA = 64
ICLR_LORA = 64
GATE_LORA = 160
LORA_W = DECAY_LORA + ICLR_LORA + GATE_LORA
LANES = 128
SUBLANES = 8
BF16_ROWS = 16
FFN_PARTS = 4
RWKV_GROUP = 4
LORA_PAD = -(-LORA_W // LANES) * LANES
SHIFT_COLS = 3 * RWKV_W + LORA_W
SHIFT_PAD = 3 * RWKV_W + LORA_PAD
FFN_RES = 0.5
NORM_EPS = 1e-6
GN_EPS = 64e-5
SUBLN_EPS = 1e-5
NEG_BIG = -1e30
LOG2E = math.log2(math.e)
EXP_NEG_HALF = math.exp(-0.5)
VMEM_LIMIT = 56 * 1024 * 1024

_NT = (((1,), (1,)), ((), ()))
_TN = (((0,), (0,)), ((), ()))


def _dot(a, b):
    return jnp.dot(a, b, preferred_element_type=F32)


def _split_dot_k2(x, w2_bf16):
    hi = x.astype(BF16)
    lo = (x - hi.astype(F32)).astype(BF16)
    return _dot(jnp.concatenate([hi, lo], axis=1), w2_bf16)


def _rms(x, g, eps):
    return x * lax.rsqrt(jnp.mean(x * x, axis=-1, keepdims=True) + eps) * g


def _ffn_body(*refs, tf, mix):
    if mix:
        x_ref, yr_ref, yd_ref, wo_ref, gmix_ref, *refs = refs
    else:
        x_ref, *refs = refs
    gpre_ref, gpost_ref, wg_ref, wu_ref, wd_ref, o_ref, a_ref = refs
    part = x_ref.shape[0] // FFN_PARTS
    parts = [slice(r * part, (r + 1) * part) for r in range(FFN_PARTS)]
    xs, hs = [], []
    for rows in parts:
        x = x_ref[rows]
        if mix:
            nr = yr_ref.shape[1]
            y = _dot(yr_ref[rows], wo_ref[:nr]) + _dot(yd_ref[rows], wo_ref[nr:])
            x = x + _rms(y, gmix_ref[...], NORM_EPS)
        xs.append(x)
        hs.append(_rms(x, gpre_ref[...], NORM_EPS).astype(BF16))
    for rows, x, h in zip(parts, xs, hs):
        for f0 in range(0, wg_ref.shape[1], tf):
            cols = slice(f0, f0 + tf)
            g = _dot(h, wg_ref[:, cols])
            u = _dot(h, wu_ref[:, cols])
            a_ref[rows, cols] = (g * jax.nn.sigmoid(g) * u).astype(BF16)
        y = _dot(a_ref[rows], wd_ref[...])
        o_ref[rows] = x + FFN_RES * _rms(y, gpost_ref[...], NORM_EPS)


def _ffn(x, g_pre, g_post, w_gate, w_up, w_down, *, tm, tf, mix=None):
    m, d = x.shape
    f = w_gate.shape[1]
    rows = lambda width: pl.BlockSpec((tm, width), lambda i: (i, 0))
    resident = lambda shape: pl.BlockSpec(shape, lambda i: (0, 0), pipeline_mode=pl.Buffered(1))
    args, specs = [x], [rows(d)]
    if mix is not None:
        y_rwkv, y_diff, w_o, g_mix = mix
        args += [y_rwkv, y_diff, w_o.astype(BF16), g_mix.reshape(1, d)]
        specs += [rows(y_rwkv.shape[1]), rows(y_diff.shape[1]), resident(w_o.shape), resident((1, d))]
    args += [g_pre.reshape(1, d), g_post.reshape(1, d), w_gate.astype(BF16), w_up.astype(BF16),
             w_down.astype(BF16)]
    specs += [resident((1, d)), resident((1, d)), resident((d, f)), resident((d, f)),
              resident((f, d))]
    return pl.pallas_call(
        functools.partial(_ffn_body, tf=tf, mix=mix is not None),
        grid=(m // tm,),
        in_specs=specs,
        out_specs=rows(d),
        out_shape=jax.ShapeDtypeStruct((m, d), F32),
        scratch_shapes=[pltpu.VMEM((tm, f), BF16)],
        compiler_params=pltpu.CompilerParams(
            dimension_semantics=("parallel",), vmem_limit_bytes=VMEM_LIMIT),
        name="ffn_mix" if mix is not None else "ffn",
    )(*args)


def _proj_body(x_ref, g_ref, wrkv_ref, wlora_ref, wdiff_ref, mu_ref, wup_ref, ones_ref, w0_ref,
               a0_ref, kk_ref, ka_ref, rk_ref,
               at_out, rt_out, bt_out, kt_out, bh_out, kh_out, v_out, gt_out, bonus_out, gate_out,
               qd_out, kd_out, vd_out, carry_ref, *, chunk):
    @pl.when(pl.program_id(1) == 0)
    def _():
        carry_ref[...] = jnp.zeros_like(carry_ref)

    tm = x_ref.shape[1]
    nrkv = 3 * RWKV_W
    first_row = lax.broadcasted_iota(jnp.int32, (SUBLANES, 1), 0) == 0
    lane = lax.broadcasted_iota(jnp.int32, (1, LORA_PAD), 1)
    ones_bd = ones_ref[...]

    def shift_mix(p, cols):
        prev = pltpu.roll(p, 1, 0)
        prev = jnp.concatenate(
            [jnp.where(first_row, carry_ref[:, cols], prev[:SUBLANES]), prev[SUBLANES:]], axis=0)
        carry_ref[:, cols] = p[tm - 1:]
        return p + (prev - p) * mu_ref[:, cols]

    h = _rms(x_ref[0], g_ref[...], NORM_EPS).astype(BF16)
    z = shift_mix(_dot(h, wlora_ref[...]), slice(nrkv, SHIFT_PAD))
    p_k = shift_mix(_dot(h, wrkv_ref[:, RWKV_W:2 * RWKV_W]), slice(RWKV_W, 2 * RWKV_W))
    act = jnp.where(lane < DECAY_LORA, jnp.tanh(z),
                    jnp.where(lane < DECAY_LORA + ICLR_LORA, z, jax.nn.sigmoid(z))).astype(BF16)
    up = _dot(act[:, :LANES], wup_ref[:LANES, :2 * RWKV_W])
    gate_out[0] = _dot(act[:, LANES:], wup_ref[LANES:, 2 * RWKV_W:])
    p_r = shift_mix(_dot(h, wrkv_ref[:, :RWKV_W]), slice(0, RWKV_W))
    p_v = shift_mix(_dot(h, wrkv_ref[:, 2 * RWKV_W:]), slice(2 * RWKV_W, nrkv))
    p_diff = _dot(h, wdiff_ref[...])

    ld = -EXP_NEG_HALF * jax.nn.sigmoid(w0_ref[...] + up[:, :RWKV_W])
    iclr = jax.nn.sigmoid(a0_ref[...] + up[:, RWKV_W:])
    kk = p_k * kk_ref[...]
    kk = kk * lax.rsqrt(jnp.maximum(_dot((kk * kk).astype(BF16), ones_bd), 1e-24))
    k = p_k * (iclr * ka_ref[...] + (1.0 - ka_ref[...]))
    b = kk * iclr
    v_out[0] = p_v.astype(BF16)
    bonus_out[0] = _dot((p_r * k * rk_ref[...]).astype(BF16), ones_bd) * p_v

    ri = lax.broadcasted_iota(jnp.int32, (chunk, chunk), 0)
    ci = lax.broadcasted_iota(jnp.int32, (chunk, chunk), 1)
    tril3 = jnp.concatenate([(ci <= ri).astype(BF16)] * 3, axis=1)
    ld1 = ld.astype(BF16)
    res = ld - ld1.astype(F32)
    ld2 = res.astype(BF16)
    ld3 = (res - ld2.astype(F32)).astype(BF16)
    cums, tots = [], []
    for i in range(tm // chunk):
        rows = slice(i * chunk, (i + 1) * chunk)
        cum = _dot(tril3, jnp.concatenate([ld1[rows], ld2[rows], ld3[rows]], axis=0))
        cums.append(cum)
        tots.append(jnp.broadcast_to(cum[chunk - 1:], cum.shape))
    gt_out[0] = jnp.exp(jnp.concatenate([t[:1] for t in tots], axis=0))
    cum = jnp.concatenate(cums, axis=0)
    g_inv = jnp.exp(-cum)
    g_rest = jnp.exp(jnp.concatenate(tots, axis=0) - cum)
    at_out[0] = (-kk * jnp.exp(cum - ld)).astype(BF16)
    rt_out[0] = (p_r * jnp.exp(cum)).astype(BF16)
    bt_out[0] = (b * g_inv).astype(BF16)
    kt_out[0] = (k * g_inv).astype(BF16)
    bh_out[0] = (b * g_rest).astype(BF16)
    kh_out[0] = (k * g_rest).astype(BF16)

    qd_out[0] = (p_diff[:, :DIFF_W] * (DIFF_QK ** -0.5 * LOG2E)).astype(BF16)
    kd_out[0] = p_diff[:, DIFF_W:2 * DIFF_W].astype(BF16)
    vd_out[0] = p_diff[:, 2 * DIFF_W:].T.astype(BF16)


def _head_ones(width, head):
    i = jnp.arange(width) // head
    return (i[:, None] == i[None, :]).astype(BF16)


def _proj(x, g, w_in, shift_mu, w_up, a_up, g_up, w0, a0, k_k, k_a, r_k, *, tm, chunk):
    b, t, d = x.shape
    pad = SHIFT_PAD - SHIFT_COLS
    nrkv = 3 * RWKV_W
    w = w_in.astype(BF16)
    w_lora = jnp.pad(w[:, nrkv:SHIFT_COLS], ((0, 0), (0, pad)))
    w_diff = w[:, SHIFT_COLS:]
    mu = jnp.pad(shift_mu, (0, pad)).reshape(1, SHIFT_PAD)
    wup = jnp.zeros((LORA_PAD, 3 * RWKV_W), F32)
    wup = wup.at[:DECAY_LORA, :RWKV_W].set(w_up)
    wup = wup.at[DECAY_LORA:DECAY_LORA + ICLR_LORA, RWKV_W:2 * RWKV_W].set(a_up)
    wup = wup.at[DECAY_LORA + ICLR_LORA:LORA_W, 2 * RWKV_W:].set(g_up)
    wup = wup.astype(BF16)
    vec = lambda a: a.reshape(1, RWKV_W)
    const = lambda shape: pl.BlockSpec(shape, lambda i, j: (0,) * len(shape))
    tile = lambda width: pl.BlockSpec((1, tm, width), lambda i, j: (i, j, 0))
    f32_out = jax.ShapeDtypeStruct((b, t, RWKV_W), F32)
    bf_out = jax.ShapeDtypeStruct((b, t, DIFF_W), BF16)
    assert RWKV_W == DIFF_W
    return pl.pallas_call(
        functools.partial(_proj_body, chunk=chunk),
        grid=(b, t // tm),
        in_specs=[tile(d), const((1, d)), const((d, nrkv)), const((d, LORA_PAD)),
                  const((d, 3 * DIFF_W)), const((1, SHIFT_PAD)),
                  const((LORA_PAD, 3 * RWKV_W)), const((RWKV_W, RWKV_W))] + [const((1, RWKV_W))] * 5,
        out_specs=[tile(RWKV_W)] * 7 + [pl.BlockSpec((1, tm // chunk, RWKV_W), lambda i, j: (i, j, 0))]
        + [tile(RWKV_W)] * 2 + [tile(DIFF_W)] * 2
        + [pl.BlockSpec((1, DIFF_W, tm), lambda i, j: (i, 0, j))],
        out_shape=[bf_out] * 7 + [jax.ShapeDtypeStruct((b, t // chunk, RWKV_W), F32)]
        + [f32_out] * 2 + [bf_out] * 2 + [jax.ShapeDtypeStruct((b, DIFF_W, t), BF16)],
        scratch_shapes=[pltpu.VMEM((1, SHIFT_PAD), F32)],
        compiler_params=pltpu.CompilerParams(
            dimension_semantics=("arbitrary", "arbitrary"), vmem_limit_bytes=VMEM_LIMIT),
        name="proj",
    )(x, g.reshape(1, d), w, w_lora, w_diff, mu, wup, _head_ones(RWKV_W, RWKV_HEAD),
      vec(w0), vec(a0), vec(k_k), vec(k_a), vec(r_k))


def _rwkv_body(at_ref, rt_ref, bt_ref, kt_ref, bh_ref, kh_ref, v_ref, gt_ref, bonus_ref, gate_ref,
               gnw_ref, gnb_ref, o_ref, h_ref, *, chunk, group_size):
    c = chunk
    c2 = 2 * c
    nc = at_ref.shape[1] // c

    @pl.when(pl.program_id(1) == 0)
    def _():
        h_ref[...] = jnp.zeros_like(h_ref)

    chunk_of = lambda ref: {i: ref[0, i * c:(i + 1) * c, :] for i in range(nc)}
    at, rt, bt, kt, bh, kh, v = (chunk_of(ref) for ref in
                                 (at_ref, rt_ref, bt_ref, kt_ref, bh_ref, kh_ref, v_ref))
    g_tot = {i: gt_ref[0, i:i + 1, :] for i in range(nc)}

    lane = lax.broadcasted_iota(jnp.int32, (1, LANES), 1)
    lo = lane < RWKV_HEAD

    def stack(x):
        zero = jnp.zeros_like(x)
        return jnp.concatenate([jnp.where(lo, x, zero), jnp.where(lo, zero, x)], axis=0)

    row2 = lax.broadcasted_iota(jnp.int32, (c2, c2), 0)
    col2 = lax.broadcasted_iota(jnp.int32, (c2, c2), 1)
    same = (row2 // c) == (col2 // c)
    strict = same & (col2 < row2)
    incl = same & (col2 <= row2)
    eye = (row2 == col2).astype(F32)
    levels = int(math.log2(c))
    level_masks = [((row2 >> l) == (col2 >> l)) & ((row2 >> (l - 1)) != (col2 >> (l - 1)))
                   & (col2 < row2) for l in range(1, levels + 1)]
    ones_bd = ((lax.broadcasted_iota(jnp.int32, (LANES, LANES), 0) // RWKV_HEAD)
               == (lax.broadcasted_iota(jnp.int32, (LANES, LANES), 1) // RWKV_HEAD)).astype(BF16)
    ones_bd2 = jnp.concatenate([ones_bd, ones_bd], axis=0)
    nt = lambda x, y: lax.dot_general(x, y, _NT, preferred_element_type=F32)
    tn = lambda x, y: lax.dot_general(x, y, _TN, preferred_element_type=F32)

    npairs = RWKV_HEADS // 2
    sls = [slice(p * LANES, (p + 1) * LANES) for p in range(npairs)]
    cat0 = lambda *xs: jnp.concatenate(xs, axis=0)
    cat1 = lambda *xs: jnp.concatenate(xs, axis=1)
    assert c2 == LANES
    qp, y0, gm, g_col = {}, {}, {}, {}

    for group in [range(g, min(g + group_size, nc)) for g in range(0, nc, group_size)]:
        chains = [(i, p) for i in group for p in range(npairs)]
        each = lambda f: {ip: f(ip) for ip in chains}
        pick = lambda xs: each(lambda ip: stack(xs[ip[0]][:, sls[ip[1]]]))
        a2, r2, b2, k2, bh2, kh2, v2 = (pick(xs) for xs in (at, rt, bt, kt, bh, kh, v))

        s_ar = each(lambda ip: nt(cat0(a2[ip], r2[ip]), cat0(b2[ip], k2[ip])))
        a_ab = each(lambda ip: jnp.where(strict, s_ar[ip][:c2, :c2], 0.0))
        a_ak = each(lambda ip: jnp.where(strict, s_ar[ip][:c2, c2:], 0.0).astype(BF16))
        a_rb = each(lambda ip: jnp.where(incl, s_ar[ip][c2:, :c2], 0.0).astype(BF16))
        a_rk = each(lambda ip: jnp.where(incl, s_ar[ip][c2:, c2:], 0.0).astype(BF16))

        tinv = each(lambda ip: eye + jnp.where(level_masks[0], a_ab[ip], 0.0))
        for l in range(2, levels + 1):
            e = each(lambda ip: jnp.where(level_masks[l - 1], a_ab[ip], 0.0).astype(BF16))
            tb = each(lambda ip: tinv[ip].astype(BF16))
            te = each(lambda ip: _dot(tb[ip], e[ip]).astype(BF16))
            tinv = each(lambda ip: tinv[ip] + _dot(te[ip], tb[ip]))
        tb = each(lambda ip: tinv[ip].astype(BF16))

        av = each(lambda ip: _dot(cat0(a_ak[ip], a_rk[ip]), v2[ip]))
        akv = each(lambda ip: av[ip][:c2].astype(BF16))
        wu = each(lambda ip: _dot(tb[ip], cat1(a2[ip], akv[ip])).astype(BF16))
        pg = each(lambda ip: tn(bh2[ip], wu[ip]))
        qy = each(lambda ip: _dot(a_rb[ip], wu[ip]))
        gm.update(each(lambda ip: pg[ip][:, LANES:] + tn(kh2[ip], v2[ip])))
        qp.update(each(lambda ip: cat0((r2[ip].astype(F32) + qy[ip][:, :LANES]).astype(BF16),
                                       pg[ip][:, :LANES].astype(BF16))))
        y0.update(each(lambda ip: qy[ip][:, LANES:] + av[ip][c2:]))
        g_col.update(each(
            lambda ip: jnp.sum(eye * g_tot[ip[0]][:, sls[ip[1]]], axis=1, keepdims=True)))

    state = [h_ref[p] for p in range(npairs)]
    ys = [[] for _ in range(npairs)]
    for i in range(nc):
        for p in range(npairs):
            qh = _dot(qp[i, p], state[p].astype(BF16))
            y2 = qh[:c2] + y0[i, p]
            ys[p].append(y2[:c] + y2[c:])
            state[p] = g_col[i, p] * state[p] + qh[c2:] + gm[i, p]
    for p in range(npairs):
        h_ref[p] = state[p]

    for p in range(npairs):
        sl = sls[p]
        y = cat0(*ys[p])
        mean = _split_dot_k2(y, ones_bd2) * (1.0 / RWKV_HEAD)
        yc = y - mean
        var = _split_dot_k2(yc * yc, ones_bd2) * (1.0 / RWKV_HEAD)
        yn = yc * lax.rsqrt(var + GN_EPS) * gnw_ref[:, sl] + gnb_ref[:, sl]
        o_ref[0, :, sl] = ((yn + bonus_ref[0, :, sl]) * gate_ref[0, :, sl]).astype(o_ref.dtype)


def _rwkv(at, rt, bt, kt, bh, kh, v, gt, bonus, gate, gn_w, gn_b, *, chunk, tb):
    b, t, w = at.shape
    tile = pl.BlockSpec((1, tb, w), lambda i, j: (i, j, 0))
    const = pl.BlockSpec((1, w), lambda i, j: (0, 0))
    return pl.pallas_call(
        functools.partial(_rwkv_body, chunk=chunk, group_size=RWKV_GROUP),
        grid=(b, t // tb),
        in_specs=[tile] * 7 + [pl.BlockSpec((1, tb // chunk, w), lambda i, j: (i, j, 0))]
        + [tile] * 2 + [const] * 2,
        out_specs=tile,
        out_shape=jax.ShapeDtypeStruct((b, t, w), BF16),
        scratch_shapes=[pltpu.VMEM((RWKV_HEADS // 2, LANES, LANES), F32)],
        compiler_params=pltpu.CompilerParams(
            dimension_semantics=("arbitrary", "arbitrary"), vmem_limit_bytes=VMEM_LIMIT),
        name="rwkv",
    )(at, rt, bt, kt, bh, kh, v, gt, bonus, gate, gn_w.reshape(1, w), gn_b.reshape(1, w))


def _attn_body(lq1_ref, lk1_ref, lq2_ref, lk2_ref, q_ref, k_ref, vt_ref, sw_ref, o_ref, acc_ref,
               s_ref, m_ref, *, tq, tk, wq, lambda_init):
    nq = q_ref.shape[1] // tq
    kpq = tq // tk
    lane = lax.broadcasted_iota(jnp.int32, (1, LANES), 1)
    lam = (jnp.exp(jnp.sum(lq1_ref[...] * lk1_ref[...], axis=-1, keepdims=True))
           - jnp.exp(jnp.sum(lq2_ref[...] * lk2_ref[...], axis=-1, keepdims=True)) + lambda_init)
    strips = [(c, w) for c in range(2) for w in range(tq // wq)]
    items = [(qi, j) for qi in range(nq) for j in range(kpq * (qi + 1))]
    ones_rows = jnp.ones((BF16_ROWS, tk), BF16)
    qs = {}

    def masked_q(qi):
        if qi not in qs:
            q = q_ref[0, qi * tq:(qi + 1) * tq]
            zero = jnp.zeros_like(q)
            qs[qi] = (jnp.where(lane < DIFF_QK, q, zero), jnp.where(lane < DIFF_QK, zero, q))
        return qs[qi]

    def scores_into(slot, qi, j):
        q_from = max(0, j * tk - qi * tq)
        ks = k_ref[0, j * tk:(j + 1) * tk, :]
        for c in range(2):
            s_ref[slot, c, :, q_from:] = lax.dot_general(ks, masked_q(qi)[c][q_from:], _NT,
                                                         preferred_element_type=F32)

    def consume(slot, qi, j):
        off = j * tk - qi * tq
        for c, w in strips:
            nk = min(tk, max(0, (w + 1) * wq - off))
            if nk == 0:
                continue
            cols = slice(w * wq, (w + 1) * wq)
            s = s_ref[slot, c, 0:nk, cols]
            if off + nk - 1 > w * wq:
                key = lax.broadcasted_iota(jnp.int32, (nk, wq), 0) + off
                qry = lax.broadcasted_iota(jnp.int32, (nk, wq), 1) + w * wq
                s = jnp.where(key <= qry, s, NEG_BIG)
            m = m_ref[qi, c, :, cols]
            m_new = jnp.maximum(m, jnp.max(s, axis=0, keepdims=True))
            alpha = jnp.exp2(m - m_new)
            pr = jnp.exp2(s - m_new)
            m_ref[qi, c, :, cols] = m_new
            vt = jnp.concatenate([vt_ref[0, :, j * tk:j * tk + nk], ones_rows[:, :nk]], axis=0)
            acc_ref[qi, c, :, cols] = alpha * acc_ref[qi, c, :, cols] + _dot(vt, pr.astype(BF16))

    def finish(qi):
        l1, l2 = (acc_ref[qi, c, DIFF_V:DIFF_V + 1] for c in range(2))
        o = (acc_ref[qi, 0, :DIFF_V] * (1.0 / l1)
             - lam * (acc_ref[qi, 1, :DIFF_V] * (1.0 / l2)))
        o = o * lax.rsqrt(jnp.mean(o * o, axis=0, keepdims=True) + SUBLN_EPS) * sw_ref[...]
        o = o * (1.0 - lambda_init)
        o_ref[0, qi * tq:(qi + 1) * tq] = o.T.astype(o_ref.dtype)

    acc_ref[...] = jnp.zeros_like(acc_ref)
    m_ref[...] = jnp.full_like(m_ref, NEG_BIG)
    scores_into(0, *items[0])
    for n, (qi, j) in enumerate(items):
        if n + 1 < len(items):
            scores_into((n + 1) % 2, *items[n + 1])
        consume(n % 2, qi, j)
        if j == kpq * (qi + 1) - 1:
            finish(qi)


def _attn(q, k, vt, lq1, lk1, lq2, lk2, subln_w, *, tq, tk, wq, lambda_init):
    b, t, w = q.shape
    heads = w // DIFF_V
    nq = t // tq
    lam_spec = pl.BlockSpec((1, DIFF_QK), lambda i, h: (0, 0))
    return pl.pallas_call(
        functools.partial(_attn_body, tq=tq, tk=tk, wq=wq, lambda_init=lambda_init),
        grid=(b, heads),
        in_specs=[lam_spec] * 4 + [
            pl.BlockSpec((1, t, DIFF_V), lambda i, h: (i, 0, h)),
            pl.BlockSpec((1, t, DIFF_V), lambda i, h: (i, 0, h)),
            pl.BlockSpec((1, DIFF_V, t), lambda i, h: (i, h, 0)),
            pl.BlockSpec((DIFF_V, 1), lambda i, h: (0, 0)),
        ],
        out_specs=pl.BlockSpec((1, t, DIFF_V), lambda i, h: (i, 0, h)),
        out_shape=jax.ShapeDtypeStruct((b, t, w), BF16),
        scratch_shapes=[pltpu.VMEM((nq, 2, DIFF_V + BF16_ROWS, tq), F32),
                        pltpu.VMEM((2, 2, tk, tq), F32), pltpu.VMEM((nq, 2, 1, tq), F32)],
        compiler_params=pltpu.CompilerParams(
            dimension_semantics=("parallel", "parallel"), vmem_limit_bytes=VMEM_LIMIT),
        name="attn",
    )(lq1.reshape(1, -1), lk1.reshape(1, -1), lq2.reshape(1, -1), lk2.reshape(1, -1),
      q, k, vt, subln_w.reshape(-1, 1))


def _pick(n, pref):
    return pref if n % pref == 0 else n


def _layer(x, l, p, *, chunk=64, tb_rwkv=512, tm_ffn=1024, tf=256, tm_proj=512, tq=1024,
           tk=512, wq_attn=256):
    b, t, d = x.shape
    m = b * t
    lambda_init = 0.8 - 0.6 * math.exp(-0.3 * l)
    tm_ffn = _pick(m, tm_ffn)
    tf = _pick(p["ffn1_w_gate"].shape[-1], tf)
    x = _ffn(x.reshape(m, d), p["ffn1_pre_g"][l], p["ffn1_post_g"][l], p["ffn1_w_gate"][l],
             p["ffn1_w_up"][l], p["ffn1_w_down"][l], tm=tm_ffn, tf=tf)
    outs = _proj(x.reshape(b, t, d), p["mix_pre_g"][l], p["w_in"][l], p["shift_mu"][l],
                 p["rwkv_w_up"][l], p["rwkv_a_up"][l], p["rwkv_g_up"][l], p["rwkv_w0"][l],
                 p["rwkv_a0"][l], p["rwkv_k_k"][l], p["rwkv_k_a"][l], p["rwkv_r_k"][l].reshape(-1),
                 tm=_pick(t, tm_proj), chunk=chunk)
    y_rwkv = _rwkv(*outs[:10], p["rwkv_gn_w"][l], p["rwkv_gn_b"][l], chunk=chunk,
                   tb=_pick(t, tb_rwkv))
    y_diff = _attn(*outs[10:], p["diff_lam_q1"][l], p["diff_lam_k1"][l], p["diff_lam_q2"][l],
                   p["diff_lam_k2"][l], p["diff_subln_w"][l], tq=_pick(t, tq), tk=_pick(t, tk), wq=wq_attn,
                   lambda_init=lambda_init)
    x = _ffn(x, p["ffn2_pre_g"][l], p["ffn2_post_g"][l], p["ffn2_w_gate"][l], p["ffn2_w_up"][l],
             p["ffn2_w_down"][l], tm=tm_ffn, tf=tf,
             mix=(y_rwkv.reshape(m, -1), y_diff.reshape(m, -1), p["w_o"][l], p["mix_post_g"][l]))
    return x.reshape(b, t, d)


def kernel(x, ffn1_pre_g, ffn1_post_g, ffn1_w_gate, ffn1_w_up, ffn1_w_down, mix_pre_g, mix_post_g,
           w_in, shift_mu, w_o, rwkv_w0, rwkv_w_up, rwkv_a0, rwkv_a_up, rwkv_g_up, rwkv_k_k,
           rwkv_k_a, rwkv_r_k, rwkv_gn_w, rwkv_gn_b, diff_lam_q1, diff_lam_k1, diff_lam_q2,
           diff_lam_k2, diff_subln_w, ffn2_pre_g, ffn2_post_g, ffn2_w_gate, ffn2_w_up, ffn2_w_down):
    p = dict(ffn1_pre_g=ffn1_pre_g, ffn1_post_g=ffn1_post_g, ffn1_w_gate=ffn1_w_gate,
             ffn1_w_up=ffn1_w_up, ffn1_w_down=ffn1_w_down, mix_pre_g=mix_pre_g,
             mix_post_g=mix_post_g, w_in=w_in, shift_mu=shift_mu, w_o=w_o, rwkv_w0=rwkv_w0,
             rwkv_w_up=rwkv_w_up, rwkv_a0=rwkv_a0, rwkv_a_up=rwkv_a_up, rwkv_g_up=rwkv_g_up,
             rwkv_k_k=rwkv_k_k, rwkv_k_a=rwkv_k_a, rwkv_r_k=rwkv_r_k, rwkv_gn_w=rwkv_gn_w,
             rwkv_gn_b=rwkv_gn_b, diff_lam_q1=diff_lam_q1, diff_lam_k1=diff_lam_k1,
             diff_lam_q2=diff_lam_q2, diff_lam_k2=diff_lam_k2, diff_subln_w=diff_subln_w,
             ffn2_pre_g=ffn2_pre_g, ffn2_post_g=ffn2_post_g, ffn2_w_gate=ffn2_w_gate,
             ffn2_w_up=ffn2_w_up, ffn2_w_down=ffn2_w_down)
    for l in range(ffn1_pre_g.shape[0]):
        x = _layer(x, l, p)
    return x
```

```python
import functools
import math

import jax
import jax.numpy as jnp
from jax import lax
from jax.experimental import pallas as pl
from jax.experimental.pallas import tpu as pltpu

F32 = jnp.float32
BF16 = jnp.bfloat16

RWKV_HEADS = 8
RWKV_HEAD = 64
RWKV_W = RWKV_HEADS * RWKV_HEAD
DIFF_HEADS = 4
DIFF_QK = 64
DIFF_V = 2 * DIFF_QK
DIFF_W = DIFF_HEADS * DIFF_V
DECAY_LORA = 64
ICLR_LORA = 64
GATE_LORA = 160
LORA_W = DECAY_LORA + ICLR_LORA + GATE_LORA
LANES = 128
SUBLANES = 8
BF16_ROWS = 16
FFN_PARTS = 2
RWKV_GROUP = 4
LORA_PAD = -(-LORA_W // LANES) * LANES
SHIFT_COLS = 3 * RWKV_W + LORA_W
SHIFT_PAD = 3 * RWKV_W + LORA_PAD
FFN_RES = 0.5
NORM_EPS = 1e-6
GN_EPS = 64e-5
SUBLN_EPS = 1e-5
NEG_BIG = -1e30
LOG2E = math.log2(math.e)
EXP_NEG_HALF = math.exp(-0.5)
VMEM_LIMIT = 56 * 1024 * 1024

_NT = (((1,), (1,)), ((), ()))
_TN = (((0,), (0,)), ((), ()))


def _dot(a, b):
    return jnp.dot(a, b, preferred_element_type=F32)


def _split_dot_k2(x, w2_bf16):
    hi = x.astype(BF16)
    lo = (x - hi.astype(F32)).astype(BF16)
    return _dot(jnp.concatenate([hi, lo], axis=1), w2_bf16)


def _rms(x, g, eps):
    return x * lax.rsqrt(jnp.mean(x * x, axis=-1, keepdims=True) + eps) * g


def _ffn_body(*refs, tf, mix):
    if mix:
        x_ref, yr_ref, yd_ref, wo_ref, gmix_ref, *refs = refs
    else:
        x_ref, *refs = refs
    gpre_ref, gpost_ref, wg_ref, wu_ref, wd_ref, o_ref, a_ref = refs
    part = x_ref.shape[0] // FFN_PARTS
    parts = [slice(r * part, (r + 1) * part) for r in range(FFN_PARTS)]
    xs, hs = [], []
    for rows in parts:
        x = x_ref[rows]
        if mix:
            nr = yr_ref.shape[1]
            y = _dot(yr_ref[rows], wo_ref[:nr]) + _dot(yd_ref[rows], wo_ref[nr:])
            x = x + _rms(y, gmix_ref[...], NORM_EPS)
        xs.append(x)
        hs.append(_rms(x, gpre_ref[...], NORM_EPS).astype(BF16))
    for rows, x, h in zip(parts, xs, hs):
        for f0 in range(0, wg_ref.shape[1], tf):
            cols = slice(f0, f0 + tf)
            g = _dot(h, wg_ref[:, cols])
            u = _dot(h, wu_ref[:, cols])
            a_ref[rows, cols] = (g * jax.nn.sigmoid(g) * u).astype(BF16)
        y = _dot(a_ref[rows], wd_ref[...])
        o_ref[rows] = x + FFN_RES * _rms(y, gpost_ref[...], NORM_EPS)


def _ffn(x, g_pre, g_post, w_gate, w_up, w_down, *, tm, tf, mix=None):
    m, d = x.shape
    f = w_gate.shape[1]
    rows = lambda width: pl.BlockSpec((tm, width), lambda i: (i, 0))
    resident = lambda shape: pl.BlockSpec(shape, lambda i: (0, 0), pipeline_mode=pl.Buffered(1))
    args, specs = [x], [rows(d)]
    if mix is not None:
        y_rwkv, y_diff, w_o, g_mix = mix
        args += [y_rwkv, y_diff, w_o.astype(BF16), g_mix.reshape(1, d)]
        specs += [rows(y_rwkv.shape[1]), rows(y_diff.shape[1]), resident(w_o.shape), resident((1, d))]
    args += [g_pre.reshape(1, d), g_post.reshape(1, d), w_gate.astype(BF16), w_up.astype(BF16),
             w_down.astype(BF16)]
    specs += [resident((1, d)), resident((1, d)), resident((d, f)), resident((d, f)),
              resident((f, d))]
    return pl.pallas_call(
        functools.partial(_ffn_body, tf=tf, mix=mix is not None),
        grid=(m // tm,),
        in_specs=specs,
        out_specs=rows(d),
        out_shape=jax.ShapeDtypeStruct((m, d), F32),
        scratch_shapes=[pltpu.VMEM((tm, f), BF16)],
        compiler_params=pltpu.CompilerParams(
            dimension_semantics=("parallel",), vmem_limit_bytes=VMEM_LIMIT),
        name="ffn_mix" if mix is not None else "ffn",
    )(*args)


def _proj_body(x_ref, g_ref, wrkv_ref, wlora_ref, wdiff_ref, mu_ref, wup_ref, ones_ref, w0_ref,
               a0_ref, kk_ref, ka_ref, rk_ref,
               at_out, rt_out, bt_out, kt_out, bh_out, kh_out, v_out, gt_out, bonus_out, gate_out,
               qd_out, kd_out, vd_out, carry_ref, *, chunk):
    @pl.when(pl.program_id(1) == 0)
    def _():
        carry_ref[...] = jnp.zeros_like(carry_ref)

    tm = x_ref.shape[1]
    nrkv = 3 * RWKV_W
    first_row = lax.broadcasted_iota(jnp.int32, (SUBLANES, 1), 0) == 0
    lane = lax.broadcasted_iota(jnp.int32, (1, LORA_PAD), 1)
    ones_bd = ones_ref[...]

    def shift_mix(p, cols):
        prev = pltpu.roll(p, 1, 0)
        prev = jnp.concatenate(
            [jnp.where(first_row, carry_ref[:, cols], prev[:SUBLANES]), prev[SUBLANES:]], axis=0)
        carry_ref[:, cols] = p[tm - 1:]
        return p + (prev - p) * mu_ref[:, cols]

    h = _rms(x_ref[0], g_ref[...], NORM_EPS).astype(BF16)
    z = shift_mix(_dot(h, wlora_ref[...]), slice(nrkv, SHIFT_PAD))
    p_k = shift_mix(_dot(h, wrkv_ref[:, RWKV_W:2 * RWKV_W]), slice(RWKV_W, 2 * RWKV_W))
    act = jnp.where(lane < DECAY_LORA, jnp.tanh(z),
                    jnp.where(lane < DECAY_LORA + ICLR_LORA, z, jax.nn.sigmoid(z))).astype(BF16)
    up = _dot(act[:, :LANES], wup_ref[:LANES, :2 * RWKV_W])
    gate_out[0] = _dot(act[:, LANES:], wup_ref[LANES:, 2 * RWKV_W:])
    p_r = shift_mix(_dot(h, wrkv_ref[:, :RWKV_W]), slice(0, RWKV_W))
    p_v = shift_mix(_dot(h, wrkv_ref[:, 2 * RWKV_W:]), slice(2 * RWKV_W, nrkv))
    p_diff = _dot(h, wdiff_ref[...])

    ld = -EXP_NEG_HALF * jax.nn.sigmoid(w0_ref[...] + up[:, :RWKV_W])
    iclr = jax.nn.sigmoid(a0_ref[...] + up[:, RWKV_W:])
    kk = p_k * kk_ref[...]
    kk = kk * lax.rsqrt(jnp.maximum(_dot((kk * kk).astype(BF16), ones_bd), 1e-24))
    k = p_k * (iclr * ka_ref[...] + (1.0 - ka_ref[...]))
    b = kk * iclr
    v_out[0] = p_v.astype(BF16)
    bonus_out[0] = _dot((p_r * k * rk_ref[...]).astype(BF16), ones_bd) * p_v

    ri = lax.broadcasted_iota(jnp.int32, (chunk, chunk), 0)
    ci = lax.broadcasted_iota(jnp.int32, (chunk, chunk), 1)
    tril3 = jnp.concatenate([(ci <= ri).astype(BF16)] * 3, axis=1)
    ld1 = ld.astype(BF16)
    res = ld - ld1.astype(F32)
    ld2 = res.astype(BF16)
    ld3 = (res - ld2.astype(F32)).astype(BF16)
    cums, tots = [], []
    for i in range(tm // chunk):
        rows = slice(i * chunk, (i + 1) * chunk)
        cum = _dot(tril3, jnp.concatenate([ld1[rows], ld2[rows], ld3[rows]], axis=0))
        cums.append(cum)
        tots.append(jnp.broadcast_to(cum[chunk - 1:], cum.shape))
    gt_out[0] = jnp.exp(jnp.concatenate([t[:1] for t in tots], axis=0))
    cum = jnp.concatenate(cums, axis=0)
    g_inv = jnp.exp(-cum)
    g_rest = jnp.exp(jnp.concatenate(tots, axis=0) - cum)
    at_out[0] = (-kk * jnp.exp(cum - ld)).astype(BF16)
    rt_out[0] = (p_r * jnp.exp(cum)).astype(BF16)
    bt_out[0] = (b * g_inv).astype(BF16)
    kt_out[0] = (k * g_inv).astype(BF16)
    bh_out[0] = (b * g_rest).astype(BF16)
    kh_out[0] = (k * g_rest).astype(BF16)

    qd_out[0] = (p_diff[:, :DIFF_W] * (DIFF_QK ** -0.5 * LOG2E)).astype(BF16)
    kd_out[0] = p_diff[:, DIFF_W:2 * DIFF_W].astype(BF16)
    vd_out[0] = p_diff[:, 2 * DIFF_W:].T.astype(BF16)


def _head_ones(width, head):
    i = jnp.arange(width) // head
    return (i[:, None] == i[None, :]).astype(BF16)


def _proj(x, g, w_in, shift_mu, w_up, a_up, g_up, w0, a0, k_k, k_a, r_k, *, tm, chunk):
    b, t, d = x.shape
    pad = SHIFT_PAD - SHIFT_COLS
    nrkv = 3 * RWKV_W
    w = w_in.astype(BF16)
    w_lora = jnp.pad(w[:, nrkv:SHIFT_COLS], ((0, 0), (0, pad)))
    w_diff = w[:, SHIFT_COLS:]
    mu = jnp.pad(shift_mu, (0, pad)).reshape(1, SHIFT_PAD)
    wup = jnp.zeros((LORA_PAD, 3 * RWKV_W), F32)
    wup = wup.at[:DECAY_LORA, :RWKV_W].set(w_up)
    wup = wup.at[DECAY_LORA:DECAY_LORA + ICLR_LORA, RWKV_W:2 * RWKV_W].set(a_up)
    wup = wup.at[DECAY_LORA + ICLR_LORA:LORA_W, 2 * RWKV_W:].set(g_up)
    wup = wup.astype(BF16)
    vec = lambda a: a.reshape(1, RWKV_W)
    const = lambda shape: pl.BlockSpec(shape, lambda i, j: (0,) * len(shape))
    tile = lambda width: pl.BlockSpec((1, tm, width), lambda i, j: (i, j, 0))
    f32_out = jax.ShapeDtypeStruct((b, t, RWKV_W), F32)
    bf_out = jax.ShapeDtypeStruct((b, t, DIFF_W), BF16)
    assert RWKV_W == DIFF_W
    return pl.pallas_call(
        functools.partial(_proj_body, chunk=chunk),
        grid=(b, t // tm),
        in_specs=[tile(d), const((1, d)), const((d, nrkv)), const((d, LORA_PAD)),
                  const((d, 3 * DIFF_W)), const((1, SHIFT_PAD)),
                  const((LORA_PAD, 3 * RWKV_W)), const((RWKV_W, RWKV_W))] + [const((1, RWKV_W))] * 5,
        out_specs=[tile(RWKV_W)] * 7 + [pl.BlockSpec((1, tm // chunk, RWKV_W), lambda i, j: (i, j, 0))]
        + [tile(RWKV_W)] * 2 + [tile(DIFF_W)] * 2
        + [pl.BlockSpec((1, DIFF_W, tm), lambda i, j: (i, 0, j))],
        out_shape=[bf_out] * 7 + [jax.ShapeDtypeStruct((b, t // chunk, RWKV_W), F32)]
        + [f32_out] * 2 + [bf_out] * 2 + [jax.ShapeDtypeStruct((b, DIFF_W, t), BF16)],
        scratch_shapes=[pltpu.VMEM((1, SHIFT_PAD), F32)],
        compiler_params=pltpu.CompilerParams(
            dimension_semantics=("arbitrary", "arbitrary"), vmem_limit_bytes=VMEM_LIMIT),
        name="proj",
    )(x, g.reshape(1, d), w, w_lora, w_diff, mu, wup, _head_ones(RWKV_W, RWKV_HEAD),
      vec(w0), vec(a0), vec(k_k), vec(k_a), vec(r_k))


def _rwkv_body(at_ref, rt_ref, bt_ref, kt_ref, bh_ref, kh_ref, v_ref, gt_ref, bonus_ref, gate_ref,
               gnw_ref, gnb_ref, o_ref, h_ref, *, chunk, group_size):
    c = chunk
    c2 = 2 * c
    nc = at_ref.shape[1] // c

    @pl.when(pl.program_id(1) == 0)
    def _():
        h_ref[...] = jnp.zeros_like(h_ref)

    chunk_of = lambda ref: {i: ref[0, i * c:(i + 1) * c, :] for i in range(nc)}
    at, rt, bt, kt, bh, kh, v = (chunk_of(ref) for ref in
                                 (at_ref, rt_ref, bt_ref, kt_ref, bh_ref, kh_ref, v_ref))
    g_tot = {i: gt_ref[0, i:i + 1, :] for i in range(nc)}

    lane = lax.broadcasted_iota(jnp.int32, (1, LANES), 1)
    lo = lane < RWKV_HEAD

    def stack(x):
        zero = jnp.zeros_like(x)
        return jnp.concatenate([jnp.where(lo, x, zero), jnp.where(lo, zero, x)], axis=0)

    row2 = lax.broadcasted_iota(jnp.int32, (c2, c2), 0)
    col2 = lax.broadcasted_iota(jnp.int32, (c2, c2), 1)
    same = (row2 // c) == (col2 // c)
    strict = same & (col2 < row2)
    incl = same & (col2 <= row2)
    eye = (row2 == col2).astype(F32)
    levels = int(math.log2(c))
    level_masks = [((row2 >> l) == (col2 >> l)) & ((row2 >> (l - 1)) != (col2 >> (l - 1)))
                   & (col2 < row2) for l in range(1, levels + 1)]
    ones_bd = ((lax.broadcasted_iota(jnp.int32, (LANES, LANES), 0) // RWKV_HEAD)
               == (lax.broadcasted_iota(jnp.int32, (LANES, LANES), 1) // RWKV_HEAD)).astype(BF16)
    ones_bd2 = jnp.concatenate([ones_bd, ones_bd], axis=0)
    nt = lambda x, y: lax.dot_general(x, y, _NT, preferred_element_type=F32)
    tn = lambda x, y: lax.dot_general(x, y, _TN, preferred_element_type=F32)

    npairs = RWKV_HEADS // 2
    sls = [slice(p * LANES, (p + 1) * LANES) for p in range(npairs)]
    cat0 = lambda *xs: jnp.concatenate(xs, axis=0)
    cat1 = lambda *xs: jnp.concatenate(xs, axis=1)
    assert c2 == LANES
    qp, y0, gm, g_col = {}, {}, {}, {}

    for group in [range(g, min(g + group_size, nc)) for g in range(0, nc, group_size)]:
        chains = [(i, p) for i in group for p in range(npairs)]
        each = lambda f: {ip: f(ip) for ip in chains}
        pick = lambda xs: each(lambda ip: stack(xs[ip[0]][:, sls[ip[1]]]))
        a2, r2, b2, k2, bh2, kh2, v2 = (pick(xs) for xs in (at, rt, bt, kt, bh, kh, v))

        s_ar = each(lambda ip: nt(cat0(a2[ip], r2[ip]), cat0(b2[ip], k2[ip])))
        a_ab = each(lambda ip: jnp.where(strict, s_ar[ip][:c2, :c2], 0.0))
        a_ak = each(lambda ip: jnp.where(strict, s_ar[ip][:c2, c2:], 0.0).astype(BF16))
        a_rb = each(lambda ip: jnp.where(incl, s_ar[ip][c2:, :c2], 0.0).astype(BF16))
        a_rk = each(lambda ip: jnp.where(incl, s_ar[ip][c2:, c2:], 0.0).astype(BF16))

        tinv = each(lambda ip: eye + jnp.where(level_masks[0], a_ab[ip], 0.0))
        for l in range(2, levels + 1):
            e = each(lambda ip: jnp.where(level_masks[l - 1], a_ab[ip], 0.0).astype(BF16))
            tb = each(lambda ip: tinv[ip].astype(BF16))
            te = each(lambda ip: _dot(tb[ip], e[ip]).astype(BF16))
            tinv = each(lambda ip: tinv[ip] + _dot(te[ip], tb[ip]))
        tb = each(lambda ip: tinv[ip].astype(BF16))

        av = each(lambda ip: _dot(cat0(a_ak[ip], a_rk[ip]), v2[ip]))
        akv = each(lambda ip: av[ip][:c2].astype(BF16))
        wu = each(lambda ip: _dot(tb[ip], cat1(a2[ip], akv[ip])).astype(BF16))
        pg = each(lambda ip: tn(bh2[ip], wu[ip]))
        qy = each(lambda ip: _dot(a_rb[ip], wu[ip]))
        gm.update(each(lambda ip: pg[ip][:, LANES:] + tn(kh2[ip], v2[ip])))
        qp.update(each(lambda ip: cat0((r2[ip].astype(F32) + qy[ip][:, :LANES]).astype(BF16),
                                       pg[ip][:, :LANES].astype(BF16))))
        y0.update(each(lambda ip: qy[ip][:, LANES:] + av[ip][c2:]))
        g_col.update(each(
            lambda ip: jnp.sum(eye * g_tot[ip[0]][:, sls[ip[1]]], axis=1, keepdims=True)))

    state = [h_ref[p] for p in range(npairs)]
    ys = [[] for _ in range(npairs)]
    for i in range(nc):
        for p in range(npairs):
            qh = _dot(qp[i, p], state[p].astype(BF16))
            y2 = qh[:c2] + y0[i, p]
            ys[p].append(y2[:c] + y2[c:])
            state[p] = g_col[i, p] * state[p] + qh[c2:] + gm[i, p]
    for p in range(npairs):
        h_ref[p] = state[p]

    for p in range(npairs):
        sl = sls[p]
        y = cat0(*ys[p])
        mean = _split_dot_k2(y, ones_bd2) * (1.0 / RWKV_HEAD)
        yc = y - mean
        var = _split_dot_k2(yc * yc, ones_bd2) * (1.0 / RWKV_HEAD)
        yn = yc * lax.rsqrt(var + GN_EPS) * gnw_ref[:, sl] + gnb_ref[:, sl]
        o_ref[0, :, sl] = ((yn + bonus_ref[0, :, sl]) * gate_ref[0, :, sl]).astype(o_ref.dtype)


def _rwkv(at, rt, bt, kt, bh, kh, v, gt, bonus, gate, gn_w, gn_b, *, chunk, tb):
    b, t, w = at.shape
    tile = pl.BlockSpec((1, tb, w), lambda i, j: (i, j, 0))
    const = pl.BlockSpec((1, w), lambda i, j: (0, 0))
    return pl.pallas_call(
        functools.partial(_rwkv_body, chunk=chunk, group_size=RWKV_GROUP),
        grid=(b, t // tb),
        in_specs=[tile] * 7 + [pl.BlockSpec((1, tb // chunk, w), lambda i, j: (i, j, 0))]
        + [tile] * 2 + [const] * 2,
        out_specs=tile,
        out_shape=jax.ShapeDtypeStruct((b, t, w), BF16),
        scratch_shapes=[pltpu.VMEM((RWKV_HEADS // 2, LANES, LANES), F32)],
        compiler_params=pltpu.CompilerParams(
            dimension_semantics=("arbitrary", "arbitrary"), vmem_limit_bytes=VMEM_LIMIT),
        name="rwkv",
    )(at, rt, bt, kt, bh, kh, v, gt, bonus, gate, gn_w.reshape(1, w), gn_b.reshape(1, w))


def _attn_body(lq1_ref, lk1_ref, lq2_ref, lk2_ref, q_ref, k_ref, vt_ref, sw_ref, o_ref, acc_ref,
               s_ref, m_ref, *, tq, tk, wq, lambda_init):
    nq = q_ref.shape[1] // tq
    kpq = tq // tk
    lane = lax.broadcasted_iota(jnp.int32, (1, LANES), 1)
    lam = (jnp.exp(jnp.sum(lq1_ref[...] * lk1_ref[...], axis=-1, keepdims=True))
           - jnp.exp(jnp.sum(lq2_ref[...] * lk2_ref[...], axis=-1, keepdims=True)) + lambda_init)
    strips = [(c, w) for c in range(2) for w in range(tq // wq)]
    items = [(qi, j) for qi in range(nq) for j in range(kpq * (qi + 1))]
    ones_rows = jnp.ones((BF16_ROWS, tk), BF16)
    qs = {}

    def masked_q(qi):
        if qi not in qs:
            q = q_ref[0, qi * tq:(qi + 1) * tq]
            zero = jnp.zeros_like(q)
            qs[qi] = (jnp.where(lane < DIFF_QK, q, zero), jnp.where(lane < DIFF_QK, zero, q))
        return qs[qi]

    def scores_into(slot, qi, j):
        q_from = max(0, j * tk - qi * tq)
        ks = k_ref[0, j * tk:(j + 1) * tk, :]
        for c in range(2):
            s_ref[slot, c, :, q_from:] = lax.dot_general(ks, masked_q(qi)[c][q_from:], _NT,
                                                         preferred_element_type=F32)

    def consume(slot, qi, j):
        off = j * tk - qi * tq
        for c, w in strips:
            nk = min(tk, max(0, (w + 1) * wq - off))
            if nk == 0:
                continue
            cols = slice(w * wq, (w + 1) * wq)
            s = s_ref[slot, c, 0:nk, cols]
            if off + nk - 1 > w * wq:
                key = lax.broadcasted_iota(jnp.int32, (nk, wq), 0) + off
                qry = lax.broadcasted_iota(jnp.int32, (nk, wq), 1) + w * wq
                s = jnp.where(key <= qry, s, NEG_BIG)
            m = m_ref[qi, c, :, cols]
            m_new = jnp.maximum(m, jnp.max(s, axis=0, keepdims=True))
            alpha = jnp.exp2(m - m_new)
            pr = jnp.exp2(s - m_new)
            m_ref[qi, c, :, cols] = m_new
            vt = jnp.concatenate([vt_ref[0, :, j * tk:j * tk + nk], ones_rows[:, :nk]], axis=0)
            acc_ref[qi, c, :, cols] = alpha * acc_ref[qi, c, :, cols] + _dot(vt, pr.astype(BF16))

    def finish(qi):
        l1, l2 = (acc_ref[qi, c, DIFF_V:DIFF_V + 1] for c in range(2))
        o = (acc_ref[qi, 0, :DIFF_V] * (1.0 / l1)
             - lam * (acc_ref[qi, 1, :DIFF_V] * (1.0 / l2)))
        o = o * lax.rsqrt(jnp.mean(o * o, axis=0, keepdims=True) + SUBLN_EPS) * sw_ref[...]
        o = o * (1.0 - lambda_init)
        o_ref[0, qi * tq:(qi + 1) * tq] = o.T.astype(o_ref.dtype)

    acc_ref[...] = jnp.zeros_like(acc_ref)
    m_ref[...] = jnp.full_like(m_ref, NEG_BIG)
    scores_into(0, *items[0])
    for n, (qi, j) in enumerate(items):
        if n + 1 < len(items):
            scores_into((n + 1) % 2, *items[n + 1])
        consume(n % 2, qi, j)
        if j == kpq * (qi + 1) - 1:
            finish(qi)


def _attn(q, k, vt, lq1, lk1, lq2, lk2, subln_w, *, tq, tk, wq, lambda_init):
    b, t, w = q.shape
    heads = w // DIFF_V
    nq = t // tq
    lam_spec = pl.BlockSpec((1, DIFF_QK), lambda i, h: (0, 0))
    return pl.pallas_call(
        functools.partial(_attn_body, tq=tq, tk=tk, wq=wq, lambda_init=lambda_init),
        grid=(b, heads),
        in_specs=[lam_spec] * 4 + [
            pl.BlockSpec((1, t, DIFF_V), lambda i, h: (i, 0, h)),
            pl.BlockSpec((1, t, DIFF_V), lambda i, h: (i, 0, h)),
            pl.BlockSpec((1, DIFF_V, t), lambda i, h: (i, h, 0)),
            pl.BlockSpec((DIFF_V, 1), lambda i, h: (0, 0)),
        ],
        out_specs=pl.BlockSpec((1, t, DIFF_V), lambda i, h: (i, 0, h)),
        out_shape=jax.ShapeDtypeStruct((b, t, w), BF16),
        scratch_shapes=[pltpu.VMEM((nq, 2, DIFF_V + BF16_ROWS, tq), F32),
                        pltpu.VMEM((2, 2, tk, tq), F32), pltpu.VMEM((nq, 2, 1, tq), F32)],
        compiler_params=pltpu.CompilerParams(
            dimension_semantics=("parallel", "parallel"), vmem_limit_bytes=VMEM_LIMIT),
        name="attn",
    )(lq1.reshape(1, -1), lk1.reshape(1, -1), lq2.reshape(1, -1), lk2.reshape(1, -1),
      q, k, vt, subln_w.reshape(-1, 1))


def _pick(n, pref):
    return pref if n % pref == 0 else n


def _layer(x, l, p, *, chunk=64, tb_rwkv=512, tm_ffn=1024, tf=256, tm_proj=512, tq=1024,
           tk=512, wq_attn=256):
    b, t, d = x.shape
    m = b * t
    lambda_init = 0.8 - 0.6 * math.exp(-0.3 * l)
    tm_ffn = _pick(m, tm_ffn)
    tf = _pick(p["ffn1_w_gate"].shape[-1], tf)
    x = _ffn(x.reshape(m, d), p["ffn1_pre_g"][l], p["ffn1_post_g"][l], p["ffn1_w_gate"][l],
             p["ffn1_w_up"][l], p["ffn1_w_down"][l], tm=tm_ffn, tf=tf)
    outs = _proj(x.reshape(b, t, d), p["mix_pre_g"][l], p["w_in"][l], p["shift_mu"][l],
                 p["rwkv_w_up"][l], p["rwkv_a_up"][l], p["rwkv_g_up"][l], p["rwkv_w0"][l],
                 p["rwkv_a0"][l], p["rwkv_k_k"][l], p["rwkv_k_a"][l], p["rwkv_r_k"][l].reshape(-1),
                 tm=_pick(t, tm_proj), chunk=chunk)
    y_rwkv = _rwkv(*outs[:10], p["rwkv_gn_w"][l], p["rwkv_gn_b"][l], chunk=chunk,
                   tb=_pick(t, tb_rwkv))
    y_diff = _attn(*outs[10:], p["diff_lam_q1"][l], p["diff_lam_k1"][l], p["diff_lam_q2"][l],
                   p["diff_lam_k2"][l], p["diff_subln_w"][l], tq=_pick(t, tq), tk=_pick(t, tk), wq=wq_attn,
                   lambda_init=lambda_init)
    x = _ffn(x, p["ffn2_pre_g"][l], p["ffn2_post_g"][l], p["ffn2_w_gate"][l], p["ffn2_w_up"][l],
             p["ffn2_w_down"][l], tm=tm_ffn, tf=tf,
             mix=(y_rwkv.reshape(m, -1), y_diff.reshape(m, -1), p["w_o"][l], p["mix_post_g"][l]))
    return x.reshape(b, t, d)


def kernel(x, ffn1_pre_g, ffn1_post_g, ffn1_w_gate, ffn1_w_up, ffn1_w_down, mix_pre_g, mix_post_g,
           w_in, shift_mu, w_o, rwkv_w0, rwkv_w_up, rwkv_a0, rwkv_a_up, rwkv_g_up, rwkv_k_k,
           rwkv_k_a, rwkv_r_k, rwkv_gn_w, rwkv_gn_b, diff_lam_q1, diff_lam_k1, diff_lam_q2,
           diff_lam_k2, diff_subln_w, ffn2_pre_g, ffn2_post_g, ffn2_w_gate, ffn2_w_up, ffn2_w_down):
    p = dict(ffn1_pre_g=ffn1_pre_g, ffn1_post_g=ffn1_post_g, ffn1_w_gate=ffn1_w_gate,
             ffn1_w_up=ffn1_w_up, ffn1_w_down=ffn1_w_down, mix_pre_g=mix_pre_g,
             mix_post_g=mix_post_g, w_in=w_in, shift_mu=shift_mu, w_o=w_o, rwkv_w0=rwkv_w0,
             rwkv_w_up=rwkv_w_up, rwkv_a0=rwkv_a0, rwkv_a_up=rwkv_a_up, rwkv_g_up=rwkv_g_up,
             rwkv_k_k=rwkv_k_k, rwkv_k_a=rwkv_k_a, rwkv_r_k=rwkv_r_k, rwkv_gn_w=rwkv_gn_w,
             rwkv_gn_b=rwkv_gn_b, diff_lam_q1=diff_lam_q1, diff_lam_k1=diff_lam_k1,
             diff_lam_q2=diff_lam_q2, diff_lam_k2=diff_lam_k2, diff_subln_w=diff_subln_w,
             ffn2_pre_g=ffn2_pre_g, ffn2_post_g=ffn2_post_g, ffn2_w_gate=ffn2_w_gate,
             ffn2_w_up=ffn2_w_up, ffn2_w_down=ffn2_w_down)
    for l in range(ffn1_pre_g.shape[0]):
        x = _layer(x, l, p)
    return x
```

```python
import functools
import math

import jax
import jax.numpy as jnp
from jax import lax
from jax.experimental import pallas as pl
from jax.experimental.pallas import tpu as pltpu

F32 = jnp.float32
BF16 = jnp.bfloat16

RWKV_HEADS = 8
RWKV_HEAD = 64
RWKV_W = RWKV_HEADS * RWKV_HEAD
DIFF_HEADS = 4
DIFF_QK = 64
DIFF_V = 2 * DIFF_QK
DIFF_W = DIFF_HEADS * DIFF_V
DECAY_LORA = 64
ICLR_LORA = 64
GATE_LORA = 160
LORA_W = DECAY_LORA + ICLR_LORA + GATE_LORA
LANES = 128
SUBLANES = 8
BF16_ROWS = 16
FFN_PARTS = 2
RWKV_GROUP = 4
LORA_PAD = -(-LORA_W // LANES) * LANES
SHIFT_COLS = 3 * RWKV_W + LORA_W
SHIFT_PAD = 3 * RWKV_W + LORA_PAD
FFN_RES = 0.5
NORM_EPS = 1e-6
GN_EPS = 64e-5
SUBLN_EPS = 1e-5
NEG_BIG = -1e30
LOG2E = math.log2(math.e)
EXP_NEG_HALF = math.exp(-0.5)
VMEM_LIMIT = 56 * 1024 * 1024

_NT = (((1,), (1,)), ((), ()))
_TN = (((0,), (0,)), ((), ()))


def _dot(a, b):
    return jnp.dot(a, b, preferred_element_type=F32)


def _split_dot_k2(x, w2_bf16):
    hi = x.astype(BF16)
    lo = (x - hi.astype(F32)).astype(BF16)
    return _dot(jnp.concatenate([hi, lo], axis=1), w2_bf16)


def _rms(x, g, eps):
    return x * lax.rsqrt(jnp.mean(x * x, axis=-1, keepdims=True) + eps) * g


def _ffn_body(*refs, tf, mix):
    if mix:
        x_ref, yr_ref, yd_ref, wo_ref, gmix_ref, *refs = refs
    else:
        x_ref, *refs = refs
    gpre_ref, gpost_ref, wg_ref, wu_ref, wd_ref, o_ref, a_ref = refs
    nparts = FFN_PARTS if mix else 1
    part = x_ref.shape[0] // nparts
    parts = [slice(r * part, (r + 1) * part) for r in range(nparts)]
    xs, hs = [], []
    for rows in parts:
        x = x_ref[rows]
        if mix:
            nr = yr_ref.shape[1]
            y = _dot(yr_ref[rows], wo_ref[:nr]) + _dot(yd_ref[rows], wo_ref[nr:])
            x = x + _rms(y, gmix_ref[...], NORM_EPS)
        xs.append(x)
        hs.append(_rms(x, gpre_ref[...], NORM_EPS).astype(BF16))
    for rows, x, h in zip(parts, xs, hs):
        for f0 in range(0, wg_ref.shape[1], tf):
            cols = slice(f0, f0 + tf)
            g = _dot(h, wg_ref[:, cols])
            u = _dot(h, wu_ref[:, cols])
            a_ref[rows, cols] = (g * jax.nn.sigmoid(g) * u).astype(BF16)
        y = _dot(a_ref[rows], wd_ref[...])
        o_ref[rows] = x + FFN_RES * _rms(y, gpost_ref[...], NORM_EPS)


def _ffn(x, g_pre, g_post, w_gate, w_up, w_down, *, tm, tf, mix=None):
    m, d = x.shape
    f = w_gate.shape[1]
    rows = lambda width: pl.BlockSpec((tm, width), lambda i: (i, 0))
    resident = lambda shape: pl.BlockSpec(shape, lambda i: (0, 0), pipeline_mode=pl.Buffered(1))
    args, specs = [x], [rows(d)]
    if mix is not None:
        y_rwkv, y_diff, w_o, g_mix = mix
        args += [y_rwkv, y_diff, w_o.astype(BF16), g_mix.reshape(1, d)]
        specs += [rows(y_rwkv.shape[1]), rows(y_diff.shape[1]), resident(w_o.shape), resident((1, d))]
    args += [g_pre.reshape(1, d), g_post.reshape(1, d), w_gate.astype(BF16), w_up.astype(BF16),
             w_down.astype(BF16)]
    specs += [resident((1, d)), resident((1, d)), resident((d, f)), resident((d, f)),
              resident((f, d))]
    return pl.pallas_call(
        functools.partial(_ffn_body, tf=tf, mix=mix is not None),
        grid=(m // tm,),
        in_specs=specs,
        out_specs=rows(d),
        out_shape=jax.ShapeDtypeStruct((m, d), F32),
        scratch_shapes=[pltpu.VMEM((tm, f), BF16)],
        compiler_params=pltpu.CompilerParams(
            dimension_semantics=("parallel",), vmem_limit_bytes=VMEM_LIMIT),
        name="ffn_mix" if mix is not None else "ffn",
    )(*args)


def _proj_body(x_ref, g_ref, wrkv_ref, wlora_ref, wdiff_ref, mu_ref, wup_ref, ones_ref, w0_ref,
               a0_ref, kk_ref, ka_ref, rk_ref,
               at_out, rt_out, bt_out, kt_out, bh_out, kh_out, v_out, gt_out, bonus_out, gate_out,
               qd_out, kd_out, vd_out, carry_ref, *, chunk):
    @pl.when(pl.program_id(1) == 0)
    def _():
        carry_ref[...] = jnp.zeros_like(carry_ref)

    tm = x_ref.shape[1]
    nrkv = 3 * RWKV_W
    first_row = lax.broadcasted_iota(jnp.int32, (SUBLANES, 1), 0) == 0
    lane = lax.broadcasted_iota(jnp.int32, (1, LORA_PAD), 1)
    ones_bd = ones_ref[...]

    def shift_mix(p, cols):
        prev = pltpu.roll(p, 1, 0)
        prev = jnp.concatenate(
            [jnp.where(first_row, carry_ref[:, cols], prev[:SUBLANES]), prev[SUBLANES:]], axis=0)
        carry_ref[:, cols] = p[tm - 1:]
        return p + (prev - p) * mu_ref[:, cols]

    h = _rms(x_ref[0], g_ref[...], NORM_EPS).astype(BF16)
    z = shift_mix(_dot(h, wlora_ref[...]), slice(nrkv, SHIFT_PAD))
    p_k = shift_mix(_dot(h, wrkv_ref[:, RWKV_W:2 * RWKV_W]), slice(RWKV_W, 2 * RWKV_W))
    act = jnp.where(lane < DECAY_LORA, jnp.tanh(z),
                    jnp.where(lane < DECAY_LORA + ICLR_LORA, z, jax.nn.sigmoid(z))).astype(BF16)
    up = _dot(act[:, :LANES], wup_ref[:LANES, :2 * RWKV_W])
    gate_out[0] = _dot(act[:, LANES:], wup_ref[LANES:, 2 * RWKV_W:])
    p_r = shift_mix(_dot(h, wrkv_ref[:, :RWKV_W]), slice(0, RWKV_W))
    p_v = shift_mix(_dot(h, wrkv_ref[:, 2 * RWKV_W:]), slice(2 * RWKV_W, nrkv))
    p_diff = _dot(h, wdiff_ref[...])

    ld = -EXP_NEG_HALF * jax.nn.sigmoid(w0_ref[...] + up[:, :RWKV_W])
    iclr = jax.nn.sigmoid(a0_ref[...] + up[:, RWKV_W:])
    kk = p_k * kk_ref[...]
    kk = kk * lax.rsqrt(jnp.maximum(_dot((kk * kk).astype(BF16), ones_bd), 1e-24))
    k = p_k * (iclr * ka_ref[...] + (1.0 - ka_ref[...]))
    b = kk * iclr
    v_out[0] = p_v.astype(BF16)
    bonus_out[0] = _dot((p_r * k * rk_ref[...]).astype(BF16), ones_bd) * p_v

    ri = lax.broadcasted_iota(jnp.int32, (chunk, chunk), 0)
    ci = lax.broadcasted_iota(jnp.int32, (chunk, chunk), 1)
    tril3 = jnp.concatenate([(ci <= ri).astype(BF16)] * 3, axis=1)
    ld1 = ld.astype(BF16)
    res = ld - ld1.astype(F32)
    ld2 = res.astype(BF16)
    ld3 = (res - ld2.astype(F32)).astype(BF16)
    cums, tots = [], []
    for i in range(tm // chunk):
        rows = slice(i * chunk, (i + 1) * chunk)
        cum = _dot(tril3, jnp.concatenate([ld1[rows], ld2[rows], ld3[rows]], axis=0))
        cums.append(cum)
        tots.append(jnp.broadcast_to(cum[chunk - 1:], cum.shape))
    gt_out[0] = jnp.exp(jnp.concatenate([t[:1] for t in tots], axis=0))
    cum = jnp.concatenate(cums, axis=0)
    g_inv = jnp.exp(-cum)
    g_rest = jnp.exp(jnp.concatenate(tots, axis=0) - cum)
    at_out[0] = (-kk * jnp.exp(cum - ld)).astype(BF16)
    rt_out[0] = (p_r * jnp.exp(cum)).astype(BF16)
    bt_out[0] = (b * g_inv).astype(BF16)
    kt_out[0] = (k * g_inv).astype(BF16)
    bh_out[0] = (b * g_rest).astype(BF16)
    kh_out[0] = (k * g_rest).astype(BF16)

    qd_out[0] = (p_diff[:, :DIFF_W] * (DIFF_QK ** -0.5 * LOG2E)).astype(BF16)
    kd_out[0] = p_diff[:, DIFF_W:2 * DIFF_W].astype(BF16)
    vd_out[0] = p_diff[:, 2 * DIFF_W:].T.astype(BF16)


def _head_ones(width, head):
    i = jnp.arange(width) // head
    return (i[:, None] == i[None, :]).astype(BF16)


def _proj(x, g, w_in, shift_mu, w_up, a_up, g_up, w0, a0, k_k, k_a, r_k, *, tm, chunk):
    b, t, d = x.shape
    pad = SHIFT_PAD - SHIFT_COLS
    nrkv = 3 * RWKV_W
    w = w_in.astype(BF16)
    w_lora = jnp.pad(w[:, nrkv:SHIFT_COLS], ((0, 0), (0, pad)))
    w_diff = w[:, SHIFT_COLS:]
    mu = jnp.pad(shift_mu, (0, pad)).reshape(1, SHIFT_PAD)
    wup = jnp.zeros((LORA_PAD, 3 * RWKV_W), F32)
    wup = wup.at[:DECAY_LORA, :RWKV_W].set(w_up)
    wup = wup.at[DECAY_LORA:DECAY_LORA + ICLR_LORA, RWKV_W:2 * RWKV_W].set(a_up)
    wup = wup.at[DECAY_LORA + ICLR_LORA:LORA_W, 2 * RWKV_W:].set(g_up)
    wup = wup.astype(BF16)
    vec = lambda a: a.reshape(1, RWKV_W)
    const = lambda shape: pl.BlockSpec(shape, lambda i, j: (0,) * len(shape))
    tile = lambda width: pl.BlockSpec((1, tm, width), lambda i, j: (i, j, 0))
    f32_out = jax.ShapeDtypeStruct((b, t, RWKV_W), F32)
    bf_out = jax.ShapeDtypeStruct((b, t, DIFF_W), BF16)
    assert RWKV_W == DIFF_W
    return pl.pallas_call(
        functools.partial(_proj_body, chunk=chunk),
        grid=(b, t // tm),
        in_specs=[tile(d), const((1, d)), const((d, nrkv)), const((d, LORA_PAD)),
                  const((d, 3 * DIFF_W)), const((1, SHIFT_PAD)),
                  const((LORA_PAD, 3 * RWKV_W)), const((RWKV_W, RWKV_W))] + [const((1, RWKV_W))] * 5,
        out_specs=[tile(RWKV_W)] * 7 + [pl.BlockSpec((1, tm // chunk, RWKV_W), lambda i, j: (i, j, 0))]
        + [tile(RWKV_W)] * 2 + [tile(DIFF_W)] * 2
        + [pl.BlockSpec((1, DIFF_W, tm), lambda i, j: (i, 0, j))],
        out_shape=[bf_out] * 7 + [jax.ShapeDtypeStruct((b, t // chunk, RWKV_W), F32)]
        + [f32_out] * 2 + [bf_out] * 2 + [jax.ShapeDtypeStruct((b, DIFF_W, t), BF16)],
        scratch_shapes=[pltpu.VMEM((1, SHIFT_PAD), F32)],
        compiler_params=pltpu.CompilerParams(
            dimension_semantics=("arbitrary", "arbitrary"), vmem_limit_bytes=VMEM_LIMIT),
        name="proj",
    )(x, g.reshape(1, d), w, w_lora, w_diff, mu, wup, _head_ones(RWKV_W, RWKV_HEAD),
      vec(w0), vec(a0), vec(k_k), vec(k_a), vec(r_k))


def _rwkv_body(at_ref, rt_ref, bt_ref, kt_ref, bh_ref, kh_ref, v_ref, gt_ref, bonus_ref, gate_ref,
               gnw_ref, gnb_ref, o_ref, h_ref, *, chunk, group_size):
    c = chunk
    c2 = 2 * c
    nc = at_ref.shape[1] // c

    @pl.when(pl.program_id(1) == 0)
    def _():
        h_ref[...] = jnp.zeros_like(h_ref)

    chunk_of = lambda ref: {i: ref[0, i * c:(i + 1) * c, :] for i in range(nc)}
    at, rt, bt, kt, bh, kh, v = (chunk_of(ref) for ref in
                                 (at_ref, rt_ref, bt_ref, kt_ref, bh_ref, kh_ref, v_ref))
    g_tot = {i: gt_ref[0, i:i + 1, :] for i in range(nc)}

    lane = lax.broadcasted_iota(jnp.int32, (1, LANES), 1)
    lo = lane < RWKV_HEAD

    def stack(x):
        zero = jnp.zeros_like(x)
        return jnp.concatenate([jnp.where(lo, x, zero), jnp.where(lo, zero, x)], axis=0)

    row2 = lax.broadcasted_iota(jnp.int32, (c2, c2), 0)
    col2 = lax.broadcasted_iota(jnp.int32, (c2, c2), 1)
    same = (row2 // c) == (col2 // c)
    strict = same & (col2 < row2)
    incl = same & (col2 <= row2)
    eye = (row2 == col2).astype(F32)
    levels = int(math.log2(c))
    level_masks = [((row2 >> l) == (col2 >> l)) & ((row2 >> (l - 1)) != (col2 >> (l - 1)))
                   & (col2 < row2) for l in range(1, levels + 1)]
    ones_bd = ((lax.broadcasted_iota(jnp.int32, (LANES, LANES), 0) // RWKV_HEAD)
               == (lax.broadcasted_iota(jnp.int32, (LANES, LANES), 1) // RWKV_HEAD)).astype(BF16)
    ones_bd2 = jnp.concatenate([ones_bd, ones_bd], axis=0)
    nt = lambda x, y: lax.dot_general(x, y, _NT, preferred_element_type=F32)
    tn = lambda x, y: lax.dot_general(x, y, _TN, preferred_element_type=F32)

    npairs = RWKV_HEADS // 2
    sls = [slice(p * LANES, (p + 1) * LANES) for p in range(npairs)]
    cat0 = lambda *xs: jnp.concatenate(xs, axis=0)
    cat1 = lambda *xs: jnp.concatenate(xs, axis=1)
    assert c2 == LANES
    qp, y0, gm, g_col = {}, {}, {}, {}

    for group in [range(g, min(g + group_size, nc)) for g in range(0, nc, group_size)]:
        chains = [(i, p) for i in group for p in range(npairs)]
        each = lambda f: {ip: f(ip) for ip in chains}
        pick = lambda xs: each(lambda ip: stack(xs[ip[0]][:, sls[ip[1]]]))
        a2, r2, b2, k2, bh2, kh2, v2 = (pick(xs) for xs in (at, rt, bt, kt, bh, kh, v))

        s_ar = each(lambda ip: nt(cat0(a2[ip], r2[ip]), cat0(b2[ip], k2[ip])))
        a_ab = each(lambda ip: jnp.where(strict, s_ar[ip][:c2, :c2], 0.0))
        a_ak = each(lambda ip: jnp.where(strict, s_ar[ip][:c2, c2:], 0.0).astype(BF16))
        a_rb = each(lambda ip: jnp.where(incl, s_ar[ip][c2:, :c2], 0.0).astype(BF16))
        a_rk = each(lambda ip: jnp.where(incl, s_ar[ip][c2:, c2:], 0.0).astype(BF16))

        tinv = each(lambda ip: eye + jnp.where(level_masks[0], a_ab[ip], 0.0))
        for l in range(2, levels + 1):
            e = each(lambda ip: jnp.where(level_masks[l - 1], a_ab[ip], 0.0).astype(BF16))
            tb = each(lambda ip: tinv[ip].astype(BF16))
            te = each(lambda ip: _dot(tb[ip], e[ip]).astype(BF16))
            tinv = each(lambda ip: tinv[ip] + _dot(te[ip], tb[ip]))
        tb = each(lambda ip: tinv[ip].astype(BF16))

        av = each(lambda ip: _dot(cat0(a_ak[ip], a_rk[ip]), v2[ip]))
        akv = each(lambda ip: av[ip][:c2].astype(BF16))
        wu = each(lambda ip: _dot(tb[ip], cat1(a2[ip], akv[ip])).astype(BF16))
        pg = each(lambda ip: tn(bh2[ip], wu[ip]))
        qy = each(lambda ip: _dot(a_rb[ip], wu[ip]))
        gm.update(each(lambda ip: pg[ip][:, LANES:] + tn(kh2[ip], v2[ip])))
        qp.update(each(lambda ip: cat0((r2[ip].astype(F32) + qy[ip][:, :LANES]).astype(BF16),
                                       pg[ip][:, :LANES].astype(BF16))))
        y0.update(each(lambda ip: qy[ip][:, LANES:] + av[ip][c2:]))
        g_col.update(each(
            lambda ip: jnp.sum(eye * g_tot[ip[0]][:, sls[ip[1]]], axis=1, keepdims=True)))

    state = [h_ref[p] for p in range(npairs)]
    ys = [[] for _ in range(npairs)]
    for i in range(nc):
        for p in range(npairs):
            qh = _dot(qp[i, p], state[p].astype(BF16))
            y2 = qh[:c2] + y0[i, p]
            ys[p].append(y2[:c] + y2[c:])
            state[p] = g_col[i, p] * state[p] + qh[c2:] + gm[i, p]
    for p in range(npairs):
        h_ref[p] = state[p]

    for p in range(npairs):
        sl = sls[p]
        y = cat0(*ys[p])
        mean = _split_dot_k2(y, ones_bd2) * (1.0 / RWKV_HEAD)
        yc = y - mean
        var = _split_dot_k2(yc * yc, ones_bd2) * (1.0 / RWKV_HEAD)
        yn = yc * lax.rsqrt(var + GN_EPS) * gnw_ref[:, sl] + gnb_ref[:, sl]
        o_ref[0, :, sl] = ((yn + bonus_ref[0, :, sl]) * gate_ref[0, :, sl]).astype(o_ref.dtype)


def _rwkv(at, rt, bt, kt, bh, kh, v, gt, bonus, gate, gn_w, gn_b, *, chunk, tb):
    b, t, w = at.shape
    tile = pl.BlockSpec((1, tb, w), lambda i, j: (i, j, 0))
    const = pl.BlockSpec((1, w), lambda i, j: (0, 0))
    return pl.pallas_call(
        functools.partial(_rwkv_body, chunk=chunk, group_size=RWKV_GROUP),
        grid=(b, t // tb),
        in_specs=[tile] * 7 + [pl.BlockSpec((1, tb // chunk, w), lambda i, j: (i, j, 0))]
        + [tile] * 2 + [const] * 2,
        out_specs=tile,
        out_shape=jax.ShapeDtypeStruct((b, t, w), BF16),
        scratch_shapes=[pltpu.VMEM((RWKV_HEADS // 2, LANES, LANES), F32)],
        compiler_params=pltpu.CompilerParams(
            dimension_semantics=("arbitrary", "arbitrary"), vmem_limit_bytes=VMEM_LIMIT),
        name="rwkv",
    )(at, rt, bt, kt, bh, kh, v, gt, bonus, gate, gn_w.reshape(1, w), gn_b.reshape(1, w))


def _attn_body(lq1_ref, lk1_ref, lq2_ref, lk2_ref, q_ref, k_ref, vt_ref, sw_ref, o_ref, acc_ref,
               s_ref, m_ref, *, tq, tk, wq, lambda_init):
    nq = q_ref.shape[1] // tq
    kpq = tq // tk
    lane = lax.broadcasted_iota(jnp.int32, (1, LANES), 1)
    lam = (jnp.exp(jnp.sum(lq1_ref[...] * lk1_ref[...], axis=-1, keepdims=True))
           - jnp.exp(jnp.sum(lq2_ref[...] * lk2_ref[...], axis=-1, keepdims=True)) + lambda_init)
    strips = [(c, w) for c in range(2) for w in range(tq // wq)]
    items = [(qi, j) for qi in range(nq) for j in range(kpq * (qi + 1))]
    ones_rows = jnp.ones((BF16_ROWS, tk), BF16)
    qs = {}

    def masked_q(qi):
        if qi not in qs:
            q = q_ref[0, qi * tq:(qi + 1) * tq]
            zero = jnp.zeros_like(q)
            qs[qi] = (jnp.where(lane < DIFF_QK, q, zero), jnp.where(lane < DIFF_QK, zero, q))
        return qs[qi]

    def scores_into(slot, qi, j):
        q_from = max(0, j * tk - qi * tq)
        ks = k_ref[0, j * tk:(j + 1) * tk, :]
        for c in range(2):
            s_ref[slot, c, :, q_from:] = lax.dot_general(ks, masked_q(qi)[c][q_from:], _NT,
                                                         preferred_element_type=F32)

    def consume(slot, qi, j):
        off = j * tk - qi * tq
        for c, w in strips:
            nk = min(tk, max(0, (w + 1) * wq - off))
            if nk == 0:
                continue
            cols = slice(w * wq, (w + 1) * wq)
            s = s_ref[slot, c, 0:nk, cols]
            if off + nk - 1 > w * wq:
                key = lax.broadcasted_iota(jnp.int32, (nk, wq), 0) + off
                qry = lax.broadcasted_iota(jnp.int32, (nk, wq), 1) + w * wq
                s = jnp.where(key <= qry, s, NEG_BIG)
            m = m_ref[qi, c, :, cols]
            m_new = jnp.maximum(m, jnp.max(s, axis=0, keepdims=True))
            alpha = jnp.exp2(m - m_new)
            pr = jnp.exp2(s - m_new)
            m_ref[qi, c, :, cols] = m_new
            vt = jnp.concatenate([vt_ref[0, :, j * tk:j * tk + nk], ones_rows[:, :nk]], axis=0)
            acc_ref[qi, c, :, cols] = alpha * acc_ref[qi, c, :, cols] + _dot(vt, pr.astype(BF16))

    def finish(qi):
        l1, l2 = (acc_ref[qi, c, DIFF_V:DIFF_V + 1] for c in range(2))
        o = (acc_ref[qi, 0, :DIFF_V] * (1.0 / l1)
             - lam * (acc_ref[qi, 1, :DIFF_V] * (1.0 / l2)))
        o = o * lax.rsqrt(jnp.mean(o * o, axis=0, keepdims=True) + SUBLN_EPS) * sw_ref[...]
        o = o * (1.0 - lambda_init)
        o_ref[0, qi * tq:(qi + 1) * tq] = o.T.astype(o_ref.dtype)

    acc_ref[...] = jnp.zeros_like(acc_ref)
    m_ref[...] = jnp.full_like(m_ref, NEG_BIG)
    scores_into(0, *items[0])
    for n, (qi, j) in enumerate(items):
        if n + 1 < len(items):
            scores_into((n + 1) % 2, *items[n + 1])
        consume(n % 2, qi, j)
        if j == kpq * (qi + 1) - 1:
            finish(qi)


def _attn(q, k, vt, lq1, lk1, lq2, lk2, subln_w, *, tq, tk, wq, lambda_init):
    b, t, w = q.shape
    heads = w // DIFF_V
    nq = t // tq
    lam_spec = pl.BlockSpec((1, DIFF_QK), lambda i, h: (0, 0))
    return pl.pallas_call(
        functools.partial(_attn_body, tq=tq, tk=tk, wq=wq, lambda_init=lambda_init),
        grid=(b, heads),
        in_specs=[lam_spec] * 4 + [
            pl.BlockSpec((1, t, DIFF_V), lambda i, h: (i, 0, h)),
            pl.BlockSpec((1, t, DIFF_V), lambda i, h: (i, 0, h)),
            pl.BlockSpec((1, DIFF_V, t), lambda i, h: (i, h, 0)),
            pl.BlockSpec((DIFF_V, 1), lambda i, h: (0, 0)),
        ],
        out_specs=pl.BlockSpec((1, t, DIFF_V), lambda i, h: (i, 0, h)),
        out_shape=jax.ShapeDtypeStruct((b, t, w), BF16),
        scratch_shapes=[pltpu.VMEM((nq, 2, DIFF_V + BF16_ROWS, tq), F32),
                        pltpu.VMEM((2, 2, tk, tq), F32), pltpu.VMEM((nq, 2, 1, tq), F32)],
        compiler_params=pltpu.CompilerParams(
            dimension_semantics=("parallel", "parallel"), vmem_limit_bytes=VMEM_LIMIT),
        name="attn",
    )(lq1.reshape(1, -1), lk1.reshape(1, -1), lq2.reshape(1, -1), lk2.reshape(1, -1),
      q, k, vt, subln_w.reshape(-1, 1))


def _pick(n, pref):
    return pref if n % pref == 0 else n


def _layer(x, l, p, *, chunk=64, tb_rwkv=512, tm_ffn=1024, tf=256, tm_proj=512, tq=1024,
           tk=512, wq_attn=256):
    b, t, d = x.shape
    m = b * t
    lambda_init = 0.8 - 0.6 * math.exp(-0.3 * l)
    tm_ffn = _pick(m, tm_ffn)
    tf = _pick(p["ffn1_w_gate"].shape[-1], tf)
    x = _ffn(x.reshape(m, d), p["ffn1_pre_g"][l], p["ffn1_post_g"][l], p["ffn1_w_gate"][l],
             p["ffn1_w_up"][l], p["ffn1_w_down"][l], tm=tm_ffn, tf=tf)
    outs = _proj(x.reshape(b, t, d), p["mix_pre_g"][l], p["w_in"][l], p["shift_mu"][l],
                 p["rwkv_w_up"][l], p["rwkv_a_up"][l], p["rwkv_g_up"][l], p["rwkv_w0"][l],
                 p["rwkv_a0"][l], p["rwkv_k_k"][l], p["rwkv_k_a"][l], p["rwkv_r_k"][l].reshape(-1),
                 tm=_pick(t, tm_proj), chunk=chunk)
    y_rwkv = _rwkv(*outs[:10], p["rwkv_gn_w"][l], p["rwkv_gn_b"][l], chunk=chunk,
                   tb=_pick(t, tb_rwkv))
    y_diff = _attn(*outs[10:], p["diff_lam_q1"][l], p["diff_lam_k1"][l], p["diff_lam_q2"][l],
                   p["diff_lam_k2"][l], p["diff_subln_w"][l], tq=_pick(t, tq), tk=_pick(t, tk), wq=wq_attn,
                   lambda_init=lambda_init)
    x = _ffn(x, p["ffn2_pre_g"][l], p["ffn2_post_g"][l], p["ffn2_w_gate"][l], p["ffn2_w_up"][l],
             p["ffn2_w_down"][l], tm=tm_ffn, tf=tf,
             mix=(y_rwkv.reshape(m, -1), y_diff.reshape(m, -1), p["w_o"][l], p["mix_post_g"][l]))
    return x.reshape(b, t, d)


def kernel(x, ffn1_pre_g, ffn1_post_g, ffn1_w_gate, ffn1_w_up, ffn1_w_down, mix_pre_g, mix_post_g,
           w_in, shift_mu, w_o, rwkv_w0, rwkv_w_up, rwkv_a0, rwkv_a_up, rwkv_g_up, rwkv_k_k,
           rwkv_k_a, rwkv_r_k, rwkv_gn_w, rwkv_gn_b, diff_lam_q1, diff_lam_k1, diff_lam_q2,
           diff_lam_k2, diff_subln_w, ffn2_pre_g, ffn2_post_g, ffn2_w_gate, ffn2_w_up, ffn2_w_down):
    p = dict(ffn1_pre_g=ffn1_pre_g, ffn1_post_g=ffn1_post_g, ffn1_w_gate=ffn1_w_gate,
             ffn1_w_up=ffn1_w_up, ffn1_w_down=ffn1_w_down, mix_pre_g=mix_pre_g,
             mix_post_g=mix_post_g, w_in=w_in, shift_mu=shift_mu, w_o=w_o, rwkv_w0=rwkv_w0,
             rwkv_w_up=rwkv_w_up, rwkv_a0=rwkv_a0, rwkv_a_up=rwkv_a_up, rwkv_g_up=rwkv_g_up,
             rwkv_k_k=rwkv_k_k, rwkv_k_a=rwkv_k_a, rwkv_r_k=rwkv_r_k, rwkv_gn_w=rwkv_gn_w,
             rwkv_gn_b=rwkv_gn_b, diff_lam_q1=diff_lam_q1, diff_lam_k1=diff_lam_k1,
             diff_lam_q2=diff_lam_q2, diff_lam_k2=diff_lam_k2, diff_subln_w=diff_subln_w,
             ffn2_pre_g=ffn2_pre_g, ffn2_post_g=ffn2_post_g, ffn2_w_gate=ffn2_w_gate,
             ffn2_w_up=ffn2_w_up, ffn2_w_down=ffn2_w_down)
    for l in range(ffn1_pre_g.shape[0]):
        x = _layer(x, l, p)
    return x
```

```python
import functools
import math

import jax
import jax.numpy as jnp
from jax import lax
from jax.experimental import pallas as pl
from jax.experimental.pallas import tpu as pltpu

F32 = jnp.float32
BF16 = jnp.bfloat16

RWKV_HEADS = 8
RWKV_HEAD = 64
RWKV_W = RWKV_HEADS * RWKV_HEAD
DIFF_HEADS = 4
DIFF_QK = 64
DIFF_V = 2 * DIFF_QK
DIFF_W = DIFF_HEADS * DIFF_V
DECAY_LORA = 64
ICLR_LORA = 64
GATE_LORA = 160
LORA_W = DECAY_LORA + ICLR_LORA + GATE_LORA
LANES = 128
SUBLANES = 8
BF16_ROWS = 16
FFN_PARTS = 2
RWKV_GROUP = 4
LORA_PAD = -(-LORA_W // LANES) * LANES
SHIFT_COLS = 3 * RWKV_W + LORA_W
SHIFT_PAD = 3 * RWKV_W + LORA_PAD
FFN_RES = 0.5
NORM_EPS = 1e-6
GN_EPS = 64e-5
SUBLN_EPS = 1e-5
NEG_BIG = -1e30
LOG2E = math.log2(math.e)
EXP_NEG_HALF = math.exp(-0.5)
VMEM_LIMIT = 56 * 1024 * 1024

_NT = (((1,), (1,)), ((), ()))
_TN = (((0,), (0,)), ((), ()))


def _dot(a, b):
    return jnp.dot(a, b, preferred_element_type=F32)


def _split_dot_k2(x, w2_bf16):
    hi = x.astype(BF16)
    lo = (x - hi.astype(F32)).astype(BF16)
    return _dot(jnp.concatenate([hi, lo], axis=1), w2_bf16)


def _rms(x, g, eps):
    return x * lax.rsqrt(jnp.mean(x * x, axis=-1, keepdims=True) + eps) * g


def _ffn_body(*refs, tf, mix):
    if mix:
        x_ref, yr_ref, yd_ref, wo_ref, gmix_ref, *refs = refs
    else:
        x_ref, *refs = refs
    gpre_ref, gpost_ref, wg_ref, wu_ref, wd_ref, o_ref, a_ref = refs
    part = x_ref.shape[0] // FFN_PARTS
    parts = [slice(r * part, (r + 1) * part) for r in range(FFN_PARTS)]
    xs, hs = [], []
    for rows in parts:
        x = x_ref[rows]
        if mix:
            nr = yr_ref.shape[1]
            y = _dot(yr_ref[rows], wo_ref[:nr]) + _dot(yd_ref[rows], wo_ref[nr:])
            x = x + _rms(y, gmix_ref[...], NORM_EPS)
        xs.append(x)
        hs.append(_rms(x, gpre_ref[...], NORM_EPS).astype(BF16))
    for rows, x, h in zip(parts, xs, hs):
        for f0 in range(0, wg_ref.shape[1], tf):
            cols = slice(f0, f0 + tf)
            g = _dot(h, wg_ref[:, cols])
            u = _dot(h, wu_ref[:, cols])
            a_ref[rows, cols] = (g * jax.nn.sigmoid(g) * u).astype(BF16)
        y = _dot(a_ref[rows], wd_ref[...])
        o_ref[rows] = x + FFN_RES * _rms(y, gpost_ref[...], NORM_EPS)


def _ffn(x, g_pre, g_post, w_gate, w_up, w_down, *, tm, tf, mix=None):
    m, d = x.shape
    f = w_gate.shape[1]
    rows = lambda width: pl.BlockSpec((tm, width), lambda i: (i, 0))
    resident = lambda shape: pl.BlockSpec(shape, lambda i: (0, 0), pipeline_mode=pl.Buffered(1))
    args, specs = [x], [rows(d)]
    if mix is not None:
        y_rwkv, y_diff, w_o, g_mix = mix
        args += [y_rwkv, y_diff, w_o.astype(BF16), g_mix.reshape(1, d)]
        specs += [rows(y_rwkv.shape[1]), rows(y_diff.shape[1]), resident(w_o.shape), resident((1, d))]
    args += [g_pre.reshape(1, d), g_post.reshape(1, d), w_gate.astype(BF16), w_up.astype(BF16),
             w_down.astype(BF16)]
    specs += [resident((1, d)), resident((1, d)), resident((d, f)), resident((d, f)),
              resident((f, d))]
    return pl.pallas_call(
        functools.partial(_ffn_body, tf=tf, mix=mix is not None),
        grid=(m // tm,),
        in_specs=specs,
        out_specs=rows(d),
        out_shape=jax.ShapeDtypeStruct((m, d), F32),
        scratch_shapes=[pltpu.VMEM((tm, f), BF16)],
        compiler_params=pltpu.CompilerParams(
            dimension_semantics=("parallel",), vmem_limit_bytes=VMEM_LIMIT),
        name="ffn_mix" if mix is not None else "ffn",
    )(*args)


def _proj_body(x_ref, g_ref, wrkv_ref, wlora_ref, wdiff_ref, mu_ref, wup_ref, ones_ref, w0_ref,
               a0_ref, kk_ref, ka_ref, rk_ref,
               at_out, rt_out, bt_out, kt_out, bh_out, kh_out, v_out, gt_out, bonus_out, gate_out,
               qd_out, kd_out, vd_out, carry_ref, *, chunk):
    @pl.when(pl.program_id(1) == 0)
    def _():
        carry_ref[...] = jnp.zeros_like(carry_ref)

    tm = x_ref.shape[1]
    nrkv = 3 * RWKV_W
    first_row = lax.broadcasted_iota(jnp.int32, (SUBLANES, 1), 0) == 0
    lane = lax.broadcasted_iota(jnp.int32, (1, LORA_PAD), 1)
    ones_bd = ones_ref[...]

    def shift_mix(p, cols):
        prev = pltpu.roll(p, 1, 0)
        prev = jnp.concatenate(
            [jnp.where(first_row, carry_ref[:, cols], prev[:SUBLANES]), prev[SUBLANES:]], axis=0)
        carry_ref[:, cols] = p[tm - 1:]
        return p + (prev - p) * mu_ref[:, cols]

    h = _rms(x_ref[0], g_ref[...], NORM_EPS).astype(BF16)
    z = shift_mix(_dot(h, wlora_ref[...]), slice(nrkv, SHIFT_PAD))
    p_k = shift_mix(_dot(h, wrkv_ref[:, RWKV_W:2 * RWKV_W]), slice(RWKV_W, 2 * RWKV_W))
    act = jnp.where(lane < DECAY_LORA, jnp.tanh(z),
                    jnp.where(lane < DECAY_LORA + ICLR_LORA, z, jax.nn.sigmoid(z))).astype(BF16)
    up = _dot(act[:, :LANES], wup_ref[:LANES, :2 * RWKV_W])
    gate_out[0] = _dot(act[:, LANES:], wup_ref[LANES:, 2 * RWKV_W:])
    p_r = shift_mix(_dot(h, wrkv_ref[:, :RWKV_W]), slice(0, RWKV_W))
    p_v = shift_mix(_dot(h, wrkv_ref[:, 2 * RWKV_W:]), slice(2 * RWKV_W, nrkv))
    p_diff = _dot(h, wdiff_ref[...])

    ld = -EXP_NEG_HALF * jax.nn.sigmoid(w0_ref[...] + up[:, :RWKV_W])
    iclr = jax.nn.sigmoid(a0_ref[...] + up[:, RWKV_W:])
    kk = p_k * kk_ref[...]
    kk = kk * lax.rsqrt(jnp.maximum(_dot((kk * kk).astype(BF16), ones_bd), 1e-24))
    k = p_k * (iclr * ka_ref[...] + (1.0 - ka_ref[...]))
    b = kk * iclr
    v_out[0] = p_v.astype(BF16)
    bonus_out[0] = _dot((p_r * k * rk_ref[...]).astype(BF16), ones_bd) * p_v

    ri = lax.broadcasted_iota(jnp.int32, (chunk, chunk), 0)
    ci = lax.broadcasted_iota(jnp.int32, (chunk, chunk), 1)
    tril3 = jnp.concatenate([(ci <= ri).astype(BF16)] * 3, axis=1)
    ld1 = ld.astype(BF16)
    res = ld - ld1.astype(F32)
    ld2 = res.astype(BF16)
    ld3 = (res - ld2.astype(F32)).astype(BF16)
    cums, tots = [], []
    for i in range(tm // chunk):
        rows = slice(i * chunk, (i + 1) * chunk)
        cum = _dot(tril3, jnp.concatenate([ld1[rows], ld2[rows], ld3[rows]], axis=0))
        cums.append(cum)
        tots.append(jnp.broadcast_to(cum[chunk - 1:], cum.shape))
    gt_out[0] = jnp.exp(jnp.concatenate([t[:1] for t in tots], axis=0))
    cum = jnp.concatenate(cums, axis=0)
    g_inv = jnp.exp(-cum)
    g_rest = jnp.exp(jnp.concatenate(tots, axis=0) - cum)
    at_out[0] = (-kk * jnp.exp(cum - ld)).astype(BF16)
    rt_out[0] = (p_r * jnp.exp(cum)).astype(BF16)
    bt_out[0] = (b * g_inv).astype(BF16)
    kt_out[0] = (k * g_inv).astype(BF16)
    bh_out[0] = (b * g_rest).astype(BF16)
    kh_out[0] = (k * g_rest).astype(BF16)

    qd_out[0] = (p_diff[:, :DIFF_W] * (DIFF_QK ** -0.5 * LOG2E)).astype(BF16)
    kd_out[0] = p_diff[:, DIFF_W:2 * DIFF_W].astype(BF16)
    vd_out[0] = p_diff[:, 2 * DIFF_W:].T.astype(BF16)


def _head_ones(width, head):
    i = jnp.arange(width) // head
    return (i[:, None] == i[None, :]).astype(BF16)


def _proj(x, g, w_in, shift_mu, w_up, a_up, g_up, w0, a0, k_k, k_a, r_k, *, tm, chunk):
    b, t, d = x.shape
    pad = SHIFT_PAD - SHIFT_COLS
    nrkv = 3 * RWKV_W
    w = w_in.astype(BF16)
    w_lora = jnp.pad(w[:, nrkv:SHIFT_COLS], ((0, 0), (0, pad)))
    w_diff = w[:, SHIFT_COLS:]
    mu = jnp.pad(shift_mu, (0, pad)).reshape(1, SHIFT_PAD)
    wup = jnp.zeros((LORA_PAD, 3 * RWKV_W), F32)
    wup = wup.at[:DECAY_LORA, :RWKV_W].set(w_up)
    wup = wup.at[DECAY_LORA:DECAY_LORA + ICLR_LORA, RWKV_W:2 * RWKV_W].set(a_up)
    wup = wup.at[DECAY_LORA + ICLR_LORA:LORA_W, 2 * RWKV_W:].set(g_up)
    wup = wup.astype(BF16)
    vec = lambda a: a.reshape(1, RWKV_W)
    const = lambda shape: pl.BlockSpec(shape, lambda i, j: (0,) * len(shape))
    tile = lambda width: pl.BlockSpec((1, tm, width), lambda i, j: (i, j, 0))
    f32_out = jax.ShapeDtypeStruct((b, t, RWKV_W), F32)
    bf_out = jax.ShapeDtypeStruct((b, t, DIFF_W), BF16)
    assert RWKV_W == DIFF_W
    return pl.pallas_call(
        functools.partial(_proj_body, chunk=chunk),
        grid=(b, t // tm),
        in_specs=[tile(d), const((1, d)), const((d, nrkv)), const((d, LORA_PAD)),
                  const((d, 3 * DIFF_W)), const((1, SHIFT_PAD)),
                  const((LORA_PAD, 3 * RWKV_W)), const((RWKV_W, RWKV_W))] + [const((1, RWKV_W))] * 5,
        out_specs=[tile(RWKV_W)] * 7 + [pl.BlockSpec((1, tm // chunk, RWKV_W), lambda i, j: (i, j, 0))]
        + [tile(RWKV_W)] * 2 + [tile(DIFF_W)] * 2
        + [pl.BlockSpec((1, DIFF_W, tm), lambda i, j: (i, 0, j))],
        out_shape=[bf_out] * 7 + [jax.ShapeDtypeStruct((b, t // chunk, RWKV_W), F32)]
        + [f32_out] * 2 + [bf_out] * 2 + [jax.ShapeDtypeStruct((b, DIFF_W, t), BF16)],
        scratch_shapes=[pltpu.VMEM((1, SHIFT_PAD), F32)],
        compiler_params=pltpu.CompilerParams(
            dimension_semantics=("arbitrary", "arbitrary"), vmem_limit_bytes=VMEM_LIMIT),
        name="proj",
    )(x, g.reshape(1, d), w, w_lora, w_diff, mu, wup, _head_ones(RWKV_W, RWKV_HEAD),
      vec(w0), vec(a0), vec(k_k), vec(k_a), vec(r_k))


def _rwkv_body(at_ref, rt_ref, bt_ref, kt_ref, bh_ref, kh_ref, v_ref, gt_ref, bonus_ref, gate_ref,
               gnw_ref, gnb_ref, o_ref, h_ref, *, chunk, group_size):
    c = chunk
    c2 = 2 * c
    nc = at_ref.shape[1] // c

    @pl.when(pl.program_id(1) == 0)
    def _():
        h_ref[...] = jnp.zeros_like(h_ref)

    chunk_of = lambda ref: {i: ref[0, i * c:(i + 1) * c, :] for i in range(nc)}
    at, rt, bt, kt, bh, kh, v = (chunk_of(ref) for ref in
                                 (at_ref, rt_ref, bt_ref, kt_ref, bh_ref, kh_ref, v_ref))
    g_tot = {i: gt_ref[0, i:i + 1, :] for i in range(nc)}

    lane = lax.broadcasted_iota(jnp.int32, (1, LANES), 1)
    lo = lane < RWKV_HEAD

    def stack(x):
        zero = jnp.zeros_like(x)
        return jnp.concatenate([jnp.where(lo, x, zero), jnp.where(lo, zero, x)], axis=0)

    row2 = lax.broadcasted_iota(jnp.int32, (c2, c2), 0)
    col2 = lax.broadcasted_iota(jnp.int32, (c2, c2), 1)
    same = (row2 // c) == (col2 // c)
    strict = same & (col2 < row2)
    incl = same & (col2 <= row2)
    eye = (row2 == col2).astype(F32)
    levels = int(math.log2(c))
    level_masks = [((row2 >> l) == (col2 >> l)) & ((row2 >> (l - 1)) != (col2 >> (l - 1)))
                   & (col2 < row2) for l in range(1, levels + 1)]
    ones_bd = ((lax.broadcasted_iota(jnp.int32, (LANES, LANES), 0) // RWKV_HEAD)
               == (lax.broadcasted_iota(jnp.int32, (LANES, LANES), 1) // RWKV_HEAD)).astype(BF16)
    ones_bd2 = jnp.concatenate([ones_bd, ones_bd], axis=0)
    nt = lambda x, y: lax.dot_general(x, y, _NT, preferred_element_type=F32)
    tn = lambda x, y: lax.dot_general(x, y, _TN, preferred_element_type=F32)

    npairs = RWKV_HEADS // 2
    sls = [slice(p * LANES, (p + 1) * LANES) for p in range(npairs)]
    cat0 = lambda *xs: jnp.concatenate(xs, axis=0)
    cat1 = lambda *xs: jnp.concatenate(xs, axis=1)
    assert c2 == LANES
    qp, y0, gm, g_col = {}, {}, {}, {}

    for group in [range(g, min(g + group_size, nc)) for g in range(0, nc, group_size)]:
        chains = [(i, p) for i in group for p in range(npairs)]
        each = lambda f: {ip: f(ip) for ip in chains}
        pick = lambda xs: each(lambda ip: stack(xs[ip[0]][:, sls[ip[1]]]))
        a2, r2, b2, k2, bh2, kh2, v2 = (pick(xs) for xs in (at, rt, bt, kt, bh, kh, v))

        s_ar = each(lambda ip: nt(cat0(a2[ip], r2[ip]), cat0(b2[ip], k2[ip])))
        a_ab = each(lambda ip: jnp.where(strict, s_ar[ip][:c2, :c2], 0.0))
        a_ak = each(lambda ip: jnp.where(strict, s_ar[ip][:c2, c2:], 0.0).astype(BF16))
        a_rb = each(lambda ip: jnp.where(incl, s_ar[ip][c2:, :c2], 0.0).astype(BF16))
        a_rk = each(lambda ip: jnp.where(incl, s_ar[ip][c2:, c2:], 0.0).astype(BF16))

        tinv = each(lambda ip: eye + jnp.where(level_masks[0], a_ab[ip], 0.0))
        for l in range(2, levels + 1):
            e = each(lambda ip: jnp.where(level_masks[l - 1], a_ab[ip], 0.0).astype(BF16))
            tb = each(lambda ip: tinv[ip].astype(BF16))
            te = each(lambda ip: _dot(tb[ip], e[ip]).astype(BF16))
            tinv = each(lambda ip: tinv[ip] + _dot(te[ip], tb[ip]))
        tb = each(lambda ip: tinv[ip].astype(BF16))

        av = each(lambda ip: _dot(cat0(a_ak[ip], a_rk[ip]), v2[ip]))
        akv = each(lambda ip: av[ip][:c2].astype(BF16))
        wu = each(lambda ip: _dot(tb[ip], cat1(a2[ip], akv[ip])).astype(BF16))
        pg = each(lambda ip: tn(bh2[ip], wu[ip]))
        qy = each(lambda ip: _dot(a_rb[ip], wu[ip]))
        gm.update(each(lambda ip: pg[ip][:, LANES:] + tn(kh2[ip], v2[ip])))
        qp.update(each(lambda ip: cat0((r2[ip].astype(F32) + qy[ip][:, :LANES]).astype(BF16),
                                       pg[ip][:, :LANES].astype(BF16))))
        y0.update(each(lambda ip: qy[ip][:, LANES:] + av[ip][c2:]))
        g_col.update(each(
            lambda ip: jnp.sum(eye * g_tot[ip[0]][:, sls[ip[1]]], axis=1, keepdims=True)))

    state = [h_ref[p] for p in range(npairs)]
    ys = [[] for _ in range(npairs)]
    for i in range(nc):
        for p in range(npairs):
            qh = _dot(qp[i, p], state[p].astype(BF16))
            y2 = qh[:c2] + y0[i, p]
            ys[p].append(y2[:c] + y2[c:])
            state[p] = g_col[i, p] * state[p] + qh[c2:] + gm[i, p]
    for p in range(npairs):
        h_ref[p] = state[p]

    for p in range(npairs):
        sl = sls[p]
        y = cat0(*ys[p])
        mean = _split_dot_k2(y, ones_bd2) * (1.0 / RWKV_HEAD)
        yc = y - mean
        var = _split_dot_k2(yc * yc, ones_bd2) * (1.0 / RWKV_HEAD)
        yn = yc * lax.rsqrt(var + GN_EPS) * gnw_ref[:, sl] + gnb_ref[:, sl]
        o_ref[0, :, sl] = ((yn + bonus_ref[0, :, sl]) * gate_ref[0, :, sl]).astype(o_ref.dtype)


def _rwkv(at, rt, bt, kt, bh, kh, v, gt, bonus, gate, gn_w, gn_b, *, chunk, tb):
    b, t, w = at.shape
    tile = pl.BlockSpec((1, tb, w), lambda i, j: (i, j, 0))
    const = pl.BlockSpec((1, w), lambda i, j: (0, 0))
    return pl.pallas_call(
        functools.partial(_rwkv_body, chunk=chunk, group_size=RWKV_GROUP),
        grid=(b, t // tb),
        in_specs=[tile] * 7 + [pl.BlockSpec((1, tb // chunk, w), lambda i, j: (i, j, 0))]
        + [tile] * 2 + [const] * 2,
        out_specs=tile,
        out_shape=jax.ShapeDtypeStruct((b, t, w), BF16),
        scratch_shapes=[pltpu.VMEM((RWKV_HEADS // 2, LANES, LANES), F32)],
        compiler_params=pltpu.CompilerParams(
            dimension_semantics=("arbitrary", "arbitrary"), vmem_limit_bytes=VMEM_LIMIT),
        name="rwkv",
    )(at, rt, bt, kt, bh, kh, v, gt, bonus, gate, gn_w.reshape(1, w), gn_b.reshape(1, w))


def _attn_body(lq1_ref, lk1_ref, lq2_ref, lk2_ref, q_ref, k_ref, vt_ref, sw_ref, o_ref, acc_ref,
               s_ref, m_ref, *, tq, tk, wq, lambda_init):
    nq = q_ref.shape[1] // tq
    kpq = tq // tk
    lane = lax.broadcasted_iota(jnp.int32, (1, LANES), 1)
    lam = (jnp.exp(jnp.sum(lq1_ref[...] * lk1_ref[...], axis=-1, keepdims=True))
           - jnp.exp(jnp.sum(lq2_ref[...] * lk2_ref[...], axis=-1, keepdims=True)) + lambda_init)
    strips = [(c, w) for c in range(2) for w in range(tq // wq)]
    items = [(qi, j) for qi in range(nq) for j in range(kpq * (qi + 1))]
    ones_rows = jnp.ones((BF16_ROWS, tk), BF16)
    qs = {}

    def masked_q(qi):
        if qi not in qs:
            q = q_ref[0, qi * tq:(qi + 1) * tq]
            zero = jnp.zeros_like(q)
            qs[qi] = (jnp.where(lane < DIFF_QK, q, zero), jnp.where(lane < DIFF_QK, zero, q))
        return qs[qi]

    def scores_into(slot, qi, j):
        q_from = max(0, j * tk - qi * tq)
        ks = k_ref[0, j * tk:(j + 1) * tk, :]
        for c in range(2):
            s_ref[slot, c, :, q_from:] = lax.dot_general(ks, masked_q(qi)[c][q_from:], _NT,
                                                         preferred_element_type=F32)

    def consume(slot, qi, j):
        off = j * tk - qi * tq
        for c, w in strips:
            nk = min(tk, max(0, (w + 1) * wq - off))
            if nk == 0:
                continue
            cols = slice(w * wq, (w + 1) * wq)
            s = s_ref[slot, c, 0:nk, cols]
            if off + nk - 1 > w * wq:
                key = lax.broadcasted_iota(jnp.int32, (nk, wq), 0) + off
                qry = lax.broadcasted_iota(jnp.int32, (nk, wq), 1) + w * wq
                s = jnp.where(key <= qry, s, NEG_BIG)
            m = m_ref[qi, c, :, cols]
            m_new = jnp.maximum(m, jnp.max(s, axis=0, keepdims=True))
            alpha = jnp.exp2(m - m_new)
            pr = jnp.exp2(s - m_new)
            m_ref[qi, c, :, cols] = m_new
            vt = jnp.concatenate([vt_ref[0, :, j * tk:j * tk + nk], ones_rows[:, :nk]], axis=0)
            acc_ref[qi, c, :, cols] = alpha * acc_ref[qi, c, :, cols] + _dot(vt, pr.astype(BF16))

    def finish(qi):
        l1, l2 = (acc_ref[qi, c, DIFF_V:DIFF_V + 1] for c in range(2))
        o = (acc_ref[qi, 0, :DIFF_V] * (1.0 / l1)
             - lam * (acc_ref[qi, 1, :DIFF_V] * (1.0 / l2)))
        o = o * lax.rsqrt(jnp.mean(o * o, axis=0, keepdims=True) + SUBLN_EPS) * sw_ref[...]
        o = o * (1.0 - lambda_init)
        o_ref[0, qi * tq:(qi + 1) * tq] = o.T.astype(o_ref.dtype)

    acc_ref[...] = jnp.zeros_like(acc_ref)
    m_ref[...] = jnp.full_like(m_ref, NEG_BIG)
    scores_into(0, *items[0])
    for n, (qi, j) in enumerate(items):
        if n + 1 < len(items):
            scores_into((n + 1) % 2, *items[n + 1])
        consume(n % 2, qi, j)
        if j == kpq * (qi + 1) - 1:
            finish(qi)


def _attn(q, k, vt, lq1, lk1, lq2, lk2, subln_w, *, tq, tk, wq, lambda_init):
    b, t, w = q.shape
    heads = w // DIFF_V
    nq = t // tq
    lam_spec = pl.BlockSpec((1, DIFF_QK), lambda i, h: (0, 0))
    return pl.pallas_call(
        functools.partial(_attn_body, tq=tq, tk=tk, wq=wq, lambda_init=lambda_init),
        grid=(b, heads),
        in_specs=[lam_spec] * 4 + [
            pl.BlockSpec((1, t, DIFF_V), lambda i, h: (i, 0, h)),
            pl.BlockSpec((1, t, DIFF_V), lambda i, h: (i, 0, h)),
            pl.BlockSpec((1, DIFF_V, t), lambda i, h: (i, h, 0)),
            pl.BlockSpec((DIFF_V, 1), lambda i, h: (0, 0)),
        ],
        out_specs=pl.BlockSpec((1, t, DIFF_V), lambda i, h: (i, 0, h)),
        out_shape=jax.ShapeDtypeStruct((b, t, w), BF16),
        scratch_shapes=[pltpu.VMEM((nq, 2, DIFF_V + BF16_ROWS, tq), F32),
                        pltpu.VMEM((2, 2, tk, tq), F32), pltpu.VMEM((nq, 2, 1, tq), F32)],
        compiler_params=pltpu.CompilerParams(
            dimension_semantics=("parallel", "parallel"), vmem_limit_bytes=VMEM_LIMIT),
        name="attn",
    )(lq1.reshape(1, -1), lk1.reshape(1, -1), lq2.reshape(1, -1), lk2.reshape(1, -1),
      q, k, vt, subln_w.reshape(-1, 1))


def _pick(n, pref):
    return pref if n % pref == 0 else n


def _layer(x, l, p, *, chunk=64, tb_rwkv=512, tm_ffn=1024, tf=256, tm_proj=512, tq=2048,
           tk=512, wq_attn=256):
    b, t, d = x.shape
    m = b * t
    lambda_init = 0.8 - 0.6 * math.exp(-0.3 * l)
    tm_ffn = _pick(m, tm_ffn)
    tf = _pick(p["ffn1_w_gate"].shape[-1], tf)
    x = _ffn(x.reshape(m, d), p["ffn1_pre_g"][l], p["ffn1_post_g"][l], p["ffn1_w_gate"][l],
             p["ffn1_w_up"][l], p["ffn1_w_down"][l], tm=tm_ffn, tf=tf)
    outs = _proj(x.reshape(b, t, d), p["mix_pre_g"][l], p["w_in"][l], p["shift_mu"][l],
                 p["rwkv_w_up"][l], p["rwkv_a_up"][l], p["rwkv_g_up"][l], p["rwkv_w0"][l],
                 p["rwkv_a0"][l], p["rwkv_k_k"][l], p["rwkv_k_a"][l], p["rwkv_r_k"][l].reshape(-1),
                 tm=_pick(t, tm_proj), chunk=chunk)
    y_rwkv = _rwkv(*outs[:10], p["rwkv_gn_w"][l], p["rwkv_gn_b"][l], chunk=chunk,
                   tb=_pick(t, tb_rwkv))
    y_diff = _attn(*outs[10:], p["diff_lam_q1"][l], p["diff_lam_k1"][l], p["diff_lam_q2"][l],
                   p["diff_lam_k2"][l], p["diff_subln_w"][l], tq=_pick(t, tq), tk=_pick(t, tk), wq=wq_attn,
                   lambda_init=lambda_init)
    x = _ffn(x, p["ffn2_pre_g"][l], p["ffn2_post_g"][l], p["ffn2_w_gate"][l], p["ffn2_w_up"][l],
             p["ffn2_w_down"][l], tm=tm_ffn, tf=tf,
             mix=(y_rwkv.reshape(m, -1), y_diff.reshape(m, -1), p["w_o"][l], p["mix_post_g"][l]))
    return x.reshape(b, t, d)


def kernel(x, ffn1_pre_g, ffn1_post_g, ffn1_w_gate, ffn1_w_up, ffn1_w_down, mix_pre_g, mix_post_g,
           w_in, shift_mu, w_o, rwkv_w0, rwkv_w_up, rwkv_a0, rwkv_a_up, rwkv_g_up, rwkv_k_k,
           rwkv_k_a, rwkv_r_k, rwkv_gn_w, rwkv_gn_b, diff_lam_q1, diff_lam_k1, diff_lam_q2,
           diff_lam_k2, diff_subln_w, ffn2_pre_g, ffn2_post_g, ffn2_w_gate, ffn2_w_up, ffn2_w_down):
    p = dict(ffn1_pre_g=ffn1_pre_g, ffn1_post_g=ffn1_post_g, ffn1_w_gate=ffn1_w_gate,
             ffn1_w_up=ffn1_w_up, ffn1_w_down=ffn1_w_down, mix_pre_g=mix_pre_g,
             mix_post_g=mix_post_g, w_in=w_in, shift_mu=shift_mu, w_o=w_o, rwkv_w0=rwkv_w0,
             rwkv_w_up=rwkv_w_up, rwkv_a0=rwkv_a0, rwkv_a_up=rwkv_a_up, rwkv_g_up=rwkv_g_up,
             rwkv_k_k=rwkv_k_k, rwkv_k_a=rwkv_k_a, rwkv_r_k=rwkv_r_k, rwkv_gn_w=rwkv_gn_w,
             rwkv_gn_b=rwkv_gn_b, diff_lam_q1=diff_lam_q1, diff_lam_k1=diff_lam_k1,
             diff_lam_q2=diff_lam_q2, diff_lam_k2=diff_lam_k2, diff_subln_w=diff_subln_w,
             ffn2_pre_g=ffn2_pre_g, ffn2_post_g=ffn2_post_g, ffn2_w_gate=ffn2_w_gate,
             ffn2_w_up=ffn2_w_up, ffn2_w_down=ffn2_w_down)
    for l in range(ffn1_pre_g.shape[0]):
        x = _layer(x, l, p)
    return x
```
